```python
import math
import jax
import jax.numpy as jnp
from jax import lax
import numpy as np

D_MODEL = 1024
BATCH = 2
SEQ = 8192
DEPTH = 4

GDN_HEADS = 4
GDN_DK = 128
GDN_DV = 128
CONV_WIDTH = 4
GDN_CHUNK = 64
ATT_HEADS = 4
ATT_DH = 128
IDX_HEADS = 4
IDX_DH = 64
TOPK_MAX = 256
Q_BLOCK = 128
GLA_HEADS = 4
GLA_DK = D_MODEL // 2 // GLA_HEADS
GLA_DV = D_MODEL // GLA_HEADS
GLA_RANK = 16
GLA_TAU = 16.0
GLA_CHUNK = 64
N_GROUPS = 4
EXPERTS_PER_GROUP = 8
N_EXPERTS = N_GROUPS * EXPERTS_PER_GROUP
TOP_K_EXPERTS = 2
D_EXPERT = 512
MOE_BLOCK = 128
ROPE_THETA = 10000.0
EPS = 1e-6

GDN_QK = GDN_HEADS * GDN_DK
GDN_V = GDN_HEADS * GDN_DV
ATT_W = ATT_HEADS * ATT_DH
EVEN_SPLITS = [GDN_QK, GDN_QK, GDN_V, GDN_V, GDN_HEADS, GDN_HEADS,
               ATT_W, ATT_W, ATT_W, IDX_HEADS * IDX_DH, IDX_DH, IDX_HEADS]
EVEN_IN = int(sum(EVEN_SPLITS))
EVEN_MIX = GDN_V + ATT_W
ODD_SPLITS = [GLA_HEADS * GLA_DK, GLA_HEADS * GLA_DK, GLA_HEADS * GLA_DV,
              GLA_HEADS * GLA_DV, GLA_RANK]
ODD_IN = int(sum(ODD_SPLITS))
ODD_MIX = GLA_HEADS * GLA_DV

kernel_name = "hybrid_gdn_dsa_gla_hmoe_trunk"

F32 = jnp.float32


def split_cols(t, sizes):
    idx = np.cumsum(sizes)[:-1].tolist()
    return jnp.split(t, idx, axis=-1)


def rms_norm(t, g):
    tf = t.astype(F32)
    y = tf * lax.rsqrt(jnp.mean(tf * tf, axis=-1, keepdims=True) + EPS)
    return (y * g.astype(F32)).astype(t.dtype)


def l2_normalize(t):
    tf = t.astype(F32)
    return (tf * lax.rsqrt(jnp.sum(tf * tf, axis=-1, keepdims=True) + EPS)).astype(t.dtype)


def modulate(hn, shift, scale):
    return hn * (1 + scale[:, None, :]) + shift[:, None, :]


def rope_tables(positions, dim):
    inv = 1.0 / (ROPE_THETA ** (jnp.arange(0, dim, 2, dtype=F32) / dim))
    ang = positions.astype(F32)[..., None] * inv
    return jnp.cos(ang), jnp.sin(ang)


def apply_rope(t, cos, sin):
    t1, t2 = jnp.split(t.astype(F32), 2, axis=-1)
    c = cos[:, :, None, :]
    s = sin[:, :, None, :]
    return jnp.concatenate([t1 * c - t2 * s, t2 * c + t1 * s], axis=-1).astype(t.dtype)


def causal_depthwise_conv(t, w):
    width = w.shape[0]
    return lax.conv_general_dilated(
        t, w[:, None, :].astype(t.dtype), window_strides=(1,),
        padding=((width - 1, 0),), dimension_numbers=('NWC', 'WIO', 'NWC'),
        feature_group_count=t.shape[-1])


def to_chunks(t, chunk):
    b, s, h = t.shape[:3]
    t = t.reshape((b, s // chunk, chunk, h) + t.shape[3:])
    return t.transpose((1, 0, 3, 2) + tuple(range(4, t.ndim)))


def from_chunks(t):
    n, b, h, c = t.shape[:4]
    t = t.transpose((1, 0, 3, 2) + tuple(range(4, t.ndim)))
    return t.reshape((b, n * c, h) + t.shape[4:])


def gated_delta_rule(q, k, v, g, beta):
    dtype = v.dtype
    b_, _, h, dk = q.shape
    dv = v.shape[-1]
    c = GDN_CHUNK
    q = to_chunks(q.astype(F32) * dk ** -0.5, c)
    k = to_chunks(k.astype(F32), c)
    v = to_chunks(v.astype(F32), c)
    g = to_chunks(g.astype(F32), c)
    beta = to_chunks(beta.astype(F32), c)
    gc = jnp.cumsum(g, axis=-1)
    incl = jnp.tril(jnp.ones((c, c), bool))
    strict = jnp.tril(jnp.ones((c, c), bool), -1)
    decay = jnp.exp(jnp.where(incl, gc[..., :, None] - gc[..., None, :], -jnp.inf))
    kb = k * beta[..., None]
    lmat = jnp.where(strict, jnp.einsum('nbhid,nbhjd->nbhij', kb, k) * decay, 0.0)
    eye = jnp.eye(c, dtype=F32)
    tmat = lax.linalg.triangular_solve(eye + lmat, jnp.broadcast_to(eye, lmat.shape),
                                       left_side=True, lower=True, unit_diagonal=True)
    w = tmat @ (kb * jnp.exp(gc)[..., None])
    u = tmat @ (v * beta[..., None])
    aqk = jnp.einsum('nbhid,nbhjd->nbhij', q, k) * decay
    qg = q * jnp.exp(gc)[..., None]
    glast = gc[..., -1:]
    kd = k * jnp.exp(glast - gc)[..., None]

    def step(state, xs):
        qg_i, aqk_i, w_i, u_i, kd_i, gl_i = xs
        v_new = u_i - w_i @ state
        o = qg_i @ state + aqk_i @ v_new
        state = state * jnp.exp(gl_i)[..., None] + jnp.swapaxes(kd_i, -1, -2) @ v_new
        return state, o

    s0 = jnp.zeros((b_, h, dk, dv), F32)
    _, o = lax.scan(step, s0, (qg, aqk, w, u, kd, glast))
    return from_chunks(o).astype(dtype)


def gla_chunked(q, k, v, log_a):
    dtype = v.dtype
    b_, _, h, dk = q.shape
    dv = v.shape[-1]
    c = GLA_CHUNK
    q = to_chunks(q.astype(F32) * dk ** -0.5, c)
    k = to_chunks(k.astype(F32), c)
    v = to_chunks(v.astype(F32), c)
    bcum = jnp.cumsum(to_chunks(log_a.astype(F32), c), axis=-2)
    incl = jnp.tril(jnp.ones((c, c), bool))

    def step(state, xs):
        q_i, k_i, v_i, b_i = xs
        bl = b_i[..., -1:, :]
        o_inter = (q_i * jnp.exp(b_i)) @ state
        diff = jnp.where(incl[..., None], b_i[..., :, None, :] - b_i[..., None, :, :], -jnp.inf)
        att = jnp.einsum('bhid,bhjd,bhijd->bhij', q_i, k_i, jnp.exp(diff))
        o = o_inter + att @ v_i
        state = (state * jnp.swapaxes(jnp.exp(bl), -1, -2)
                 + jnp.swapaxes(k_i * jnp.exp(bl - b_i), -1, -2) @ v_i)
        return state, o

    s0 = jnp.zeros((b_, h, dk, dv), F32)
    _, o = lax.scan(step, s0, (q, k, v, bcum))
    return from_chunks(o).astype(dtype)


def dsa_sparse_attention(q, k, v, q_idx, k_idx, w_idx):
    b_, s_, h, dh = q.shape
    topk = min(TOPK_MAX, s_ // 4)
    nblk = s_ // Q_BLOCK
    key_pos = jnp.arange(s_)
    k_idx32 = k_idx.astype(F32)

    def blockify(t):
        return jnp.swapaxes(t.reshape((b_, nblk, Q_BLOCK) + t.shape[2:]), 0, 1)

    def one_block(args):
        q_b, qi_b, wi_b, start = args
        q_pos = start + jnp.arange(Q_BLOCK)
        dots = jnp.einsum('bqhd,bsd->bqhs', qi_b.astype(F32), k_idx32)
        score = jnp.einsum('bqh,bqhs->bqs', wi_b.astype(F32), jax.nn.relu(dots))
        causal = key_pos[None, :] <= q_pos[:, None]
        score = jnp.where(causal[None], score, -jnp.inf)
        _, sel = lax.top_k(score, topk)
        k_sel = jax.vmap(lambda kk, ii: kk[ii])(k, sel)
        v_sel = jax.vmap(lambda vv, ii: vv[ii])(v, sel)
        logits = jnp.einsum('bqhd,bqkhd->bqhk', q_b, k_sel).astype(F32) * dh ** -0.5
        valid = sel <= q_pos[None, :, None]
        logits = jnp.where(valid[:, :, None, :], logits, -jnp.inf)
        p = jax.nn.softmax(logits, axis=-1).astype(v.dtype)
        return jnp.einsum('bqhk,bqkhd->bqhd', p, v_sel)

    starts = jnp.arange(nblk) * Q_BLOCK
    out = lax.map(one_block, (blockify(q), blockify(q_idx), blockify(w_idx), starts))
    return jnp.swapaxes(out, 0, 1).reshape(b_, s_, h, dh)


def even_mixer(h, cos_a, sin_a, cos_i, sin_i, w_in, conv_w, a_log, dt_bias, out_norm, w_out):
    b_, s_, _ = h.shape
    heads = lambda t, n: t.reshape(b_, s_, n, -1)
    proj = h @ w_in
    qa, ka, va, za, ba, aa, qb, kb, vb, qi, ki, wi = split_cols(proj, EVEN_SPLITS)
    qkv = jax.nn.silu(causal_depthwise_conv(jnp.concatenate([qa, ka, va], axis=-1), conv_w))
    qa, ka, va = split_cols(qkv, [GDN_QK, GDN_QK, GDN_V])
    qa = l2_normalize(heads(qa, GDN_HEADS))
    ka = l2_normalize(heads(ka, GDN_HEADS))
    va = heads(va, GDN_HEADS)
    beta = jax.nn.sigmoid(ba.astype(F32))
    g = -jnp.exp(a_log.astype(F32)) * jax.nn.softplus(aa.astype(F32) + dt_bias.astype(F32))
    oa = gated_delta_rule(qa, ka, va, g, beta)
    oa = rms_norm(oa, out_norm) * jax.nn.silu(heads(za, GDN_HEADS))
    qb = apply_rope(heads(qb, ATT_HEADS), cos_a, sin_a)
    kb = apply_rope(heads(kb, ATT_HEADS), cos_a, sin_a)
    vb = heads(vb, ATT_HEADS)
    qi = apply_rope(heads(qi, IDX_HEADS), cos_i, sin_i)
    ki = apply_rope(ki[:, :, None, :], cos_i, sin_i)[:, :, 0, :]
    wi = wi * (IDX_HEADS * IDX_DH) ** -0.5
    ob = dsa_sparse_attention(qb, kb, vb, qi, ki, wi)
    o = jnp.concatenate([oa.reshape(b_, s_, -1), ob.reshape(b_, s_, -1)], axis=-1)
    return o @ w_out


def gla_mixer(h, w_in, w_gate2, b_gate2, out_norm, w_out):
    b_, s_, _ = h.shape
    heads = lambda t: t.reshape(b_, s_, GLA_HEADS, -1)
    proj = h @ w_in
    q, k, v, r, glr = split_cols(proj, ODD_SPLITS)
    log_a = jax.nn.log_sigmoid((glr @ w_gate2 + b_gate2).astype(F32)) / GLA_TAU
    o = gla_chunked(heads(q), heads(k), heads(v), heads(log_a))
    o = rms_norm(o, out_norm) * jax.nn.silu(heads(r))
    return o.reshape(b_, s_, -1) @ w_out


def hier_moe(h, wg, bg, we, be, w1, w3, w2):
    b_, s_, d = h.shape
    t = b_ * s_
    xt = h.reshape(t, d)
    tok_ids = jnp.arange(t)
    g_logits = (xt @ wg + bg).astype(F32)
    g_prob = jax.nn.softmax(g_logits, axis=-1)
    g_sel = jnp.argmax(g_logits, axis=-1)
    g_gate = g_prob[tok_ids, g_sel]
    e_logits = (jnp.einsum('td,gde->tge', xt, we) + be).astype(F32)
    e_logits = e_logits[tok_ids, g_sel]
    top_l, top_i = lax.top_k(e_logits, TOP_K_EXPERTS)
    gates = g_gate[:, None] * jax.nn.softmax(top_l, axis=-1)
    eid = (g_sel[:, None] * EXPERTS_PER_GROUP + top_i).reshape(-1).astype(jnp.int32)
    wts = gates.reshape(-1)
    tok = jnp.repeat(tok_ids.astype(jnp.int32), TOP_K_EXPERTS)
    m = t * TOP_K_EXPERTS
    counts = jnp.bincount(eid, length=N_EXPERTS)
    padded = ((counts + MOE_BLOCK - 1) // MOE_BLOCK) * MOE_BLOCK
    pad_end = jnp.cumsum(padded)
    pad_start = pad_end - padded
    raw_start = jnp.cumsum(counts) - counts
    order = jnp.argsort(eid)
    se = eid[order]
    dest = pad_start[se] + (jnp.arange(m) - raw_start[se])
    p_rows = (-(-m // MOE_BLOCK) + N_EXPERTS) * MOE_BLOCK
    buf_tok = jnp.zeros((p_rows,), jnp.int32).at[dest].set(tok[order])
    buf_w = jnp.zeros((p_rows,), F32).at[dest].set(wts[order])
    nblk = p_rows // MOE_BLOCK
    blk_exp = jnp.clip(jnp.searchsorted(pad_end, jnp.arange(nblk) * MOE_BLOCK, side='right'),
                       0, N_EXPERTS - 1)
    xb = xt[buf_tok].reshape(nblk, MOE_BLOCK, d)

    def expert_block(args):
        xi, e = args
        hid = jax.nn.silu(xi @ w1[e]) * (xi @ w3[e])
        return hid @ w2[e]

    yb = lax.map(expert_block, (xb, blk_exp)).reshape(p_rows, d)
    y = jnp.zeros((t, d), h.dtype).at[buf_tok].add(yb * buf_w[:, None].astype(yb.dtype))
    return y.reshape(b_, s_, d)


def setup_inputs(seed: int = 0) -> dict:
    key = jax.random.key(seed)
    d = D_MODEL
    n_even = (DEPTH + 1) // 2
    n_odd = DEPTH // 2

    def nrm(i, shape, scale):
        return jax.random.normal(jax.random.fold_in(key, i), shape, F32) * scale

    def gain(i, shape):
        return 1.0 + nrm(i, shape, 0.02)

    return {
        "x": nrm(0, (BATCH, SEQ, d), 1.0),
        "c": nrm(1, (BATCH, d), 1.0),
        "positions": jnp.broadcast_to(jnp.arange(SEQ, dtype=jnp.int32), (BATCH, SEQ)),
        "ada_w": nrm(2, (DEPTH, d, 6 * d), 0.5 * d ** -0.5),
        "ada_b": nrm(3, (DEPTH, 6 * d), 0.02),
        "norm_mix": gain(4, (DEPTH, d)),
        "norm_ffn": gain(5, (DEPTH, d)),
        "even_w_in": nrm(6, (n_even, d, EVEN_IN), d ** -0.5),
        "gdn_conv_w": nrm(7, (n_even, CONV_WIDTH, 2 * GDN_QK + GDN_V), CONV_WIDTH ** -0.5),
        "gdn_a_log": jnp.log(jax.random.uniform(jax.random.fold_in(key, 8), (n_even, GDN_HEADS), F32, 1.0, 16.0)),
        "gdn_dt_bias": nrm(9, (n_even, GDN_HEADS), 0.1),
        "gdn_out_norm": gain(10, (n_even, GDN_DV)),
        "even_w_out": nrm(11, (n_even, EVEN_MIX, d), EVEN_MIX ** -0.5),
        "gla_w_in": nrm(12, (n_odd, d, ODD_IN), d ** -0.5),
        "gla_w_gate2": nrm(13, (n_odd, GLA_RANK, GLA_HEADS * GLA_DK), GLA_RANK ** -0.5),
        "gla_b_gate2": nrm(14, (n_odd, GLA_HEADS * GLA_DK), 0.1),
        "gla_out_norm": gain(15, (n_odd, GLA_DV)),
        "gla_w_out": nrm(16, (n_odd, ODD_MIX, d), ODD_MIX ** -0.5),
        "router_group_w": nrm(17, (DEPTH, d, N_GROUPS), d ** -0.5),
        "router_group_b": nrm(18, (DEPTH, N_GROUPS), 0.01),
        "router_exp_w": nrm(19, (DEPTH, N_GROUPS, d, EXPERTS_PER_GROUP), d ** -0.5),
        "router_exp_b": nrm(20, (DEPTH, N_GROUPS, EXPERTS_PER_GROUP), 0.01),
        "exp_w1": nrm(21, (DEPTH, N_EXPERTS, d, D_EXPERT), d ** -0.5),
        "exp_w3": nrm(22, (DEPTH, N_EXPERTS, d, D_EXPERT), d ** -0.5),
        "exp_w2": nrm(23, (DEPTH, N_EXPERTS, D_EXPERT, d), D_EXPERT ** -0.5),
        "final_norm": gain(24, (d,)),
    }


def reference(x, c, positions, ada_w, ada_b, norm_mix, norm_ffn,
              even_w_in, gdn_conv_w, gdn_a_log, gdn_dt_bias, gdn_out_norm, even_w_out,
              gla_w_in, gla_w_gate2, gla_b_gate2, gla_out_norm, gla_w_out,
              router_group_w, router_group_b, router_exp_w, router_exp_b,
              exp_w1, exp_w3, exp_w2, final_norm):
    cos_a, sin_a = rope_tables(positions, ATT_DH)
    cos_i, sin_i = rope_tables(positions, IDX_DH)
    cond = jax.nn.silu(c)
    for layer in range(DEPTH):
        mod = cond @ ada_w[layer] + ada_b[layer]
        sh_m, sc_m, gt_m, sh_f, sc_f, gt_f = jnp.split(mod, 6, axis=-1)
        h = modulate(rms_norm(x, norm_mix[layer]), sh_m, sc_m)
        i = layer // 2
        if layer % 2 == 0:
            mix = even_mixer(h, cos_a, sin_a, cos_i, sin_i, even_w_in[i], gdn_conv_w[i],
                             gdn_a_log[i], gdn_dt_bias[i], gdn_out_norm[i], even_w_out[i])
        else:
            mix = gla_mixer(h, gla_w_in[i], gla_w_gate2[i], gla_b_gate2[i],
                            gla_out_norm[i], gla_w_out[i])
        x = x + gt_m[:, None, :] * mix
        h = modulate(rms_norm(x, norm_ffn[layer]), sh_f, sc_f)
        x = x + gt_f[:, None, :] * hier_moe(h, router_group_w[layer], router_group_b[layer],
                                           router_exp_w[layer], router_exp_b[layer],
                                           exp_w1[layer], exp_w3[layer], exp_w2[layer])
    return rms_norm(x, final_norm)
```

```python
import functools

import jax
import jax.numpy as jnp
import numpy as np
from jax import lax
from jax.experimental import pallas as pl
from jax.experimental.pallas import tpu as pltpu

F32 = jnp.float32
BF16 = jnp.bfloat16
I32 = jnp.int32
HI = lax.Precision.HIGHEST

LANES = 128
SUBLANES = 8
VMEM_LIMIT = 56 * 1024 * 1024

EPS = 1e-6
ROPE_THETA = 10000.0
GDN_HEADS, GDN_DK = 4, 128
CONV_WIDTH = 4
GDN_CHUNK = 64
ATT_HEADS, ATT_DH = 4, 128
IDX_HEADS, IDX_DH = 4, 64
TOPK_MAX = 256
Q_BLOCK = 128
GLA_HEADS, GLA_DK, GLA_DV = 4, 128, 256
GLA_RANK = 16
GLA_TAU = 16.0
GLA_CHUNK = 64
GLA_SUB = 16
N_GROUPS, EXPERTS_PER_GROUP = 4, 8
N_EXPERTS = N_GROUPS * EXPERTS_PER_GROUP
MOE_ROWS = 256

EV_QA, EV_KA, EV_VA, EV_ZA = 0, 512, 1024, 1536
EV_QB, EV_KB, EV_VB, EV_QI, EV_SM = 2048, 2560, 3072, 3584, 3840
EV_N = 3968
SM_KI, SM_BA, SM_AA, SM_WI = 0, 64, 68, 72
OD_Q, OD_K, OD_V, OD_R, OD_G = 0, 512, 1024, 2048, 3072
OD_N = 3200

NEG_INF = float("-inf")
INT_MIN = -(2 ** 31)


def _cparams(sem):
    return pltpu.CompilerParams(dimension_semantics=sem, vmem_limit_bytes=VMEM_LIMIT)


def _sigmoid(x):
    return 1.0 / (1.0 + jnp.exp(-x))


def _silu(x):
    return x * _sigmoid(x)


def _softplus(x):
    return jnp.maximum(x, 0.0) + jnp.log(1.0 + jnp.exp(-jnp.abs(x)))


def _dot(a, b, precision=None):
    return jnp.dot(a, b, precision=precision, preferred_element_type=F32)


def _dot_nt(a, b, precision=None):
    return lax.dot_general(a, b, (((1,), (1,)), ((), ())), precision=precision,
                           preferred_element_type=F32)


def _dot_tn(a, b, precision=None):
    return lax.dot_general(a, b, (((0,), (0,)), ((), ())), precision=precision,
                           preferred_element_type=F32)


def _norm_mod(x, g, shift, scale):
    ms = jnp.mean(x * x, axis=-1, keepdims=True)
    y = x * lax.rsqrt(ms + EPS) * g
    return y * (1.0 + scale) + shift


def _ada_kernel(c_ref, w_ref, b_ref, o_ref):
    o_ref[0] = _dot(_silu(c_ref[...]), w_ref[0], HI) + b_ref[0]


def ada_modulation(c, ada_w, ada_b):
    depth, d, n = ada_w.shape
    b = c.shape[0]
    cp = jnp.zeros((SUBLANES, d), F32).at[:b].set(c)
    tn = 1536
    out = pl.pallas_call(
        _ada_kernel,
        grid=(depth, n // tn),
        in_specs=[pl.BlockSpec((SUBLANES, d), lambda l, j: (0, 0)),
                  pl.BlockSpec((1, d, tn), lambda l, j: (l, 0, j)),
                  pl.BlockSpec((1, 1, tn), lambda l, j: (l, 0, j))],
        out_specs=pl.BlockSpec((1, SUBLANES, tn), lambda l, j: (l, 0, j)),
        out_shape=jax.ShapeDtypeStruct((depth, SUBLANES, n), F32),
        compiler_params=_cparams(("arbitrary", "arbitrary")),
        name="ada_modulation",
    )(cp, ada_w, ada_b.reshape(depth, 1, n))
    return out[:, :b]


def _inproj_kernel(x_ref, g_ref, sh_ref, sc_ref, w_ref, o_ref):
    h = _norm_mod(x_ref[...], g_ref[...], sh_ref[0], sc_ref[0])
    o_ref[...] = _dot(h.astype(BF16), w_ref[...])


def in_projection(x, g, shift, scale, w, seq):
    t, d = x.shape
    n = w.shape[1]
    tm = 512
    per_b = seq // tm
    return pl.pallas_call(
        _inproj_kernel,
        grid=(t // tm,),
        in_specs=[pl.BlockSpec((tm, d), lambda i: (i, 0)),
                  pl.BlockSpec((1, d), lambda i: (0, 0)),
                  pl.BlockSpec((1, 1, d), lambda i: (i // per_b, 0, 0)),
                  pl.BlockSpec((1, 1, d), lambda i: (i // per_b, 0, 0)),
                  pl.BlockSpec((d, n), lambda i: (0, 0))],
        out_specs=pl.BlockSpec((tm, n), lambda i: (i, 0)),
        out_shape=jax.ShapeDtypeStruct((t, n), F32),
        compiler_params=_cparams(("arbitrary",)),
        name="in_projection",
    )(x, g, shift, scale, w)


def _outproj_kernel(*refs, n_in):
    o_refs = refs[:n_in]
    w_refs = refs[n_in:2 * n_in]
    x_ref, gt_ref, xo_ref = refs[2 * n_in:]
    acc = _dot(o_refs[0][...].astype(BF16), w_refs[0][...])
    for o_ref, w_ref in zip(o_refs[1:], w_refs[1:]):
        acc = acc + _dot(o_ref[...].astype(BF16), w_ref[...])
    xo_ref[...] = x_ref[...] + gt_ref[0] * acc


def out_projection(parts, weights, x, gate, seq):
    t, d = x.shape
    tm = 512
    per_b = seq // tm
    n_in = len(parts)
    in_specs = [pl.BlockSpec((tm, p.shape[1]), lambda i: (i, 0)) for p in parts]
    in_specs += [pl.BlockSpec(w.shape, lambda i: (0, 0)) for w in weights]
    in_specs += [pl.BlockSpec((tm, d), lambda i: (i, 0)),
                 pl.BlockSpec((1, 1, d), lambda i: (i // per_b, 0, 0))]
    return pl.pallas_call(
        functools.partial(_outproj_kernel, n_in=n_in),
        grid=(t // tm,),
        in_specs=in_specs,
        out_specs=pl.BlockSpec((tm, d), lambda i: (i, 0)),
        out_shape=jax.ShapeDtypeStruct((t, d), F32),
        compiler_params=_cparams(("arbitrary",)),
        name="out_projection",
    )(*parts, *weights, x, gate)


def _final_norm_kernel(x_ref, g_ref, o_ref):
    x = x_ref[...]
    ms = jnp.mean(x * x, axis=-1, keepdims=True)
    o_ref[...] = x * lax.rsqrt(ms + EPS) * g_ref[...]


def final_rms_norm(x, g):
    t, d = x.shape
    tm = 512
    return pl.pallas_call(
        _final_norm_kernel,
        grid=(t // tm,),
        in_specs=[pl.BlockSpec((tm, d), lambda i: (i, 0)),
                  pl.BlockSpec((1, d), lambda i: (0, 0))],
        out_specs=pl.BlockSpec((tm, d), lambda i: (i, 0)),
        out_shape=jax.ShapeDtypeStruct((t, d), F32),
        compiler_params=_cparams(("arbitrary",)),
        name="final_norm",
    )(x, g)


def _tri_masks(c):
    row = lax.broadcasted_iota(I32, (c, c), 0)
    col = lax.broadcasted_iota(I32, (c, c), 1)
    return row >= col, row > col, row == col


def _gdn_kernel(qkv_ref, za_ref, sm_ref, cw_ref, nega_ref, dtb_ref, on_ref, o_ref,
                tail_ref, state_ref, *, rb):
    c = GDN_CHUNK
    dk = GDN_DK

    @pl.when(pl.program_id(1) == 0)
    def _():
        tail_ref[...] = jnp.zeros_like(tail_ref)
        state_ref[...] = jnp.zeros_like(state_ref)

    x = qkv_ref[...]
    xx = jnp.concatenate([tail_ref[...], x], axis=0)
    tail_ref[...] = x[rb - SUBLANES:, :]
    cw = cw_ref[...]
    y = x * cw[CONV_WIDTH - 1:CONV_WIDTH, :]
    for i in range(1, CONV_WIDTH):
        y = y + pltpu.roll(xx, i, axis=0)[SUBLANES:, :] * cw[CONV_WIDTH - 1 - i:CONV_WIDTH - i, :]
    y = _silu(y)

    sm = sm_ref[...]
    beta_t = _sigmoid(sm)
    g_t = nega_ref[...] * _softplus(sm + dtb_ref[...])
    za = za_ref[...]
    incl, strict, diag = _tri_masks(c)
    incl_f = incl.astype(F32)
    strict_f = strict.astype(F32)
    eye = diag.astype(F32)
    hw = GDN_HEADS * dk

    for ci in range(rb // c):
        r0 = ci * c
        for h in range(GDN_HEADS):
            lo = h * dk
            q = y[r0:r0 + c, lo:lo + dk]
            k = y[r0:r0 + c, hw + lo:hw + lo + dk]
            v = y[r0:r0 + c, 2 * hw + lo:2 * hw + lo + dk]
            q = q * lax.rsqrt(jnp.sum(q * q, axis=-1, keepdims=True) + EPS) * dk ** -0.5
            k = k * lax.rsqrt(jnp.sum(k * k, axis=-1, keepdims=True) + EPS)
            gb = jnp.broadcast_to(g_t[r0:r0 + c, SM_AA + h:SM_AA + h + 1], (c, dk))
            bb = jnp.broadcast_to(beta_t[r0:r0 + c, SM_BA + h:SM_BA + h + 1], (c, dk))
            gc = _dot(incl_f, gb, HI)
            dmat = _dot(incl_f, gb[:, :c] * strict_f, HI)
            decay = jnp.where(incl, jnp.exp(jnp.where(incl, dmat, 0.0)), 0.0)
            kbeta = k * bb
            lmat = jnp.where(strict, _dot_nt(kbeta, k, HI) * decay, 0.0)
            ypow = -lmat
            tmat = eye + ypow
            for _ in range(int(np.log2(c)) - 1):
                ypow = _dot(ypow, ypow, HI)
                tmat = tmat + _dot(tmat, ypow, HI)
            egc = jnp.exp(gc)
            w = _dot(tmat, kbeta * egc, HI)
            u = _dot(tmat, v * bb, HI)
            aqk = _dot_nt(q, k, HI) * decay
            gl = gc[c - 1:c, :]
            kd = k * jnp.exp(gl - gc)
            st = state_ref[h]
            v_new = u - _dot(w, st, HI)
            o = _dot(q * egc, st, HI) + _dot(aqk, v_new, HI)
            state_ref[h] = st * jnp.exp(gl) + _dot_tn(kd, v_new, HI)
            o = o * lax.rsqrt(jnp.mean(o * o, axis=-1, keepdims=True) + EPS) * on_ref[...]
            o_ref[r0:r0 + c, lo:lo + dk] = o * _silu(za[r0:r0 + c, lo:lo + dk])


def gdn_mixer(proj, conv_w, a_log, dt_bias, out_norm, batch, seq):
    t = proj.shape[0]
    rb = 128
    nsb = seq // rb
    hw = GDN_HEADS * GDN_DK
    nega = jnp.zeros((1, LANES), F32).at[0, SM_AA:SM_AA + GDN_HEADS].set(-jnp.exp(a_log))
    dtb = jnp.zeros((1, LANES), F32).at[0, SM_AA:SM_AA + GDN_HEADS].set(dt_bias)
    row = lambda b, s: b * nsb + s
    return pl.pallas_call(
        functools.partial(_gdn_kernel, rb=rb),
        grid=(batch, nsb),
        in_specs=[pl.BlockSpec((rb, 3 * hw), lambda b, s: (row(b, s), 0)),
                  pl.BlockSpec((rb, hw), lambda b, s: (row(b, s), EV_ZA // hw)),
                  pl.BlockSpec((rb, LANES), lambda b, s: (row(b, s), EV_SM // LANES)),
                  pl.BlockSpec((CONV_WIDTH, 3 * hw), lambda b, s: (0, 0)),
                  pl.BlockSpec((1, LANES), lambda b, s: (0, 0)),
                  pl.BlockSpec((1, LANES), lambda b, s: (0, 0)),
                  pl.BlockSpec((1, GDN_DK), lambda b, s: (0, 0))],
        out_specs=pl.BlockSpec((rb, hw), lambda b, s: (row(b, s), 0)),
        out_shape=jax.ShapeDtypeStruct((t, hw), F32),
        scratch_shapes=[pltpu.VMEM((SUBLANES, 3 * hw), F32),
                        pltpu.VMEM((GDN_HEADS, GDN_DK, GDN_DK), F32)],
        compiler_params=_cparams(("arbitrary", "arbitrary")),
        name="gdn_mixer",
    )(proj, proj, proj, conv_w, nega, dtb, out_norm.reshape(1, GDN_DK))


def _gla_kernel(qk_ref, v_ref, r_ref, gl_ref, wg_ref, bg_ref, on_ref, o_ref, state_ref, *, rb):
    c = GLA_CHUNK
    dk, dv = GLA_DK, GLA_DV
    sb = GLA_SUB

    @pl.when(pl.program_id(1) == 0)
    def _():
        state_ref[...] = jnp.zeros_like(state_ref)

    z = _dot(gl_ref[...], wg_ref[...], HI) + bg_ref[...]
    log_a = -_softplus(-z) * (1.0 / GLA_TAU)
    qk = qk_ref[...]
    vv = v_ref[...]
    rr = r_ref[...]
    incl, _, _ = _tri_masks(c)
    incl_f = incl.astype(F32)
    hw = GLA_HEADS * dk
    sub_row = lax.broadcasted_iota(I32, (sb, dk), 0)
    sub_lane = lax.broadcasted_iota(I32, (sb, sb), 1)

    for ci in range(rb // c):
        r0 = ci * c
        for h in range(GLA_HEADS):
            q = qk[r0:r0 + c, h * dk:(h + 1) * dk] * dk ** -0.5
            k = qk[r0:r0 + c, hw + h * dk:hw + (h + 1) * dk]
            v = vv[r0:r0 + c, h * dv:(h + 1) * dv]
            b = _dot(incl_f, log_a[r0:r0 + c, h * dk:(h + 1) * dk], HI)
            st = state_ref[h]
            o_rows = []
            for bi in range(c // sb):
                s0 = bi * sb
                qi = q[s0:s0 + sb]
                bi_ = b[s0:s0 + sb]
                ref = b[s0:s0 + 1]
                o_blk = _dot_nt(qi * jnp.exp(bi_), st, HI)
                if bi > 0:
                    qt = qi * jnp.exp(bi_ - ref)
                    kt = k[:s0] * jnp.exp(ref - b[:s0])
                    o_blk = o_blk + _dot(_dot_nt(qt, kt, HI), v[:s0], HI)
                att = jnp.zeros((sb, sb), F32)
                for j in range(sb):
                    kj = k[s0 + j:s0 + j + 1]
                    bj = b[s0 + j:s0 + j + 1]
                    e = jnp.exp(jnp.where(sub_row >= j, bi_ - bj, NEG_INF))
                    col = jnp.sum(qi * kj * e, axis=-1, keepdims=True)
                    att = jnp.where(sub_lane == j, col, att)
                o_blk = o_blk + _dot(att, v[s0:s0 + sb], HI)
                o_rows.append(o_blk)
            o = jnp.concatenate(o_rows, axis=0)
            bl = b[c - 1:c]
            state_ref[h] = st * jnp.exp(bl) + _dot_tn(v, k * jnp.exp(bl - b), HI)
            o = o * lax.rsqrt(jnp.mean(o * o, axis=-1, keepdims=True) + EPS) * on_ref[...]
            o_ref[r0:r0 + c, h * dv:(h + 1) * dv] = o * _silu(rr[r0:r0 + c, h * dv:(h + 1) * dv])


def gla_mixer(proj, w_gate2, b_gate2, out_norm, batch, seq):
    t = proj.shape[0]
    rb = GLA_CHUNK
    nsb = seq // rb
    hw = GLA_HEADS * GLA_DK
    hv = GLA_HEADS * GLA_DV
    wg = jnp.zeros((LANES, hw), F32).at[:GLA_RANK].set(w_gate2)
    row = lambda b, s: b * nsb + s
    return pl.pallas_call(
        functools.partial(_gla_kernel, rb=rb),
        grid=(batch, nsb),
        in_specs=[pl.BlockSpec((rb, 2 * hw), lambda b, s: (row(b, s), 0)),
                  pl.BlockSpec((rb, hv), lambda b, s: (row(b, s), OD_V // hv)),
                  pl.BlockSpec((rb, hv), lambda b, s: (row(b, s), OD_R // hv)),
                  pl.BlockSpec((rb, LANES), lambda b, s: (row(b, s), OD_G // LANES)),
                  pl.BlockSpec((LANES, hw), lambda b, s: (0, 0)),
                  pl.BlockSpec((1, hw), lambda b, s: (0, 0)),
                  pl.BlockSpec((1, GLA_DV), lambda b, s: (0, 0))],
        out_specs=pl.BlockSpec((rb, hv), lambda b, s: (row(b, s), 0)),
        out_shape=jax.ShapeDtypeStruct((t, hv), F32),
        scratch_shapes=[pltpu.VMEM((GLA_HEADS, GLA_DV, GLA_DK), F32)],
        compiler_params=_cparams(("arbitrary", "arbitrary")),
        name="gla_mixer",
    )(proj, proj, proj, proj, wg, b_gate2.reshape(1, hw), out_norm.reshape(1, GLA_DV))


def _rope_table_kernel(pos_ref, inva_ref, invi_ref, ca_ref, sa_ref, ci_ref, si_ref):
    pos = pos_ref[...].astype(F32)
    lane = lax.broadcasted_iota(I32, (pos.shape[0], LANES), 1)
    ang_a = pos * inva_ref[...]
    ca_ref[...] = jnp.cos(ang_a)
    sa_ref[...] = jnp.where(lane < ATT_DH // 2, -1.0, 1.0) * jnp.sin(ang_a)
    ang_i = pos * invi_ref[...]
    ci_ref[...] = jnp.cos(ang_i)
    si_ref[...] = jnp.where(lane % IDX_DH < IDX_DH // 2, -1.0, 1.0) * jnp.sin(ang_i)


def rope_tables(positions):
    t = positions.size
    inv_a = 1.0 / (ROPE_THETA ** (jnp.arange(0, ATT_DH, 2, dtype=F32) / ATT_DH))
    inv_i = 1.0 / (ROPE_THETA ** (jnp.arange(0, IDX_DH, 2, dtype=F32) / IDX_DH))
    inv_a = jnp.tile(inv_a, 2).reshape(1, LANES)
    inv_i = jnp.tile(inv_i, 4).reshape(1, LANES)
    tm = 512
    spec = pl.BlockSpec((tm, LANES), lambda i: (i, 0))
    vec = pl.BlockSpec((1, LANES), lambda i: (0, 0))
    return pl.pallas_call(
        _rope_table_kernel,
        grid=(t // tm,),
        in_specs=[pl.BlockSpec((tm, 1), lambda i: (i, 0)), vec, vec],
        out_specs=[spec] * 4,
        out_shape=[jax.ShapeDtypeStruct((t, LANES), F32)] * 4,
        compiler_params=_cparams(("arbitrary",)),
        name="rope_tables",
    )(positions.reshape(t, 1), inv_a, inv_i)


def _dsa_prep_kernel(qb_ref, kb_ref, vb_ref, qi_ref, sm_ref, ca_ref, sa_ref, ci_ref, si_ref,
                     qo_ref, ko_ref, vo_ref, qio_ref, kio_ref):
    ca, sa, ci, si = ca_ref[...], sa_ref[...], ci_ref[...], si_ref[...]
    lane = lax.broadcasted_iota(I32, ca.shape, 1)
    first_half = lane % IDX_DH < IDX_DH // 2

    def rope_att(tile):
        return tile * ca + pltpu.roll(tile, ATT_DH // 2, axis=1) * sa

    def rope_idx(tile):
        partner = jnp.where(first_half, pltpu.roll(tile, LANES - IDX_DH // 2, axis=1),
                            pltpu.roll(tile, IDX_DH // 2, axis=1))
        return tile * ci + partner * si

    qb, kb = qb_ref[...], kb_ref[...]
    for h in range(ATT_HEADS):
        sl = slice(h * ATT_DH, (h + 1) * ATT_DH)
        qo_ref[:, sl] = (rope_att(qb[:, sl]) * ATT_DH ** -0.5).astype(BF16)
        ko_ref[:, sl] = rope_att(kb[:, sl]).astype(BF16)
    vo_ref[...] = vb_ref[...].astype(BF16)
    qi = qi_ref[...]
    for p in range(IDX_HEADS * IDX_DH // LANES):
        sl = slice(p * LANES, (p + 1) * LANES)
        qio_ref[:, sl] = rope_idx(qi[:, sl]).astype(BF16)
    ki = rope_idx(sm_ref[...])
    kio_ref[...] = jnp.where(lane < IDX_DH, ki, pltpu.roll(ki, IDX_DH, axis=1)).astype(BF16)


def dsa_prep(proj, tables):
    t = proj.shape[0]
    tm = 512
    aw = ATT_HEADS * ATT_DH
    iw = IDX_HEADS * IDX_DH
    tab = pl.BlockSpec((tm, LANES), lambda i: (i, 0))
    return pl.pallas_call(
        _dsa_prep_kernel,
        grid=(t // tm,),
        in_specs=[pl.BlockSpec((tm, aw), lambda i: (i, EV_QB // aw)),
                  pl.BlockSpec((tm, aw), lambda i: (i, EV_KB // aw)),
                  pl.BlockSpec((tm, aw), lambda i: (i, EV_VB // aw)),
                  pl.BlockSpec((tm, iw), lambda i: (i, EV_QI // iw)),
                  pl.BlockSpec((tm, LANES), lambda i: (i, EV_SM // LANES)),
                  tab, tab, tab, tab],
        out_specs=[pl.BlockSpec((tm, aw), lambda i: (i, 0))] * 3
        + [pl.BlockSpec((tm, iw), lambda i: (i, 0)), pl.BlockSpec((tm, LANES), lambda i: (i, 0))],
        out_shape=[jax.ShapeDtypeStruct((t, aw), BF16)] * 3
        + [jax.ShapeDtypeStruct((t, iw), BF16), jax.ShapeDtypeStruct((t, LANES), BF16)],
        compiler_params=_cparams(("arbitrary",)),
        name="dsa_prep",
    )(proj, proj, proj, proj, proj, *tables)


def _dsa_kernel(qb_ref, qi_ref, sm_ref, k_ref, v_ref, ki_ref, o_ref,
                key_ref, m_ref, l_ref, acc_ref, *, kc, topk):
    qb_rows = Q_BLOCK
    i = pl.program_id(1)
    nkc = (i * qb_rows + qb_rows + kc - 1) // kc
    q_pos = i * qb_rows + lax.broadcasted_iota(I32, (qb_rows, 1), 0)
    lane = lax.broadcasted_iota(I32, (qb_rows, LANES), 1)
    col0 = lax.broadcasted_iota(I32, (qb_rows, kc), 1)
    wi = sm_ref[...] * (IDX_HEADS * IDX_DH) ** -0.5
    qi = qi_ref[...]
    zero_b = jnp.zeros((qb_rows, LANES), BF16)
    qi_heads = []
    for h in range(IDX_HEADS):
        pair = qi[:, (h // 2) * LANES:(h // 2 + 1) * LANES]
        own = (lane < IDX_DH) if h % 2 == 0 else (lane >= IDX_DH)
        qi_heads.append(jnp.where(own, pair, zero_b))

    def score_body(c, carry):
        start = pl.multiple_of(c * kc, kc)
        kic = ki_ref[0, pl.ds(start, kc), :]
        s = jnp.zeros((qb_rows, kc), F32)
        for h in range(IDX_HEADS):
            d = _dot_nt(qi_heads[h], kic)
            s = s + wi[:, SM_WI + h:SM_WI + h + 1] * jnp.maximum(d, 0.0)
        s = s + 0.0
        s = jnp.where(c * kc + col0 <= q_pos, s, NEG_INF)
        bits = pltpu.bitcast(s, I32)
        key_ref[c] = bits ^ ((bits >> 31) & 0x7FFFFFFF)
        return carry

    lax.fori_loop(0, nkc, score_body, 0)

    def lane_fold(x):
        acc = x[:, :LANES]
        for j in range(1, kc // LANES):
            acc = acc + x[:, j * LANES:(j + 1) * LANES]
        return acc

    def count(pred):
        def body(c, acc):
            return acc + lane_fold(jnp.where(pred(key_ref[c], c), 1, 0).astype(I32))
        acc = lax.fori_loop(0, nkc, body, jnp.zeros((qb_rows, LANES), I32))
        return jnp.sum(acc, axis=1, keepdims=True)

    thr = jnp.where(count(lambda key, c: key >= 0) >= topk, 0, INT_MIN).astype(I32)

    def bit_body(it, thr):
        cand = thr + (jnp.int32(1) << (30 - it))
        return jnp.where(count(lambda key, c: key >= cand) >= topk, cand, thr)

    thr = lax.fori_loop(0, 31, bit_body, thr)

    n_gt = count(lambda key, c: key > thr)
    n_eq = count(lambda key, c: key == thr)
    need = topk - n_gt
    big = jnp.int32(2 ** 30)

    def tie_search():
        def idx_body(it, x):
            cand = x + (jnp.int32(1) << (14 - it))
            below = count(lambda key, c: (key == thr) & (c * kc + col0 < cand))
            return jnp.where(below < need, cand, x)
        return lax.fori_loop(0, 15, idx_body, jnp.zeros((qb_rows, 1), I32))

    any_tie = jnp.max(jnp.where(n_eq > need, 1, 0)) > 0
    last = lax.cond(any_tie, tie_search, lambda: jnp.full((qb_rows, 1), big, I32))

    m_ref[...] = jnp.full(m_ref.shape, NEG_INF, F32)
    l_ref[...] = jnp.zeros(l_ref.shape, F32)
    acc_ref[...] = jnp.zeros(acc_ref.shape, F32)
    qb = qb_ref[...]

    def att_body(c, carry):
        start = pl.multiple_of(c * kc, kc)
        key = key_ref[c]
        col = c * kc + col0
        sel = ((key > thr) | ((key == thr) & (col <= last))) & (col <= q_pos)
        kck = k_ref[0, pl.ds(start, kc), :]
        vck = v_ref[0, pl.ds(start, kc), :]
        for h in range(ATT_HEADS):
            sl = slice(h * ATT_DH, (h + 1) * ATT_DH)
            logits = jnp.where(sel, _dot_nt(qb[:, sl], kck[:, sl]), NEG_INF)
            m_old = m_ref[h]
            m_new = jnp.maximum(m_old, jnp.max(logits, axis=1, keepdims=True))
            m_safe = jnp.where(m_new == NEG_INF, 0.0, m_new)
            alpha = jnp.exp(m_old - m_safe)
            p = jnp.exp(logits - m_safe[:, :1])
            l_ref[h] = alpha * l_ref[h] + jnp.sum(p, axis=1, keepdims=True)
            acc_ref[h] = alpha * acc_ref[h] + _dot(p.astype(BF16), vck[:, sl])
            m_ref[h] = m_new
        return carry

    lax.fori_loop(0, nkc, att_body, 0)
    for h in range(ATT_HEADS):
        o_ref[:, h * ATT_DH:(h + 1) * ATT_DH] = acc_ref[h] / l_ref[h]


def dsa_attention(proj, qb, kb, vb, qi, ki, batch, seq):
    t = proj.shape[0]
    aw = ATT_HEADS * ATT_DH
    iw = IDX_HEADS * IDX_DH
    kc = min(512, seq)
    topk = min(TOPK_MAX, seq // 4)
    nq = seq // Q_BLOCK
    row = lambda b, i: b * nq + i
    kb3, vb3, ki3 = (a.reshape(batch, seq, a.shape[1]) for a in (kb, vb, ki))
    resident = lambda w: pl.BlockSpec((1, seq, w), lambda b, i: (b, 0, 0),
                                      pipeline_mode=pl.Buffered(1))
    return pl.pallas_call(
        functools.partial(_dsa_kernel, kc=kc, topk=topk),
        grid=(batch, nq),
        in_specs=[pl.BlockSpec((Q_BLOCK, aw), lambda b, i: (row(b, i), 0)),
                  pl.BlockSpec((Q_BLOCK, iw), lambda b, i: (row(b, i), 0)),
                  pl.BlockSpec((Q_BLOCK, LANES), lambda b, i: (row(b, i), EV_SM // LANES)),
                  resident(aw), resident(aw), resident(LANES)],
        out_specs=pl.BlockSpec((Q_BLOCK, aw), lambda b, i: (row(b, i), 0)),
        out_shape=jax.ShapeDtypeStruct((t, aw), F32),
        scratch_shapes=[pltpu.VMEM((seq // kc, Q_BLOCK, kc), I32),
                        pltpu.VMEM((ATT_HEADS, Q_BLOCK, LANES), F32),
                        pltpu.VMEM((ATT_HEADS, Q_BLOCK, LANES), F32),
                        pltpu.VMEM((ATT_HEADS, Q_BLOCK, ATT_DH), F32)],
        compiler_params=_cparams(("arbitrary", "arbitrary")),
        name="dsa_attention",
    )(qb, qi, proj, kb3, vb3, ki3)


def _router_kernel(x_ref, g_ref, sh_ref, sc_ref, wr_ref, br_ref, h_ref, route_ref, cnt_ref,
                   carry_ref):
    @pl.when(pl.program_id(0) == 0)
    def _():
        carry_ref[...] = jnp.zeros_like(carry_ref)

    h = _norm_mod(x_ref[...], g_ref[...], sh_ref[0], sc_ref[0])
    h_ref[...] = h
    tm = h.shape[0]
    logits = _dot(h, wr_ref[...], HI) + br_ref[...]
    lane = lax.broadcasted_iota(I32, (tm, LANES), 1)
    far = jnp.int32(LANES)

    def first_max(vals):
        m = jnp.max(vals, axis=1, keepdims=True)
        return m, jnp.min(jnp.where(vals == m, lane, far), axis=1, keepdims=True)

    is_group = lane < N_GROUPS
    gmax, gsel = first_max(jnp.where(is_group, logits, NEG_INF))
    gsum = jnp.sum(jnp.where(is_group, jnp.exp(logits - gmax), 0.0), axis=1, keepdims=True)
    lo = N_GROUPS + EXPERTS_PER_GROUP * gsel
    el = jnp.where((lane >= lo) & (lane < lo + EXPERTS_PER_GROUP), logits, NEG_INF)
    m1, i1 = first_max(el)
    m2, i2 = first_max(jnp.where(lane == i1, NEG_INF, el))
    e2 = jnp.exp(m2 - m1)
    gate1 = 1.0 / (gsum * (1.0 + e2))
    gate2 = e2 * gate1
    eid1 = i1 - N_GROUPS
    eid2 = i2 - N_GROUPS
    oh1 = lane == eid1
    oh2 = lane == eid2
    onehot = jnp.where(oh1 | oh2, 1.0, 0.0)
    r = lax.broadcasted_iota(I32, (tm, tm), 0)
    cidx = lax.broadcasted_iota(I32, (tm, tm), 1)
    below = jnp.where(r > cidx, 1.0, 0.0).astype(BF16)
    cum = _dot(below, onehot.astype(BF16)) + carry_ref[0:1, :]
    rank1 = jnp.sum(jnp.where(oh1, cum, 0.0), axis=1, keepdims=True)
    rank2 = jnp.sum(jnp.where(oh2, cum, 0.0), axis=1, keepdims=True)
    carry_ref[...] = carry_ref[...] + jnp.sum(onehot, axis=0, keepdims=True)
    cnt_ref[...] = carry_ref[...]
    out = jnp.zeros((tm, LANES), F32)
    for j, val in enumerate((eid1.astype(F32), eid2.astype(F32), gate1, gate2, rank1, rank2)):
        out = jnp.where(lane == j, val, out)
    route_ref[...] = out


def moe_router(x, g, shift, scale, wr, br, seq):
    t, d = x.shape
    tm = 512
    per_b = seq // tm
    return pl.pallas_call(
        _router_kernel,
        grid=(t // tm,),
        in_specs=[pl.BlockSpec((tm, d), lambda i: (i, 0)),
                  pl.BlockSpec((1, d), lambda i: (0, 0)),
                  pl.BlockSpec((1, 1, d), lambda i: (i // per_b, 0, 0)),
                  pl.BlockSpec((1, 1, d), lambda i: (i // per_b, 0, 0)),
                  pl.BlockSpec((d, LANES), lambda i: (0, 0)),
                  pl.BlockSpec((1, LANES), lambda i: (0, 0))],
        out_specs=[pl.BlockSpec((tm, d), lambda i: (i, 0)),
                   pl.BlockSpec((tm, LANES), lambda i: (i, 0)),
                   pl.BlockSpec((SUBLANES, LANES), lambda i: (0, 0))],
        out_shape=[jax.ShapeDtypeStruct((t, d), F32),
                   jax.ShapeDtypeStruct((t, LANES), F32),
                   jax.ShapeDtypeStruct((SUBLANES, LANES), F32)],
        scratch_shapes=[pltpu.VMEM((SUBLANES, LANES), F32)],
        compiler_params=_cparams(("arbitrary",)),
        name="moe_router",
    )(x, g, shift, scale, wr, br)


def _zero_kernel(o_ref):
    o_ref[...] = jnp.zeros_like(o_ref)


def zero_rows(rows, d):
    tm = 1024
    return pl.pallas_call(
        _zero_kernel,
        grid=(rows // tm,),
        out_specs=pl.BlockSpec((tm, d), lambda i: (i, 0)),
        out_shape=jax.ShapeDtypeStruct((rows, d), F32),
        compiler_params=_cparams(("arbitrary",)),
        name="zero_rows",
    )()


def _dispatch_kernel(dest_ref, h_ref, xs_in_ref, xs_ref, sem, *, tm):
    del xs_in_ref
    base = pl.program_id(0) * tm * 2

    def row_copy(r, j):
        d = dest_ref[base + 2 * r + j]
        return pltpu.make_async_copy(h_ref.at[pl.ds(r, 1)], xs_ref.at[pl.ds(d, 1)], sem)

    def start(r, carry):
        row_copy(r, 0).start()
        row_copy(r, 1).start()
        return carry

    def wait(r, carry):
        row_copy(r, 0).wait()
        row_copy(r, 1).wait()
        return carry

    lax.fori_loop(0, tm, start, 0)
    lax.fori_loop(0, tm, wait, 0)


def moe_dispatch(dest, h, xs_zero):
    t, d = h.shape
    tm = 256
    return pl.pallas_call(
        functools.partial(_dispatch_kernel, tm=tm),
        grid_spec=pltpu.PrefetchScalarGridSpec(
            num_scalar_prefetch=1,
            grid=(t // tm,),
            in_specs=[pl.BlockSpec((tm, d), lambda i, dest: (i, 0)),
                      pl.BlockSpec(memory_space=pl.ANY)],
            out_specs=pl.BlockSpec(memory_space=pl.ANY),
            scratch_shapes=[pltpu.SemaphoreType.DMA(())]),
        out_shape=jax.ShapeDtypeStruct(xs_zero.shape, F32),
        input_output_aliases={2: 0},
        compiler_params=_cparams(("arbitrary",)),
        name="moe_dispatch",
    )(dest, h, xs_zero)


def _expert_kernel(be_ref, nb_ref, xs_ref, w1_ref, w3_ref, w2_ref, ys_ref, w1b, w3b, w2b):
    i = pl.program_id(0)
    prev = be_ref[jnp.maximum(i - 1, 0)]

    @pl.when((i == 0) | (be_ref[i] != prev))
    def _():
        w1b[...] = w1_ref[0].astype(BF16)
        w3b[...] = w3_ref[0].astype(BF16)
        w2b[...] = w2_ref[0].astype(BF16)

    @pl.when(i < nb_ref[0])
    def _():
        x = xs_ref[...].astype(BF16)
        a = _dot(x, w1b[...])
        hid = _silu(a) * _dot(x, w3b[...])
        ys_ref[...] = _dot(hid.astype(BF16), w2b[...])

    @pl.when(i >= nb_ref[0])
    def _():
        ys_ref[...] = jnp.zeros_like(ys_ref)


def moe_experts(blk_exp, n_used, xs, w1, w3, w2):
    p, d = xs.shape
    de = w1.shape[2]
    rows = MOE_ROWS
    return pl.pallas_call(
        _expert_kernel,
        grid_spec=pltpu.PrefetchScalarGridSpec(
            num_scalar_prefetch=2,
            grid=(p // rows,),
            in_specs=[pl.BlockSpec((rows, d), lambda i, be, nb: (i, 0)),
                      pl.BlockSpec((1, d, de), lambda i, be, nb: (be[i], 0, 0)),
                      pl.BlockSpec((1, d, de), lambda i, be, nb: (be[i], 0, 0)),
                      pl.BlockSpec((1, de, d), lambda i, be, nb: (be[i], 0, 0))],
            out_specs=pl.BlockSpec((rows, d), lambda i, be, nb: (i, 0)),
            scratch_shapes=[pltpu.VMEM((d, de), BF16), pltpu.VMEM((d, de), BF16),
                            pltpu.VMEM((de, d), BF16)]),
        out_shape=jax.ShapeDtypeStruct((p, d), F32),
        compiler_params=_cparams(("arbitrary",)),
        name="moe_experts",
    )(blk_exp, n_used, xs, w1, w3, w2)


def _combine_kernel(dest_ref, ys_ref, route_ref, x_ref, gt_ref, o_ref, buf_ref, sem, *, tm):
    base = pl.program_id(0) * tm * 2

    def row_copy(r, j):
        d = dest_ref[base + 2 * r + j]
        return pltpu.make_async_copy(ys_ref.at[pl.ds(d, 1)], buf_ref.at[j, pl.ds(r, 1)], sem)

    def start(r, carry):
        row_copy(r, 0).start()
        row_copy(r, 1).start()
        return carry

    def wait(r, carry):
        row_copy(r, 0).wait()
        row_copy(r, 1).wait()
        return carry

    lax.fori_loop(0, tm, start, 0)
    lax.fori_loop(0, tm, wait, 0)
    route = route_ref[...]
    y = route[:, 2:3] * buf_ref[0] + route[:, 3:4] * buf_ref[1]
    o_ref[...] = x_ref[...] + gt_ref[0] * y


def moe_combine(dest, ys, route, x, gate, seq):
    t, d = x.shape
    tm = 256
    per_b = seq // tm
    return pl.pallas_call(
        functools.partial(_combine_kernel, tm=tm),
        grid_spec=pltpu.PrefetchScalarGridSpec(
            num_scalar_prefetch=1,
            grid=(t // tm,),
            in_specs=[pl.BlockSpec(memory_space=pl.ANY),
                      pl.BlockSpec((tm, LANES), lambda i, dest: (i, 0)),
                      pl.BlockSpec((tm, d), lambda i, dest: (i, 0)),
                      pl.BlockSpec((1, 1, d), lambda i, dest: (i // per_b, 0, 0))],
            out_specs=pl.BlockSpec((tm, d), lambda i, dest: (i, 0)),
            scratch_shapes=[pltpu.VMEM((2, tm, d), F32), pltpu.SemaphoreType.DMA(())]),
        out_shape=jax.ShapeDtypeStruct((t, d), F32),
        compiler_params=_cparams(("arbitrary",)),
        name="moe_combine",
    )(dest, ys, route, x, gate)


def hier_moe_layer(x, g, shift, scale, gate, wg, bg, we, be, w1, w3, w2, seq):
    t, d = x.shape
    wr = jnp.zeros((d, LANES), F32)
    wr = wr.at[:, :N_GROUPS].set(wg)
    wr = wr.at[:, N_GROUPS:N_GROUPS + N_EXPERTS].set(we.transpose(1, 0, 2).reshape(d, N_EXPERTS))
    br = jnp.zeros((1, LANES), F32)
    br = br.at[0, :N_GROUPS].set(bg).at[0, N_GROUPS:N_GROUPS + N_EXPERTS].set(be.reshape(-1))
    h, route, cnt = moe_router(x, g, shift, scale, wr, br, seq)
    counts = cnt[0, :N_EXPERTS].astype(I32)
    padded = ((counts + MOE_ROWS - 1) // MOE_ROWS) * MOE_ROWS
    pad_end = jnp.cumsum(padded)
    pad_start = pad_end - padded
    eid = route[:, 0:2].astype(I32)
    rank = route[:, 4:6].astype(I32)
    dest = (pad_start[eid] + rank).reshape(-1)
    p_rows = 2 * t + N_EXPERTS * MOE_ROWS
    nblk = p_rows // MOE_ROWS
    blk_exp = jnp.clip(jnp.searchsorted(pad_end, jnp.arange(nblk, dtype=I32) * MOE_ROWS,
                                        side="right"), 0, N_EXPERTS - 1).astype(I32)
    n_used = (pad_end[-1:] // MOE_ROWS).astype(I32)
    xs = moe_dispatch(dest, h, zero_rows(p_rows, d))
    ys = moe_experts(blk_exp, n_used, xs, w1, w3, w2)
    return moe_combine(dest, ys, route, x, gate, seq)


def _even_weight(w_in):
    d = w_in.shape[0]
    sizes = [512, 512, 512, 512, 4, 4, 512, 512, 512, 256, 64, 4]
    offs = np.concatenate([[0], np.cumsum(sizes)])
    qa, ka, va, za, ba, aa, qb, kb, vb, qi, ki, wi = (w_in[:, offs[j]:offs[j + 1]]
                                                      for j in range(len(sizes)))
    pad = jnp.zeros((d, EV_N - EV_SM - 76), w_in.dtype)
    return jnp.concatenate([qa, ka, va, za, qb, kb, vb, qi, ki, ba, aa, wi, pad], axis=1)


def kernel(x, c, positions, ada_w, ada_b, norm_mix, norm_ffn, even_w_in, gdn_conv_w, gdn_a_log, gdn_dt_bias, gdn_out_norm, even_w_out, gla_w_in, gla_w_gate2, gla_b_gate2, gla_out_norm, gla_w_out, router_group_w, router_group_b, router_exp_w, router_exp_b, exp_w1, exp_w3, exp_w2, final_norm):
    batch, seq, d = x.shape
    depth = ada_w.shape[0]
    t = batch * seq
    xt = x.reshape(t, d)
    mod = ada_modulation(c, ada_w, ada_b)
    tables = rope_tables(positions)
    for layer in range(depth):
        sh_m, sc_m, gt_m, sh_f, sc_f, gt_f = (mod[layer, :, j * d:(j + 1) * d].reshape(batch, 1, d)
                                              for j in range(6))
        i = layer // 2
        g_mix = norm_mix[layer].reshape(1, d)
        if layer % 2 == 0:
            w_in = _even_weight(even_w_in[i]).astype(BF16)
            proj = in_projection(xt, g_mix, sh_m, sc_m, w_in, seq)
            oa = gdn_mixer(proj, gdn_conv_w[i], gdn_a_log[i], gdn_dt_bias[i], gdn_out_norm[i],
                           batch, seq)
            qb, kb, vb, qi, ki = dsa_prep(proj, tables)
            ob = dsa_attention(proj, qb, kb, vb, qi, ki, batch, seq)
            w_out = even_w_out[i].astype(BF16)
            half = GDN_HEADS * GDN_DK
            xt = out_projection([oa, ob], [w_out[:half], w_out[half:]], xt, gt_m, seq)
        else:
            w_in = jnp.pad(gla_w_in[i], ((0, 0), (0, OD_N - gla_w_in.shape[2]))).astype(BF16)
            proj = in_projection(xt, g_mix, sh_m, sc_m, w_in, seq)
            o = gla_mixer(proj, gla_w_gate2[i], gla_b_gate2[i], gla_out_norm[i], batch, seq)
            xt = out_projection([o], [gla_w_out[i].astype(BF16)], xt, gt_m, seq)
        xt = hier_moe_layer(xt, norm_ffn[layer].reshape(1, d), sh_f, sc_f, gt_f,
                            router_group_w[layer], router_group_b[layer],
                            router_exp_w[layer], router_exp_b[layer],
                            exp_w1[layer], exp_w3[layer], exp_w2[layer], seq)
    return final_rms_norm(xt, final_norm.reshape(1, d)).reshape(batch, seq, d)
```

```python
import functools

import jax
import jax.numpy as jnp
import numpy as np
from jax import lax
from jax.experimental import pallas as pl
from jax.experimental.pallas import tpu as pltpu

F32 = jnp.float32
BF16 = jnp.bfloat16
I32 = jnp.int32
HI = lax.Precision.HIGHEST

LANES = 128
SUBLANES = 8
VMEM_LIMIT = 56 * 1024 * 1024

EPS = 1e-6
ROPE_THETA = 10000.0
GDN_HEADS, GDN_DK = 4, 128
CONV_WIDTH = 4
GDN_CHUNK = 64
ATT_HEADS, ATT_DH = 4, 128
IDX_HEADS, IDX_DH = 4, 64
TOPK_MAX = 256
Q_BLOCK = 128
GLA_HEADS, GLA_DK, GLA_DV = 4, 128, 256
GLA_RANK = 16
GLA_TAU = 16.0
GLA_CHUNK = 64
GLA_SUB = 16
N_GROUPS, EXPERTS_PER_GROUP = 4, 8
N_EXPERTS = N_GROUPS * EXPERTS_PER_GROUP
MOE_ROWS = 256

EV_QA, EV_KA, EV_VA, EV_ZA = 0, 512, 1024, 1536
EV_QB, EV_KB, EV_VB, EV_QI, EV_SM = 2048, 2560, 3072, 3584, 3840
EV_N = 3968
SM_KI, SM_BA, SM_AA, SM_WI = 0, 64, 68, 72
OD_Q, OD_K, OD_V, OD_R, OD_G = 0, 512, 1024, 2048, 3072
OD_N = 3200

NEG_INF = float("-inf")
INT_MIN = -(2 ** 31)
KEY_NEG_INF = INT_MIN + 0x7FFFFF


def _cparams(sem):
    return pltpu.CompilerParams(dimension_semantics=sem, vmem_limit_bytes=VMEM_LIMIT)


def _sigmoid(x):
    return 1.0 / (1.0 + jnp.exp(-x))


def _silu(x):
    return x * _sigmoid(x)


def _softplus(x):
    return jnp.maximum(x, 0.0) + jnp.log(1.0 + jnp.exp(-jnp.abs(x)))


def _dot(a, b, precision=None):
    return jnp.dot(a, b, precision=precision, preferred_element_type=F32)


def _dot_nt(a, b, precision=None):
    return lax.dot_general(a, b, (((1,), (1,)), ((), ())), precision=precision,
                           preferred_element_type=F32)


def _dot_tn(a, b, precision=None):
    return lax.dot_general(a, b, (((0,), (0,)), ((), ())), precision=precision,
                           preferred_element_type=F32)


def _bdot(a, b):
    return _dot(a.astype(BF16), b.astype(BF16))


def _bdot_nt(a, b):
    return _dot_nt(a.astype(BF16), b.astype(BF16))


def _bdot_tn(a, b):
    return _dot_tn(a.astype(BF16), b.astype(BF16))


def _split2(a):
    hi = a.astype(BF16)
    return hi, (a - hi.astype(F32)).astype(BF16)


def _dot3(a, b):
    ah, al = _split2(a)
    bh, bl = _split2(b)
    return _dot(ah, bh) + (_dot(ah, bl) + _dot(al, bh))


def _dot_mask(mask_b, x):
    x0 = x.astype(BF16)
    r1 = x - x0.astype(F32)
    x1 = r1.astype(BF16)
    x2 = (r1 - x1.astype(F32)).astype(BF16)
    return _dot(mask_b, x0) + (_dot(mask_b, x1) + _dot(mask_b, x2))


def _norm_mod(x, g, shift, scale):
    ms = jnp.mean(x * x, axis=-1, keepdims=True)
    y = x * lax.rsqrt(ms + EPS) * g
    return y * (1.0 + scale) + shift


def _ada_kernel(c_ref, w_ref, b_ref, o_ref):
    o_ref[0] = _dot(_silu(c_ref[...]), w_ref[0], HI) + b_ref[0]


def ada_modulation(c, ada_w, ada_b):
    depth, d, n = ada_w.shape
    b = c.shape[0]
    cp = jnp.zeros((SUBLANES, d), F32).at[:b].set(c)
    tn = 1536
    out = pl.pallas_call(
        _ada_kernel,
        grid=(depth, n // tn),
        in_specs=[pl.BlockSpec((SUBLANES, d), lambda l, j: (0, 0)),
                  pl.BlockSpec((1, d, tn), lambda l, j: (l, 0, j)),
                  pl.BlockSpec((1, 1, tn), lambda l, j: (l, 0, j))],
        out_specs=pl.BlockSpec((1, SUBLANES, tn), lambda l, j: (l, 0, j)),
        out_shape=jax.ShapeDtypeStruct((depth, SUBLANES, n), F32),
        compiler_params=_cparams(("arbitrary", "arbitrary")),
        name="ada_modulation",
    )(cp, ada_w, ada_b.reshape(depth, 1, n))
    return out[:, :b]


def _inproj_kernel(x_ref, g_ref, sh_ref, sc_ref, w_ref, o_ref):
    h = _norm_mod(x_ref[...], g_ref[...], sh_ref[0], sc_ref[0])
    o_ref[...] = _dot(h.astype(BF16), w_ref[...])


def in_projection(x, g, shift, scale, w, seq):
    t, d = x.shape
    n = w.shape[1]
    tm = 512
    per_b = seq // tm
    return pl.pallas_call(
        _inproj_kernel,
        grid=(t // tm,),
        in_specs=[pl.BlockSpec((tm, d), lambda i: (i, 0)),
                  pl.BlockSpec((1, d), lambda i: (0, 0)),
                  pl.BlockSpec((1, 1, d), lambda i: (i // per_b, 0, 0)),
                  pl.BlockSpec((1, 1, d), lambda i: (i // per_b, 0, 0)),
                  pl.BlockSpec((d, n), lambda i: (0, 0))],
        out_specs=pl.BlockSpec((tm, n), lambda i: (i, 0)),
        out_shape=jax.ShapeDtypeStruct((t, n), F32),
        compiler_params=_cparams(("arbitrary",)),
        name="in_projection",
    )(x, g, shift, scale, w)


def _outproj_kernel(*refs, n_in):
    o_refs = refs[:n_in]
    w_refs = refs[n_in:2 * n_in]
    x_ref, gt_ref, xo_ref = refs[2 * n_in:]
    acc = _dot(o_refs[0][...].astype(BF16), w_refs[0][...])
    for o_ref, w_ref in zip(o_refs[1:], w_refs[1:]):
        acc = acc + _dot(o_ref[...].astype(BF16), w_ref[...])
    xo_ref[...] = x_ref[...] + gt_ref[0] * acc


def out_projection(parts, weights, x, gate, seq):
    t, d = x.shape
    tm = 512
    per_b = seq // tm
    n_in = len(parts)
    in_specs = [pl.BlockSpec((tm, p.shape[1]), lambda i: (i, 0)) for p in parts]
    in_specs += [pl.BlockSpec(w.shape, lambda i: (0, 0)) for w in weights]
    in_specs += [pl.BlockSpec((tm, d), lambda i: (i, 0)),
                 pl.BlockSpec((1, 1, d), lambda i: (i // per_b, 0, 0))]
    return pl.pallas_call(
        functools.partial(_outproj_kernel, n_in=n_in),
        grid=(t // tm,),
        in_specs=in_specs,
        out_specs=pl.BlockSpec((tm, d), lambda i: (i, 0)),
        out_shape=jax.ShapeDtypeStruct((t, d), F32),
        compiler_params=_cparams(("arbitrary",)),
        name="out_projection",
    )(*parts, *weights, x, gate)


def _final_norm_kernel(x_ref, g_ref, o_ref):
    x = x_ref[...]
    ms = jnp.mean(x * x, axis=-1, keepdims=True)
    o_ref[...] = x * lax.rsqrt(ms + EPS) * g_ref[...]


def final_rms_norm(x, g):
    t, d = x.shape
    tm = 512
    return pl.pallas_call(
        _final_norm_kernel,
        grid=(t // tm,),
        in_specs=[pl.BlockSpec((tm, d), lambda i: (i, 0)),
                  pl.BlockSpec((1, d), lambda i: (0, 0))],
        out_specs=pl.BlockSpec((tm, d), lambda i: (i, 0)),
        out_shape=jax.ShapeDtypeStruct((t, d), F32),
        compiler_params=_cparams(("arbitrary",)),
        name="final_norm",
    )(x, g)


def _tri_masks(c):
    row = lax.broadcasted_iota(I32, (c, c), 0)
    col = lax.broadcasted_iota(I32, (c, c), 1)
    return row >= col, row > col, row == col


def _gdn_kernel(qkv_ref, za_ref, sm_ref, cw_ref, nega_ref, dtb_ref, on_ref, o_ref,
                tail_ref, state_ref, *, rb):
    c = GDN_CHUNK
    dk = GDN_DK

    @pl.when(pl.program_id(1) == 0)
    def _():
        tail_ref[...] = jnp.zeros_like(tail_ref)
        state_ref[...] = jnp.zeros_like(state_ref)

    x = qkv_ref[...]
    xx = jnp.concatenate([tail_ref[...], x], axis=0)
    tail_ref[...] = x[rb - SUBLANES:, :]
    cw = cw_ref[...]
    y = x * cw[CONV_WIDTH - 1:CONV_WIDTH, :]
    for i in range(1, CONV_WIDTH):
        y = y + pltpu.roll(xx, i, axis=0)[SUBLANES:, :] * cw[CONV_WIDTH - 1 - i:CONV_WIDTH - i, :]
    y = _silu(y)

    sm = sm_ref[...]
    beta_t = _sigmoid(sm)
    g_t = nega_ref[...] * _softplus(sm + dtb_ref[...])
    za = za_ref[...]
    incl, strict, diag = _tri_masks(c)
    incl_b = incl.astype(F32).astype(BF16)
    strict_f = strict.astype(F32)
    eye = diag.astype(F32)
    hw = GDN_HEADS * dk

    for ci in range(rb // c):
        r0 = ci * c
        gc_all = _dot_mask(incl_b, g_t[r0:r0 + c])
        for h in range(GDN_HEADS):
            lo = h * dk
            q = y[r0:r0 + c, lo:lo + dk]
            k = y[r0:r0 + c, hw + lo:hw + lo + dk]
            v = y[r0:r0 + c, 2 * hw + lo:2 * hw + lo + dk]
            q = q * lax.rsqrt(jnp.sum(q * q, axis=-1, keepdims=True) + EPS) * dk ** -0.5
            k = k * lax.rsqrt(jnp.sum(k * k, axis=-1, keepdims=True) + EPS)
            gb = jnp.broadcast_to(g_t[r0:r0 + c, SM_AA + h:SM_AA + h + 1], (c, c))
            bb = jnp.broadcast_to(beta_t[r0:r0 + c, SM_BA + h:SM_BA + h + 1], (c, dk))
            gc = jnp.broadcast_to(gc_all[:, SM_AA + h:SM_AA + h + 1], (c, dk))
            dmat = _dot_mask(incl_b, gb * strict_f)
            decay = jnp.where(incl, jnp.exp(jnp.where(incl, dmat, 0.0)), 0.0)
            kbeta = k * bb
            lmat = jnp.where(strict, _bdot_nt(kbeta, k) * decay, 0.0)
            ypow = -lmat
            tmat = eye + ypow
            for _ in range(int(np.log2(c)) - 1):
                ypow = _dot3(ypow, ypow)
                tmat = tmat + _dot3(tmat, ypow)
            egc = jnp.exp(gc)
            wu = _bdot(tmat, jnp.concatenate([kbeta * egc, v * bb], axis=1))
            w, u = wu[:, :dk], wu[:, dk:]
            aqk = _bdot_nt(q, k) * decay
            gl = gc[c - 1:c, :]
            kd = k * jnp.exp(gl - gc)
            st = state_ref[h]
            ws = _bdot(jnp.concatenate([w, q * egc], axis=0), st)
            v_new = u - ws[:c]
            o = ws[c:] + _bdot(aqk, v_new)
            state_ref[h] = st * jnp.exp(gl) + _bdot_tn(kd, v_new)
            o = o * lax.rsqrt(jnp.mean(o * o, axis=-1, keepdims=True) + EPS) * on_ref[...]
            o_ref[r0:r0 + c, lo:lo + dk] = o * _silu(za[r0:r0 + c, lo:lo + dk])


def gdn_mixer(proj, conv_w, a_log, dt_bias, out_norm, batch, seq):
    t = proj.shape[0]
    rb = 128
    nsb = seq // rb
    hw = GDN_HEADS * GDN_DK
    nega = jnp.zeros((1, LANES), F32).at[0, SM_AA:SM_AA + GDN_HEADS].set(-jnp.exp(a_log))
    dtb = jnp.zeros((1, LANES), F32).at[0, SM_AA:SM_AA + GDN_HEADS].set(dt_bias)
    row = lambda b, s: b * nsb + s
    return pl.pallas_call(
        functools.partial(_gdn_kernel, rb=rb),
        grid=(batch, nsb),
        in_specs=[pl.BlockSpec((rb, 3 * hw), lambda b, s: (row(b, s), 0)),
                  pl.BlockSpec((rb, hw), lambda b, s: (row(b, s), EV_ZA // hw)),
                  pl.BlockSpec((rb, LANES), lambda b, s: (row(b, s), EV_SM // LANES)),
                  pl.BlockSpec((CONV_WIDTH, 3 * hw), lambda b, s: (0, 0)),
                  pl.BlockSpec((1, LANES), lambda b, s: (0, 0)),
                  pl.BlockSpec((1, LANES), lambda b, s: (0, 0)),
                  pl.BlockSpec((1, GDN_DK), lambda b, s: (0, 0))],
        out_specs=pl.BlockSpec((rb, hw), lambda b, s: (row(b, s), 0)),
        out_shape=jax.ShapeDtypeStruct((t, hw), F32),
        scratch_shapes=[pltpu.VMEM((SUBLANES, 3 * hw), F32),
                        pltpu.VMEM((GDN_HEADS, GDN_DK, GDN_DK), F32)],
        compiler_params=_cparams(("arbitrary", "arbitrary")),
        name="gdn_mixer",
    )(proj, proj, proj, conv_w, nega, dtb, out_norm.reshape(1, GDN_DK))


def _gla_kernel(qk_ref, v_ref, r_ref, gl_ref, wg_ref, bg_ref, on_ref, o_ref, state_ref, *, rb):
    c = GLA_CHUNK
    dk, dv = GLA_DK, GLA_DV
    sb = GLA_SUB

    @pl.when(pl.program_id(1) == 0)
    def _():
        state_ref[...] = jnp.zeros_like(state_ref)

    z = _dot3(gl_ref[...], wg_ref[...]) + bg_ref[...]
    log_a = -_softplus(-z) * (1.0 / GLA_TAU)
    qk = qk_ref[...]
    vv = v_ref[...]
    rr = r_ref[...]
    incl, _, _ = _tri_masks(c)
    incl_b = incl.astype(F32).astype(BF16)
    hw = GLA_HEADS * dk
    sub_row = lax.broadcasted_iota(I32, (sb, dk), 0)
    att_lane = lax.broadcasted_iota(I32, (sb, c), 1)

    for ci in range(rb // c):
        r0 = ci * c
        b_all = _dot_mask(incl_b, log_a[r0:r0 + c])
        for h in range(GLA_HEADS):
            q = qk[r0:r0 + c, h * dk:(h + 1) * dk] * dk ** -0.5
            k = qk[r0:r0 + c, hw + h * dk:hw + (h + 1) * dk]
            v = vv[r0:r0 + c, h * dv:(h + 1) * dv]
            b = b_all[:, h * dk:(h + 1) * dk]
            st = state_ref[h]
            o = _bdot_nt(q * jnp.exp(b), st)
            att_rows = []
            for bi in range(c // sb):
                s0 = bi * sb
                qi = q[s0:s0 + sb]
                bi_ = b[s0:s0 + sb]
                ref = b[s0:s0 + 1]
                if bi > 0:
                    qt = qi * jnp.exp(bi_ - ref)
                    kt = k * jnp.exp(jnp.minimum(ref - b, 0.0))
                    att = jnp.where(att_lane < s0, _bdot_nt(qt, kt), 0.0)
                else:
                    att = jnp.zeros((sb, c), F32)
                for j in range(sb):
                    kj = k[s0 + j:s0 + j + 1]
                    bj = b[s0 + j:s0 + j + 1]
                    e = jnp.exp(jnp.where(sub_row >= j, bi_ - bj, NEG_INF))
                    col = jnp.sum(qi * kj * e, axis=-1, keepdims=True)
                    att = jnp.where(att_lane == s0 + j, col, att)
                att_rows.append(att)
            o = o + _bdot(jnp.concatenate(att_rows, axis=0), v)
            bl = b[c - 1:c]
            state_ref[h] = st * jnp.exp(bl) + _bdot_tn(v, k * jnp.exp(bl - b))
            o = o * lax.rsqrt(jnp.mean(o * o, axis=-1, keepdims=True) + EPS) * on_ref[...]
            o_ref[r0:r0 + c, h * dv:(h + 1) * dv] = o * _silu(rr[r0:r0 + c, h * dv:(h + 1) * dv])


def gla_mixer(proj, w_gate2, b_gate2, out_norm, batch, seq):
    t = proj.shape[0]
    rb = GLA_CHUNK
    nsb = seq // rb
    hw = GLA_HEADS * GLA_DK
    hv = GLA_HEADS * GLA_DV
    wg = jnp.zeros((LANES, hw), F32).at[:GLA_RANK].set(w_gate2)
    row = lambda b, s: b * nsb + s
    return pl.pallas_call(
        functools.partial(_gla_kernel, rb=rb),
        grid=(batch, nsb),
        in_specs=[pl.BlockSpec((rb, 2 * hw), lambda b, s: (row(b, s), 0)),
                  pl.BlockSpec((rb, hv), lambda b, s: (row(b, s), OD_V // hv)),
                  pl.BlockSpec((rb, hv), lambda b, s: (row(b, s), OD_R // hv)),
                  pl.BlockSpec((rb, LANES), lambda b, s: (row(b, s), OD_G // LANES)),
                  pl.BlockSpec((LANES, hw), lambda b, s: (0, 0)),
                  pl.BlockSpec((1, hw), lambda b, s: (0, 0)),
                  pl.BlockSpec((1, GLA_DV), lambda b, s: (0, 0))],
        out_specs=pl.BlockSpec((rb, hv), lambda b, s: (row(b, s), 0)),
        out_shape=jax.ShapeDtypeStruct((t, hv), F32),
        scratch_shapes=[pltpu.VMEM((GLA_HEADS, GLA_DV, GLA_DK), F32)],
        compiler_params=_cparams(("arbitrary", "arbitrary")),
        name="gla_mixer",
    )(proj, proj, proj, proj, wg, b_gate2.reshape(1, hw), out_norm.reshape(1, GLA_DV))


def _rope_table_kernel(pos_ref, inva_ref, invi_ref, ca_ref, sa_ref, ci_ref, si_ref):
    pos = pos_ref[...].astype(F32)
    lane = lax.broadcasted_iota(I32, (pos.shape[0], LANES), 1)
    ang_a = pos * inva_ref[...]
    ca_ref[...] = jnp.cos(ang_a)
    sa_ref[...] = jnp.where(lane < ATT_DH // 2, -1.0, 1.0) * jnp.sin(ang_a)
    ang_i = pos * invi_ref[...]
    ci_ref[...] = jnp.cos(ang_i)
    si_ref[...] = jnp.where(lane % IDX_DH < IDX_DH // 2, -1.0, 1.0) * jnp.sin(ang_i)


def rope_tables(positions):
    t = positions.size
    inv_a = 1.0 / (ROPE_THETA ** (jnp.arange(0, ATT_DH, 2, dtype=F32) / ATT_DH))
    inv_i = 1.0 / (ROPE_THETA ** (jnp.arange(0, IDX_DH, 2, dtype=F32) / IDX_DH))
    inv_a = jnp.tile(inv_a, 2).reshape(1, LANES)
    inv_i = jnp.tile(inv_i, 4).reshape(1, LANES)
    tm = 512
    spec = pl.BlockSpec((tm, LANES), lambda i: (i, 0))
    vec = pl.BlockSpec((1, LANES), lambda i: (0, 0))
    return pl.pallas_call(
        _rope_table_kernel,
        grid=(t // tm,),
        in_specs=[pl.BlockSpec((tm, 1), lambda i: (i, 0)), vec, vec],
        out_specs=[spec] * 4,
        out_shape=[jax.ShapeDtypeStruct((t, LANES), F32)] * 4,
        compiler_params=_cparams(("arbitrary",)),
        name="rope_tables",
    )(positions.reshape(t, 1), inv_a, inv_i)


def _dsa_prep_kernel(qb_ref, kb_ref, vb_ref, qi_ref, sm_ref, ca_ref, sa_ref, ci_ref, si_ref,
                     qo_ref, ko_ref, vo_ref, qio_ref, kio_ref):
    ca, sa, ci, si = ca_ref[...], sa_ref[...], ci_ref[...], si_ref[...]
    lane = lax.broadcasted_iota(I32, ca.shape, 1)
    first_half = lane % IDX_DH < IDX_DH // 2

    def rope_att(tile):
        return tile * ca + pltpu.roll(tile, ATT_DH // 2, axis=1) * sa

    def rope_idx(tile):
        partner = jnp.where(first_half, pltpu.roll(tile, LANES - IDX_DH // 2, axis=1),
                            pltpu.roll(tile, IDX_DH // 2, axis=1))
        return tile * ci + partner * si

    qb, kb = qb_ref[...], kb_ref[...]
    for h in range(ATT_HEADS):
        sl = slice(h * ATT_DH, (h + 1) * ATT_DH)
        qo_ref[:, sl] = (rope_att(qb[:, sl]) * ATT_DH ** -0.5).astype(BF16)
        ko_ref[:, sl] = rope_att(kb[:, sl]).astype(BF16)
    vo_ref[...] = vb_ref[...].astype(BF16)
    qi = qi_ref[...]
    for p in range(IDX_HEADS * IDX_DH // LANES):
        sl = slice(p * LANES, (p + 1) * LANES)
        qio_ref[:, sl] = rope_idx(qi[:, sl]).astype(BF16)
    ki = rope_idx(sm_ref[...])
    kio_ref[...] = jnp.where(lane < IDX_DH, ki, pltpu.roll(ki, IDX_DH, axis=1)).astype(BF16)


def dsa_prep(proj, tables):
    t = proj.shape[0]
    tm = 512
    aw = ATT_HEADS * ATT_DH
    iw = IDX_HEADS * IDX_DH
    tab = pl.BlockSpec((tm, LANES), lambda i: (i, 0))
    return pl.pallas_call(
        _dsa_prep_kernel,
        grid=(t // tm,),
        in_specs=[pl.BlockSpec((tm, aw), lambda i: (i, EV_QB // aw)),
                  pl.BlockSpec((tm, aw), lambda i: (i, EV_KB // aw)),
                  pl.BlockSpec((tm, aw), lambda i: (i, EV_VB // aw)),
                  pl.BlockSpec((tm, iw), lambda i: (i, EV_QI // iw)),
                  pl.BlockSpec((tm, LANES), lambda i: (i, EV_SM // LANES)),
                  tab, tab, tab, tab],
        out_specs=[pl.BlockSpec((tm, aw), lambda i: (i, 0))] * 3
        + [pl.BlockSpec((tm, iw), lambda i: (i, 0)), pl.BlockSpec((tm, LANES), lambda i: (i, 0))],
        out_shape=[jax.ShapeDtypeStruct((t, aw), BF16)] * 3
        + [jax.ShapeDtypeStruct((t, iw), BF16), jax.ShapeDtypeStruct((t, LANES), BF16)],
        compiler_params=_cparams(("arbitrary",)),
        name="dsa_prep",
    )(proj, proj, proj, proj, proj, *tables)


def _dsa_kernel(qb_ref, qi_ref, sm_ref, k_ref, v_ref, ki_ref, o_ref,
                key_ref, m_ref, l_ref, acc_ref, *, kc, topk):
    qb_rows = Q_BLOCK
    i = pl.program_id(1)
    nkc = (i * qb_rows + qb_rows + kc - 1) // kc
    q_pos = i * qb_rows + lax.broadcasted_iota(I32, (qb_rows, 1), 0)
    lane = lax.broadcasted_iota(I32, (qb_rows, LANES), 1)
    col0 = lax.broadcasted_iota(I32, (qb_rows, kc), 1)
    wi = sm_ref[...] * (IDX_HEADS * IDX_DH) ** -0.5
    qi = qi_ref[...]
    zero_b = jnp.zeros((qb_rows, LANES), BF16)
    qi_heads = []
    for h in range(IDX_HEADS):
        pair = qi[:, (h // 2) * LANES:(h // 2 + 1) * LANES]
        own = (lane < IDX_DH) if h % 2 == 0 else (lane >= IDX_DH)
        qi_heads.append(jnp.where(own, pair, zero_b))

    def score_body(c, carry):
        start = pl.multiple_of(c * kc, kc)
        kic = ki_ref[0, pl.ds(start, kc), :]
        s = jnp.zeros((qb_rows, kc), F32)
        for h in range(IDX_HEADS):
            d = _dot_nt(qi_heads[h], kic)
            s = s + wi[:, SM_WI + h:SM_WI + h + 1] * jnp.maximum(d, 0.0)
        s = s + 0.0
        s = jnp.where(c * kc + col0 <= q_pos, s, NEG_INF)
        bits = pltpu.bitcast(s, I32)
        key_ref[c] = bits ^ ((bits >> 31) & 0x7FFFFFFF)
        return carry

    lax.fori_loop(0, nkc, score_body, 0)

    def lane_fold(x):
        acc = x[:, :LANES]
        for j in range(1, kc // LANES):
            acc = acc + x[:, j * LANES:(j + 1) * LANES]
        return acc

    def count(pred):
        def body(c, acc):
            return acc + lane_fold(jnp.where(pred(key_ref[c], c), 1, 0).astype(I32))
        acc = lax.fori_loop(0, nkc, body, jnp.zeros((qb_rows, LANES), I32))
        return jnp.sum(acc, axis=1, keepdims=True)

    thr = jnp.where(count(lambda key, c: key >= 0) >= topk, 0, INT_MIN).astype(I32)

    def bit_body(it, thr):
        cand = thr + (jnp.int32(1) << (30 - it))
        return jnp.where(count(lambda key, c: key >= cand) >= topk, cand, thr)

    thr = lax.fori_loop(0, 31, bit_body, thr)

    n_gt = count(lambda key, c: key > thr)
    n_eq = count(lambda key, c: key == thr)
    need = topk - n_gt
    big = jnp.int32(2 ** 30)

    def tie_search():
        def idx_body(it, x):
            cand = x + (jnp.int32(1) << (14 - it))
            below = count(lambda key, c: (key == thr) & (c * kc + col0 < cand))
            return jnp.where(below < need, cand, x)
        return lax.fori_loop(0, 15, idx_body, jnp.zeros((qb_rows, 1), I32))

    any_tie = jnp.max(jnp.where(n_eq > need, 1, 0)) > 0
    last = lax.cond(any_tie, tie_search, lambda: jnp.full((qb_rows, 1), big, I32))

    last = jnp.where(thr == KEY_NEG_INF, -1, last)
    zero_f = jnp.float32(0.0)
    ninf_f = jnp.float32(NEG_INF)

    def bias_body(c, carry):
        key = key_ref[c]
        tie_ok = jnp.where(c * kc + col0 <= last, zero_f, ninf_f)
        bias = jnp.where(key > thr, zero_f, jnp.where(key == thr, tie_ok, ninf_f))
        key_ref[c] = pltpu.bitcast(bias, I32)
        return carry

    lax.fori_loop(0, nkc, bias_body, 0)
    qb = qb_ref[...]
    m_ref[...] = jnp.full(m_ref.shape, NEG_INF, F32)
    l_ref[...] = jnp.zeros(l_ref.shape, F32)
    acc_ref[...] = jnp.zeros(acc_ref.shape, F32)

    def chunk_logits(c, h):
        start = pl.multiple_of(c * kc, kc)
        kch = k_ref[0, pl.ds(start, kc), h * ATT_DH:(h + 1) * ATT_DH]
        return _dot_nt(qb[:, h * ATT_DH:(h + 1) * ATT_DH], kch) + pltpu.bitcast(key_ref[c], F32)

    def lane_fold_max(x):
        acc = x[:, :LANES]
        for j in range(1, kc // LANES):
            acc = jnp.maximum(acc, x[:, j * LANES:(j + 1) * LANES])
        return acc

    def max_body(c, carry):
        for h in range(ATT_HEADS):
            m_ref[h] = jnp.maximum(m_ref[h], lane_fold_max(chunk_logits(c, h)))
        return carry

    lax.fori_loop(0, nkc, max_body, 0)
    m_rows = [jnp.max(m_ref[h], axis=1, keepdims=True) for h in range(ATT_HEADS)]

    def pv_body(c, carry):
        start = pl.multiple_of(c * kc, kc)
        for h in range(ATT_HEADS):
            p = jnp.exp(chunk_logits(c, h) - m_rows[h])
            l_ref[h] = l_ref[h] + lane_fold(p)
            vch = v_ref[0, pl.ds(start, kc), h * ATT_DH:(h + 1) * ATT_DH]
            acc_ref[h] = acc_ref[h] + _dot(p.astype(BF16), vch)
        return carry

    lax.fori_loop(0, nkc, pv_body, 0)
    for h in range(ATT_HEADS):
        denom = jnp.sum(l_ref[h], axis=1, keepdims=True)
        o_ref[:, h * ATT_DH:(h + 1) * ATT_DH] = acc_ref[h] / denom


def dsa_attention(proj, qb, kb, vb, qi, ki, batch, seq):
    t = proj.shape[0]
    aw = ATT_HEADS * ATT_DH
    iw = IDX_HEADS * IDX_DH
    kc = min(512, seq)
    topk = min(TOPK_MAX, seq // 4)
    nq = seq // Q_BLOCK
    row = lambda b, i: b * nq + i
    kb3, vb3, ki3 = (a.reshape(batch, seq, a.shape[1]) for a in (kb, vb, ki))
    resident = lambda w: pl.BlockSpec((1, seq, w), lambda b, i: (b, 0, 0),
                                      pipeline_mode=pl.Buffered(1))
    return pl.pallas_call(
        functools.partial(_dsa_kernel, kc=kc, topk=topk),
        grid=(batch, nq),
        in_specs=[pl.BlockSpec((Q_BLOCK, aw), lambda b, i: (row(b, i), 0)),
                  pl.BlockSpec((Q_BLOCK, iw), lambda b, i: (row(b, i), 0)),
                  pl.BlockSpec((Q_BLOCK, LANES), lambda b, i: (row(b, i), EV_SM // LANES)),
                  resident(aw), resident(aw), resident(LANES)],
        out_specs=pl.BlockSpec((Q_BLOCK, aw), lambda b, i: (row(b, i), 0)),
        out_shape=jax.ShapeDtypeStruct((t, aw), F32),
        scratch_shapes=[pltpu.VMEM((seq // kc, Q_BLOCK, kc), I32),
                        pltpu.VMEM((ATT_HEADS, Q_BLOCK, LANES), F32),
                        pltpu.VMEM((ATT_HEADS, Q_BLOCK, LANES), F32),
                        pltpu.VMEM((ATT_HEADS, Q_BLOCK, ATT_DH), F32)],
        compiler_params=_cparams(("arbitrary", "arbitrary")),
        name="dsa_attention",
    )(qb, qi, proj, kb3, vb3, ki3)


def _router_kernel(x_ref, g_ref, sh_ref, sc_ref, wr_ref, br_ref, h_ref, route_ref, cnt_ref,
                   carry_ref):
    @pl.when(pl.program_id(0) == 0)
    def _():
        carry_ref[...] = jnp.zeros_like(carry_ref)

    h = _norm_mod(x_ref[...], g_ref[...], sh_ref[0], sc_ref[0])
    h_ref[...] = h
    tm = h.shape[0]
    logits = _dot(h, wr_ref[...], HI) + br_ref[...]
    lane = lax.broadcasted_iota(I32, (tm, LANES), 1)
    far = jnp.int32(LANES)

    def first_max(vals):
        m = jnp.max(vals, axis=1, keepdims=True)
        return m, jnp.min(jnp.where(vals == m, lane, far), axis=1, keepdims=True)

    is_group = lane < N_GROUPS
    gmax, gsel = first_max(jnp.where(is_group, logits, NEG_INF))
    gsum = jnp.sum(jnp.where(is_group, jnp.exp(logits - gmax), 0.0), axis=1, keepdims=True)
    lo = N_GROUPS + EXPERTS_PER_GROUP * gsel
    el = jnp.where((lane >= lo) & (lane < lo + EXPERTS_PER_GROUP), logits, NEG_INF)
    m1, i1 = first_max(el)
    m2, i2 = first_max(jnp.where(lane == i1, NEG_INF, el))
    e2 = jnp.exp(m2 - m1)
    gate1 = 1.0 / (gsum * (1.0 + e2))
    gate2 = e2 * gate1
    eid1 = i1 - N_GROUPS
    eid2 = i2 - N_GROUPS
    oh1 = lane == eid1
    oh2 = lane == eid2
    onehot = jnp.where(oh1 | oh2, 1.0, 0.0)
    r = lax.broadcasted_iota(I32, (tm, tm), 0)
    cidx = lax.broadcasted_iota(I32, (tm, tm), 1)
    below = jnp.where(r > cidx, 1.0, 0.0).astype(BF16)
    cum = _dot(below, onehot.astype(BF16)) + carry_ref[0:1, :]
    rank1 = jnp.sum(jnp.where(oh1, cum, 0.0), axis=1, keepdims=True)
    rank2 = jnp.sum(jnp.where(oh2, cum, 0.0), axis=1, keepdims=True)
    carry_ref[...] = carry_ref[...] + jnp.sum(onehot, axis=0, keepdims=True)
    cnt_ref[...] = carry_ref[...]
    out = jnp.zeros((tm, LANES), F32)
    for j, val in enumerate((eid1.astype(F32), eid2.astype(F32), gate1, gate2, rank1, rank2)):
        out = jnp.where(lane == j, val, out)
    route_ref[...] = out


def moe_router(x, g, shift, scale, wr, br, seq):
    t, d = x.shape
    tm = 512
    per_b = seq // tm
    return pl.pallas_call(
        _router_kernel,
        grid=(t // tm,),
        in_specs=[pl.BlockSpec((tm, d), lambda i: (i, 0)),
                  pl.BlockSpec((1, d), lambda i: (0, 0)),
                  pl.BlockSpec((1, 1, d), lambda i: (i // per_b, 0, 0)),
                  pl.BlockSpec((1, 1, d), lambda i: (i // per_b, 0, 0)),
                  pl.BlockSpec((d, LANES), lambda i: (0, 0)),
                  pl.BlockSpec((1, LANES), lambda i: (0, 0))],
        out_specs=[pl.BlockSpec((tm, d), lambda i: (i, 0)),
                   pl.BlockSpec((tm, LANES), lambda i: (i, 0)),
                   pl.BlockSpec((SUBLANES, LANES), lambda i: (0, 0))],
        out_shape=[jax.ShapeDtypeStruct((t, d), F32),
                   jax.ShapeDtypeStruct((t, LANES), F32),
                   jax.ShapeDtypeStruct((SUBLANES, LANES), F32)],
        scratch_shapes=[pltpu.VMEM((SUBLANES, LANES), F32)],
        compiler_params=_cparams(("arbitrary",)),
        name="moe_router",
    )(x, g, shift, scale, wr, br)


def _zero_kernel(o_ref):
    o_ref[...] = jnp.zeros_like(o_ref)


def zero_rows(rows, d):
    tm = 1024
    return pl.pallas_call(
        _zero_kernel,
        grid=(rows // tm,),
        out_specs=pl.BlockSpec((tm, d), lambda i: (i, 0)),
        out_shape=jax.ShapeDtypeStruct((rows, d), F32),
        compiler_params=_cparams(("arbitrary",)),
        name="zero_rows",
    )()


def _dispatch_kernel(dest_ref, h_ref, xs_in_ref, xs_ref, sem, *, tm):
    del xs_in_ref
    base = pl.program_id(0) * tm * 2

    def row_copy(r, j):
        d = dest_ref[base + 2 * r + j]
        return pltpu.make_async_copy(h_ref.at[pl.ds(r, 1)], xs_ref.at[pl.ds(d, 1)], sem)

    def start(r, carry):
        row_copy(r, 0).start()
        row_copy(r, 1).start()
        return carry

    def wait(r, carry):
        row_copy(r, 0).wait()
        row_copy(r, 1).wait()
        return carry

    lax.fori_loop(0, tm, start, 0)
    lax.fori_loop(0, tm, wait, 0)


def moe_dispatch(dest, h, xs_zero):
    t, d = h.shape
    tm = 256
    return pl.pallas_call(
        functools.partial(_dispatch_kernel, tm=tm),
        grid_spec=pltpu.PrefetchScalarGridSpec(
            num_scalar_prefetch=1,
            grid=(t // tm,),
            in_specs=[pl.BlockSpec((tm, d), lambda i, dest: (i, 0)),
                      pl.BlockSpec(memory_space=pl.ANY)],
            out_specs=pl.BlockSpec(memory_space=pl.ANY),
            scratch_shapes=[pltpu.SemaphoreType.DMA(())]),
        out_shape=jax.ShapeDtypeStruct(xs_zero.shape, F32),
        input_output_aliases={2: 0},
        compiler_params=_cparams(("arbitrary",)),
        name="moe_dispatch",
    )(dest, h, xs_zero)


def _expert_kernel(be_ref, nb_ref, xs_ref, w1_ref, w3_ref, w2_ref, ys_ref, w1b, w3b, w2b):
    i = pl.program_id(0)
    prev = be_ref[jnp.maximum(i - 1, 0)]

    @pl.when((i == 0) | (be_ref[i] != prev))
    def _():
        w1b[...] = w1_ref[0, 0].astype(BF16)
        w3b[...] = w3_ref[0, 0].astype(BF16)
        w2b[...] = w2_ref[0, 0].astype(BF16)

    @pl.when(i < nb_ref[0])
    def _():
        x = xs_ref[...].astype(BF16)
        a = _dot(x, w1b[...])
        hid = _silu(a) * _dot(x, w3b[...])
        ys_ref[...] = _dot(hid.astype(BF16), w2b[...])

    @pl.when(i >= nb_ref[0])
    def _():
        ys_ref[...] = jnp.zeros_like(ys_ref)


def moe_experts(blk_exp, n_used, xs, w1, w3, w2, layer):
    p, d = xs.shape
    de = w1.shape[3]
    rows = MOE_ROWS
    return pl.pallas_call(
        _expert_kernel,
        grid_spec=pltpu.PrefetchScalarGridSpec(
            num_scalar_prefetch=2,
            grid=(p // rows,),
            in_specs=[pl.BlockSpec((rows, d), lambda i, be, nb: (i, 0)),
                      pl.BlockSpec((1, 1, d, de), lambda i, be, nb: (layer, be[i], 0, 0)),
                      pl.BlockSpec((1, 1, d, de), lambda i, be, nb: (layer, be[i], 0, 0)),
                      pl.BlockSpec((1, 1, de, d), lambda i, be, nb: (layer, be[i], 0, 0))],
            out_specs=pl.BlockSpec((rows, d), lambda i, be, nb: (i, 0)),
            scratch_shapes=[pltpu.VMEM((d, de), BF16), pltpu.VMEM((d, de), BF16),
                            pltpu.VMEM((de, d), BF16)]),
        out_shape=jax.ShapeDtypeStruct((p, d), F32),
        compiler_params=_cparams(("arbitrary",)),
        name="moe_experts",
    )(blk_exp, n_used, xs, w1, w3, w2)


def _combine_kernel(dest_ref, ys_ref, route_ref, x_ref, gt_ref, o_ref, buf_ref, sem, *, tm):
    base = pl.program_id(0) * tm * 2

    def row_copy(r, j):
        d = dest_ref[base + 2 * r + j]
        return pltpu.make_async_copy(ys_ref.at[pl.ds(d, 1)], buf_ref.at[j, pl.ds(r, 1)], sem)

    def start(r, carry):
        row_copy(r, 0).start()
        row_copy(r, 1).start()
        return carry

    def wait(r, carry):
        row_copy(r, 0).wait()
        row_copy(r, 1).wait()
        return carry

    lax.fori_loop(0, tm, start, 0)
    lax.fori_loop(0, tm, wait, 0)
    route = route_ref[...]
    y = route[:, 2:3] * buf_ref[0] + route[:, 3:4] * buf_ref[1]
    o_ref[...] = x_ref[...] + gt_ref[0] * y


def moe_combine(dest, ys, route, x, gate, seq):
    t, d = x.shape
    tm = 256
    per_b = seq // tm
    return pl.pallas_call(
        functools.partial(_combine_kernel, tm=tm),
        grid_spec=pltpu.PrefetchScalarGridSpec(
            num_scalar_prefetch=1,
            grid=(t // tm,),
            in_specs=[pl.BlockSpec(memory_space=pl.ANY),
                      pl.BlockSpec((tm, LANES), lambda i, dest: (i, 0)),
                      pl.BlockSpec((tm, d), lambda i, dest: (i, 0)),
                      pl.BlockSpec((1, 1, d), lambda i, dest: (i // per_b, 0, 0))],
            out_specs=pl.BlockSpec((tm, d), lambda i, dest: (i, 0)),
            scratch_shapes=[pltpu.VMEM((2, tm, d), F32), pltpu.SemaphoreType.DMA(())]),
        out_shape=jax.ShapeDtypeStruct((t, d), F32),
        compiler_params=_cparams(("arbitrary",)),
        name="moe_combine",
    )(dest, ys, route, x, gate)


def hier_moe_layer(x, g, shift, scale, gate, wg, bg, we, be, w1, w3, w2, layer, seq):
    t, d = x.shape
    wr = jnp.zeros((d, LANES), F32)
    wr = wr.at[:, :N_GROUPS].set(wg)
    wr = wr.at[:, N_GROUPS:N_GROUPS + N_EXPERTS].set(we.transpose(1, 0, 2).reshape(d, N_EXPERTS))
    br = jnp.zeros((1, LANES), F32)
    br = br.at[0, :N_GROUPS].set(bg).at[0, N_GROUPS:N_GROUPS + N_EXPERTS].set(be.reshape(-1))
    h, route, cnt = moe_router(x, g, shift, scale, wr, br, seq)
    counts = cnt[0, :N_EXPERTS].astype(I32)
    padded = ((counts + MOE_ROWS - 1) // MOE_ROWS) * MOE_ROWS
    pad_end = jnp.cumsum(padded)
    pad_start = pad_end - padded
    eid = route[:, 0:2].astype(I32)
    rank = route[:, 4:6].astype(I32)
    experts = jnp.arange(N_EXPERTS, dtype=I32)
    start_of = jnp.sum(jnp.where(eid[:, :, None] == experts, pad_start, 0), axis=-1)
    dest = (start_of + rank).reshape(-1)
    p_rows = 2 * t + N_EXPERTS * MOE_ROWS
    nblk = p_rows // MOE_ROWS
    blk_row = jnp.arange(nblk, dtype=I32)[:, None] * MOE_ROWS
    blk_exp = jnp.minimum(jnp.sum((pad_end[None, :] <= blk_row).astype(I32), axis=1),
                          N_EXPERTS - 1)
    n_used = (pad_end[-1:] // MOE_ROWS).astype(I32)
    xs = moe_dispatch(dest, h, zero_rows(p_rows, d))
    ys = moe_experts(blk_exp, n_used, xs, w1, w3, w2, layer)
    return moe_combine(dest, ys, route, x, gate, seq)


def _even_weight(w_in):
    d = w_in.shape[0]
    sizes = [512, 512, 512, 512, 4, 4, 512, 512, 512, 256, 64, 4]
    offs = np.concatenate([[0], np.cumsum(sizes)])
    qa, ka, va, za, ba, aa, qb, kb, vb, qi, ki, wi = (w_in[:, offs[j]:offs[j + 1]]
                                                      for j in range(len(sizes)))
    pad = jnp.zeros((d, EV_N - EV_SM - 76), w_in.dtype)
    return jnp.concatenate([qa, ka, va, za, qb, kb, vb, qi, ki, ba, aa, wi, pad], axis=1)


def kernel(x, c, positions, ada_w, ada_b, norm_mix, norm_ffn, even_w_in, gdn_conv_w, gdn_a_log, gdn_dt_bias, gdn_out_norm, even_w_out, gla_w_in, gla_w_gate2, gla_b_gate2, gla_out_norm, gla_w_out, router_group_w, router_group_b, router_exp_w, router_exp_b, exp_w1, exp_w3, exp_w2, final_norm):
    batch, seq, d = x.shape
    depth = ada_w.shape[0]
    t = batch * seq
    xt = x.reshape(t, d)
    mod = ada_modulation(c, ada_w, ada_b)
    tables = rope_tables(positions)
    for layer in range(depth):
        sh_m, sc_m, gt_m, sh_f, sc_f, gt_f = (mod[layer, :, j * d:(j + 1) * d].reshape(batch, 1, d)
                                              for j in range(6))
        i = layer // 2
        g_mix = norm_mix[layer].reshape(1, d)
        if layer % 2 == 0:
            w_in = _even_weight(even_w_in[i]).astype(BF16)
            proj = in_projection(xt, g_mix, sh_m, sc_m, w_in, seq)
            oa = gdn_mixer(proj, gdn_conv_w[i], gdn_a_log[i], gdn_dt_bias[i], gdn_out_norm[i],
                           batch, seq)
            qb, kb, vb, qi, ki = dsa_prep(proj, tables)
            ob = dsa_attention(proj, qb, kb, vb, qi, ki, batch, seq)
            w_out = even_w_out[i].astype(BF16)
            half = GDN_HEADS * GDN_DK
            xt = out_projection([oa, ob], [w_out[:half], w_out[half:]], xt, gt_m, seq)
        else:
            w_in = jnp.pad(gla_w_in[i], ((0, 0), (0, OD_N - gla_w_in.shape[2]))).astype(BF16)
            proj = in_projection(xt, g_mix, sh_m, sc_m, w_in, seq)
            o = gla_mixer(proj, gla_w_gate2[i], gla_b_gate2[i], gla_out_norm[i], batch, seq)
            xt = out_projection([o], [gla_w_out[i].astype(BF16)], xt, gt_m, seq)
        xt = hier_moe_layer(xt, norm_ffn[layer].reshape(1, d), sh_f, sc_f, gt_f,
                            router_group_w[layer], router_group_b[layer],
                            router_exp_w[layer], router_exp_b[layer],
                            exp_w1, exp_w3, exp_w2, layer, seq)
    return final_rms_norm(xt, final_norm.reshape(1, d)).reshape(batch, seq, d)
```

```python
import functools

import jax
import jax.numpy as jnp
import numpy as np
from jax import lax
from jax.experimental import pallas as pl
from jax.experimental.pallas import tpu as pltpu

F32 = jnp.float32
BF16 = jnp.bfloat16
I32 = jnp.int32
HI = lax.Precision.HIGHEST

LANES = 128
SUBLANES = 8
VMEM_LIMIT = 56 * 1024 * 1024

EPS = 1e-6
ROPE_THETA = 10000.0
GDN_HEADS, GDN_DK = 4, 128
CONV_WIDTH = 4
GDN_CHUNK = 64
ATT_HEADS, ATT_DH = 4, 128
IDX_HEADS, IDX_DH = 4, 64
TOPK_MAX = 256
Q_BLOCK = 128
DSA_KEY_CHUNK = 512
GLA_HEADS, GLA_DK, GLA_DV = 4, 128, 256
GLA_RANK = 16
GLA_TAU = 16.0
GLA_CHUNK = 64
GLA_SUB = 16
N_GROUPS, EXPERTS_PER_GROUP = 4, 8
N_EXPERTS = N_GROUPS * EXPERTS_PER_GROUP
MOE_ROWS = 256

EV_QA, EV_KA, EV_VA, EV_ZA = 0, 512, 1024, 1536
EV_QB, EV_KB, EV_VB, EV_QI, EV_SM = 2048, 2560, 3072, 3584, 3840
EV_N = 3968
SM_KI, SM_BA, SM_AA, SM_WI = 0, 64, 68, 72
OD_Q, OD_K, OD_V, OD_R, OD_G = 0, 512, 1024, 2048, 3072
OD_N = 3200

NEG_INF = float("-inf")
INT_MIN = -(2 ** 31)
KEY_NEG_INF = INT_MIN + 0x7FFFFF


def _cparams(sem):
    return pltpu.CompilerParams(dimension_semantics=sem, vmem_limit_bytes=VMEM_LIMIT)


def _sigmoid(x):
    return 1.0 / (1.0 + jnp.exp(-x))


def _silu(x):
    return x * _sigmoid(x)


def _softplus(x):
    return jnp.maximum(x, 0.0) + jnp.log(1.0 + jnp.exp(-jnp.abs(x)))


def _dot(a, b, precision=None):
    return jnp.dot(a, b, precision=precision, preferred_element_type=F32)


def _dot_nt(a, b, precision=None):
    return lax.dot_general(a, b, (((1,), (1,)), ((), ())), precision=precision,
                           preferred_element_type=F32)


def _dot_tn(a, b, precision=None):
    return lax.dot_general(a, b, (((0,), (0,)), ((), ())), precision=precision,
                           preferred_element_type=F32)


def _bdot(a, b):
    return _dot(a.astype(BF16), b.astype(BF16))


def _bdot_nt(a, b):
    return _dot_nt(a.astype(BF16), b.astype(BF16))


def _bdot_tn(a, b):
    return _dot_tn(a.astype(BF16), b.astype(BF16))


def _split2(a):
    hi = a.astype(BF16)
    return hi, (a - hi.astype(F32)).astype(BF16)


def _dot3(a, b):
    ah, al = _split2(a)
    bh, bl = _split2(b)
    return _dot(ah, bh) + (_dot(ah, bl) + _dot(al, bh))


def _dot_mask(mask_b, x):
    x0 = x.astype(BF16)
    r1 = x - x0.astype(F32)
    x1 = r1.astype(BF16)
    x2 = (r1 - x1.astype(F32)).astype(BF16)
    return _dot(mask_b, x0) + (_dot(mask_b, x1) + _dot(mask_b, x2))


def _norm_mod(x, g, shift, scale):
    ms = jnp.mean(x * x, axis=-1, keepdims=True)
    y = x * lax.rsqrt(ms + EPS) * g
    return y * (1.0 + scale) + shift


def _ada_kernel(c_ref, w_ref, b_ref, o_ref):
    o_ref[0] = _dot(_silu(c_ref[...]), w_ref[0], HI) + b_ref[0]


def ada_modulation(c, ada_w, ada_b):
    depth, d, n = ada_w.shape
    b = c.shape[0]
    cp = jnp.zeros((SUBLANES, d), F32).at[:b].set(c)
    tn = 1536
    out = pl.pallas_call(
        _ada_kernel,
        grid=(depth, n // tn),
        in_specs=[pl.BlockSpec((SUBLANES, d), lambda l, j: (0, 0)),
                  pl.BlockSpec((1, d, tn), lambda l, j: (l, 0, j)),
                  pl.BlockSpec((1, 1, tn), lambda l, j: (l, 0, j))],
        out_specs=pl.BlockSpec((1, SUBLANES, tn), lambda l, j: (l, 0, j)),
        out_shape=jax.ShapeDtypeStruct((depth, SUBLANES, n), F32),
        compiler_params=_cparams(("arbitrary", "arbitrary")),
        name="ada_modulation",
    )(cp, ada_w, ada_b.reshape(depth, 1, n))
    return out[:, :b]


def _inproj_kernel(x_ref, g_ref, sh_ref, sc_ref, w_ref, o_ref):
    h = _norm_mod(x_ref[...], g_ref[...], sh_ref[0], sc_ref[0])
    o_ref[...] = _dot(h.astype(BF16), w_ref[...])


def in_projection(x, g, shift, scale, w, seq):
    t, d = x.shape
    n = w.shape[1]
    tm = 512
    per_b = seq // tm
    return pl.pallas_call(
        _inproj_kernel,
        grid=(t // tm,),
        in_specs=[pl.BlockSpec((tm, d), lambda i: (i, 0)),
                  pl.BlockSpec((1, d), lambda i: (0, 0)),
                  pl.BlockSpec((1, 1, d), lambda i: (i // per_b, 0, 0)),
                  pl.BlockSpec((1, 1, d), lambda i: (i // per_b, 0, 0)),
                  pl.BlockSpec((d, n), lambda i: (0, 0))],
        out_specs=pl.BlockSpec((tm, n), lambda i: (i, 0)),
        out_shape=jax.ShapeDtypeStruct((t, n), F32),
        compiler_params=_cparams(("arbitrary",)),
        name="in_projection",
    )(x, g, shift, scale, w)


def _outproj_kernel(*refs, n_in):
    o_refs = refs[:n_in]
    w_refs = refs[n_in:2 * n_in]
    x_ref, gt_ref, xo_ref = refs[2 * n_in:]
    acc = _dot(o_refs[0][...].astype(BF16), w_refs[0][...])
    for o_ref, w_ref in zip(o_refs[1:], w_refs[1:]):
        acc = acc + _dot(o_ref[...].astype(BF16), w_ref[...])
    xo_ref[...] = x_ref[...] + gt_ref[0] * acc


def out_projection(parts, weights, x, gate, seq):
    t, d = x.shape
    tm = 512
    per_b = seq // tm
    n_in = len(parts)
    in_specs = [pl.BlockSpec((tm, p.shape[1]), lambda i: (i, 0)) for p in parts]
    in_specs += [pl.BlockSpec(w.shape, lambda i: (0, 0)) for w in weights]
    in_specs += [pl.BlockSpec((tm, d), lambda i: (i, 0)),
                 pl.BlockSpec((1, 1, d), lambda i: (i // per_b, 0, 0))]
    return pl.pallas_call(
        functools.partial(_outproj_kernel, n_in=n_in),
        grid=(t // tm,),
        in_specs=in_specs,
        out_specs=pl.BlockSpec((tm, d), lambda i: (i, 0)),
        out_shape=jax.ShapeDtypeStruct((t, d), F32),
        compiler_params=_cparams(("arbitrary",)),
        name="out_projection",
    )(*parts, *weights, x, gate)


def _final_norm_kernel(x_ref, g_ref, o_ref):
    x = x_ref[...]
    ms = jnp.mean(x * x, axis=-1, keepdims=True)
    o_ref[...] = x * lax.rsqrt(ms + EPS) * g_ref[...]


def final_rms_norm(x, g):
    t, d = x.shape
    tm = 512
    return pl.pallas_call(
        _final_norm_kernel,
        grid=(t // tm,),
        in_specs=[pl.BlockSpec((tm, d), lambda i: (i, 0)),
                  pl.BlockSpec((1, d), lambda i: (0, 0))],
        out_specs=pl.BlockSpec((tm, d), lambda i: (i, 0)),
        out_shape=jax.ShapeDtypeStruct((t, d), F32),
        compiler_params=_cparams(("arbitrary",)),
        name="final_norm",
    )(x, g)


def _tri_masks(c):
    row = lax.broadcasted_iota(I32, (c, c), 0)
    col = lax.broadcasted_iota(I32, (c, c), 1)
    return row >= col, row > col, row == col


def _gdn_kernel(qkv_ref, za_ref, sm_ref, cw_ref, nega_ref, dtb_ref, on_ref, o_ref,
                tail_ref, state_ref, *, rb):
    c = GDN_CHUNK
    dk = GDN_DK

    @pl.when(pl.program_id(1) == 0)
    def _():
        tail_ref[...] = jnp.zeros_like(tail_ref)
        state_ref[...] = jnp.zeros_like(state_ref)

    x = qkv_ref[...]
    xx = jnp.concatenate([tail_ref[...], x], axis=0)
    tail_ref[...] = x[rb - SUBLANES:, :]
    cw = cw_ref[...]
    y = x * cw[CONV_WIDTH - 1:CONV_WIDTH, :]
    for i in range(1, CONV_WIDTH):
        y = y + pltpu.roll(xx, i, axis=0)[SUBLANES:, :] * cw[CONV_WIDTH - 1 - i:CONV_WIDTH - i, :]
    y = _silu(y)

    sm = sm_ref[...]
    beta_t = _sigmoid(sm)
    g_t = nega_ref[...] * _softplus(sm + dtb_ref[...])
    za = za_ref[...]
    hw = GDN_HEADS * dk
    nch = rb // c
    row = lax.broadcasted_iota(I32, (rb, rb), 0)
    col = lax.broadcasted_iota(I32, (rb, rb), 1)
    shift = int(np.log2(c))
    same = (row >> shift) == (col >> shift)
    incl = same & (row >= col)
    strict = same & (row > col)
    eye = (row == col).astype(F32)
    incl_b = jnp.where(incl, 1.0, 0.0).astype(BF16)
    gc_all = _dot_mask(incl_b, g_t)

    heads = range(GDN_HEADS)
    qs, ks, vs, bbs, gcs, decays = [], [], [], [], [], []
    for h in heads:
        lo = h * dk
        q = y[:, lo:lo + dk]
        k = y[:, hw + lo:hw + lo + dk]
        qs.append(q * lax.rsqrt(jnp.sum(q * q, axis=-1, keepdims=True) + EPS) * dk ** -0.5)
        ks.append(k * lax.rsqrt(jnp.sum(k * k, axis=-1, keepdims=True) + EPS))
        vs.append(y[:, 2 * hw + lo:2 * hw + lo + dk])
        bbs.append(jnp.broadcast_to(beta_t[:, SM_BA + h:SM_BA + h + 1], (rb, dk)))
        gc = jnp.broadcast_to(gc_all[:, SM_AA + h:SM_AA + h + 1], (rb, dk))
        gc_t = jnp.concatenate([gc[r:r + dk].T for r in range(0, rb, dk)], axis=1)[:1]
        gc_i = jnp.concatenate([gc] * (rb // dk), axis=1)
        decays.append(jnp.where(incl, jnp.exp(jnp.where(incl, gc_i - gc_t, 0.0)), 0.0))
        gcs.append(gc)
    kbetas = [ks[h] * bbs[h] for h in heads]
    ypows = [jnp.where(strict, -(_bdot_nt(kbetas[h], ks[h]) * decays[h]), 0.0) for h in heads]
    tmats = [eye + ypows[h] for h in heads]
    for _ in range(int(np.log2(c)) - 1):
        ypows = [_bdot(ypows[h], ypows[h]) for h in heads]
        tmats = [tmats[h] + _bdot(tmats[h], ypows[h]) for h in heads]
    egcs = [jnp.exp(gcs[h]) for h in heads]
    wus = [_bdot(tmats[h], jnp.concatenate([kbetas[h] * egcs[h], vs[h] * bbs[h]], axis=1))
           for h in heads]
    aqks = [_bdot_nt(qs[h], ks[h]) * decays[h] for h in heads]
    qgs = [qs[h] * egcs[h] for h in heads]
    sts = [state_ref[h] for h in heads]
    v_news = [[] for _ in heads]
    o_inter = [[] for _ in heads]
    for ci in range(nch):
        r0 = ci * c
        for h in heads:
            gl = gcs[h][r0 + c - 1:r0 + c, :]
            kd = ks[h][r0:r0 + c] * jnp.exp(gl - gcs[h][r0:r0 + c])
            ws = _bdot(jnp.concatenate([wus[h][r0:r0 + c, :dk], qgs[h][r0:r0 + c]], axis=0), sts[h])
            v_new = wus[h][r0:r0 + c, dk:] - ws[:c]
            o_inter[h].append(ws[c:])
            sts[h] = sts[h] * jnp.exp(gl) + _bdot_tn(kd, v_new)
            v_news[h].append(v_new)
    for h in heads:
        lo = h * dk
        state_ref[h] = sts[h]
        o = jnp.concatenate(o_inter[h], axis=0) + _bdot(aqks[h], jnp.concatenate(v_news[h], axis=0))
        o = o * lax.rsqrt(jnp.mean(o * o, axis=-1, keepdims=True) + EPS) * on_ref[...]
        o_ref[:, lo:lo + dk] = o * _silu(za[:, lo:lo + dk])


def gdn_mixer(proj, conv_w, a_log, dt_bias, out_norm, batch, seq):
    t = proj.shape[0]
    rb = 256
    nsb = seq // rb
    hw = GDN_HEADS * GDN_DK
    nega =jnp.zeros((1, LANES), F32).at[0, SM_AA:SM_AA + GDN_HEADS].set(-jnp.exp(a_log))
    dtb = jnp.zeros((1, LANES), F32).at[0, SM_AA:SM_AA + GDN_HEADS].set(dt_bias)
    row = lambda b, s: b * nsb + s
    return pl.pallas_call(
        functools.partial(_gdn_kernel, rb=rb),
        grid=(batch, nsb),
        in_specs=[pl.BlockSpec((rb, 3 * hw), lambda b, s: (row(b, s), 0)),
                  pl.BlockSpec((rb, hw), lambda b, s: (row(b, s), EV_ZA // hw)),
                  pl.BlockSpec((rb, LANES), lambda b, s: (row(b, s), EV_SM // LANES)),
                  pl.BlockSpec((CONV_WIDTH, 3 * hw), lambda b, s: (0, 0)),
                  pl.BlockSpec((1, LANES), lambda b, s: (0, 0)),
                  pl.BlockSpec((1, LANES), lambda b, s: (0, 0)),
                  pl.BlockSpec((1, GDN_DK), lambda b, s: (0, 0))],
        out_specs=pl.BlockSpec((rb, hw), lambda b, s: (row(b, s), 0)),
        out_shape=jax.ShapeDtypeStruct((t, hw), F32),
        scratch_shapes=[pltpu.VMEM((SUBLANES, 3 * hw), F32),
                        pltpu.VMEM((GDN_HEADS, GDN_DK, GDN_DK), F32)],
        compiler_params=_cparams(("arbitrary", "arbitrary")),
        name="gdn_mixer",
    )(proj, proj, proj, conv_w, nega, dtb, out_norm.reshape(1, GDN_DK))


def _gla_kernel(qk_ref, v_ref, r_ref, gl_ref, wg_ref, bg_ref, on_ref, o_ref, state_ref, *, rb):
    c = GLA_CHUNK
    dk, dv = GLA_DK, GLA_DV
    sb = GLA_SUB

    @pl.when(pl.program_id(1) == 0)
    def _():
        state_ref[...] = jnp.zeros_like(state_ref)

    z = _dot3(gl_ref[...], wg_ref[...]) + bg_ref[...]
    log_a = -_softplus(-z) * (1.0 / GLA_TAU)
    qk = qk_ref[...]
    vv = v_ref[...]
    rr = r_ref[...]
    incl, _, _ = _tri_masks(c)
    incl_b = incl.astype(F32).astype(BF16)
    hw = GLA_HEADS * dk
    sub_row = lax.broadcasted_iota(I32, (sb, dk), 0)
    att_lane = lax.broadcasted_iota(I32, (sb, c), 1)

    for ci in range(rb // c):
        r0 = ci * c
        b_all = _dot_mask(incl_b, log_a[r0:r0 + c])
        for h in range(GLA_HEADS):
            q = qk[r0:r0 + c, h * dk:(h + 1) * dk] * dk ** -0.5
            k = qk[r0:r0 + c, hw + h * dk:hw + (h + 1) * dk]
            v = vv[r0:r0 + c, h * dv:(h + 1) * dv]
            b = b_all[:, h * dk:(h + 1) * dk]
            st = state_ref[h]
            o = _bdot_nt(q * jnp.exp(b), st)
            att_rows = []
            for bi in range(c // sb):
                s0 = bi * sb
                qi = q[s0:s0 + sb]
                bi_ = b[s0:s0 + sb]
                ref = b[s0:s0 + 1]
                if bi > 0:
                    qt = qi * jnp.exp(bi_ - ref)
                    kt = k * jnp.exp(jnp.minimum(ref - b, 0.0))
                    att = jnp.where(att_lane < s0, _bdot_nt(qt, kt), 0.0)
                else:
                    att = jnp.zeros((sb, c), F32)
                for j in range(sb):
                    kj = k[s0 + j:s0 + j + 1]
                    bj = b[s0 + j:s0 + j + 1]
                    e = jnp.exp(jnp.where(sub_row >= j, bi_ - bj, NEG_INF))
                    col = jnp.sum(qi * kj * e, axis=-1, keepdims=True)
                    att = jnp.where(att_lane == s0 + j, col, att)
                att_rows.append(att)
            o = o + _bdot(jnp.concatenate(att_rows, axis=0), v)
            bl = b[c - 1:c]
            state_ref[h] = st * jnp.exp(bl) + _bdot_tn(v, k * jnp.exp(bl - b))
            o = o * lax.rsqrt(jnp.mean(o * o, axis=-1, keepdims=True) + EPS) * on_ref[...]
            o_ref[r0:r0 + c, h * dv:(h + 1) * dv] = o * _silu(rr[r0:r0 + c, h * dv:(h + 1) * dv])


def gla_mixer(proj, w_gate2, b_gate2, out_norm, batch, seq):
    t = proj.shape[0]
    rb = GLA_CHUNK
    nsb = seq // rb
    hw = GLA_HEADS * GLA_DK
    hv = GLA_HEADS * GLA_DV
    wg = jnp.zeros((LANES, hw), F32).at[:GLA_RANK].set(w_gate2)
    row = lambda b, s: b * nsb + s
    return pl.pallas_call(
        functools.partial(_gla_kernel, rb=rb),
        grid=(batch, nsb),
        in_specs=[pl.BlockSpec((rb, 2 * hw), lambda b, s: (row(b, s), 0)),
                  pl.BlockSpec((rb, hv), lambda b, s: (row(b, s), OD_V // hv)),
                  pl.BlockSpec((rb, hv), lambda b, s: (row(b, s), OD_R // hv)),
                  pl.BlockSpec((rb, LANES), lambda b, s: (row(b, s), OD_G // LANES)),
                  pl.BlockSpec((LANES, hw), lambda b, s: (0, 0)),
                  pl.BlockSpec((1, hw), lambda b, s: (0, 0)),
                  pl.BlockSpec((1, GLA_DV), lambda b, s: (0, 0))],
        out_specs=pl.BlockSpec((rb, hv), lambda b, s: (row(b, s), 0)),
        out_shape=jax.ShapeDtypeStruct((t, hv), F32),
        scratch_shapes=[pltpu.VMEM((GLA_HEADS, GLA_DV, GLA_DK), F32)],
        compiler_params=_cparams(("arbitrary", "arbitrary")),
        name="gla_mixer",
    )(proj, proj, proj, proj, wg, b_gate2.reshape(1, hw), out_norm.reshape(1, GLA_DV))


def _rope_table_kernel(pos_ref, inva_ref, invi_ref, ca_ref, sa_ref, ci_ref, si_ref):
    pos = pos_ref[...].astype(F32)
    lane = lax.broadcasted_iota(I32, (pos.shape[0], LANES), 1)
    ang_a = pos * inva_ref[...]
    ca_ref[...] = jnp.cos(ang_a)
    sa_ref[...] = jnp.where(lane < ATT_DH // 2, -1.0, 1.0) * jnp.sin(ang_a)
    ang_i = pos * invi_ref[...]
    ci_ref[...] = jnp.cos(ang_i)
    si_ref[...] = jnp.where(lane % IDX_DH < IDX_DH // 2, -1.0, 1.0) * jnp.sin(ang_i)


def rope_tables(positions):
    t = positions.size
    inv_a = 1.0 / (ROPE_THETA ** (jnp.arange(0, ATT_DH, 2, dtype=F32) / ATT_DH))
    inv_i = 1.0 / (ROPE_THETA ** (jnp.arange(0, IDX_DH, 2, dtype=F32) / IDX_DH))
    inv_a = jnp.tile(inv_a, 2).reshape(1, LANES)
    inv_i = jnp.tile(inv_i, 4).reshape(1, LANES)
    tm = 512
    spec = pl.BlockSpec((tm, LANES), lambda i: (i, 0))
    vec = pl.BlockSpec((1, LANES), lambda i: (0, 0))
    return pl.pallas_call(
        _rope_table_kernel,
        grid=(t // tm,),
        in_specs=[pl.BlockSpec((tm, 1), lambda i: (i, 0)), vec, vec],
        out_specs=[spec] * 4,
        out_shape=[jax.ShapeDtypeStruct((t, LANES), F32)] * 4,
        compiler_params=_cparams(("arbitrary",)),
        name="rope_tables",
    )(positions.reshape(t, 1), inv_a, inv_i)


def _dsa_prep_kernel(qb_ref, kb_ref, vb_ref, qi_ref, sm_ref, ca_ref, sa_ref, ci_ref, si_ref,
                     qo_ref, ko_ref, vt_ref, qio_ref, kio_ref, kn_ref):
    ca, sa, ci, si = ca_ref[...], sa_ref[...], ci_ref[...], si_ref[...]
    lane = lax.broadcasted_iota(I32, ca.shape, 1)
    first_half = lane % IDX_DH < IDX_DH // 2
    knorm = jnp.zeros((SUBLANES, LANES), F32)
    lane8 = lax.broadcasted_iota(I32, (SUBLANES, LANES), 1)

    def rope_att(tile):
        return tile * ca + pltpu.roll(tile, ATT_DH // 2, axis=1) * sa

    def rope_idx(tile):
        partner = jnp.where(first_half, pltpu.roll(tile, LANES - IDX_DH // 2, axis=1),
                            pltpu.roll(tile, IDX_DH // 2, axis=1))
        return tile * ci + partner * si

    qb, kb = qb_ref[...], kb_ref[...]
    for h in range(ATT_HEADS):
        sl = slice(h * ATT_DH, (h + 1) * ATT_DH)
        qo_ref[:, sl] = (rope_att(qb[:, sl]) * ATT_DH ** -0.5).astype(BF16)
        kr = rope_att(kb[:, sl]).astype(BF16)
        ko_ref[:, sl] = kr
        krf = kr.astype(F32)
        knorm = jnp.where(lane8 == h, jnp.max(jnp.sum(krf * krf, axis=1, keepdims=True)), knorm)
    kn_ref[0] = knorm
    vt_ref[0] = vb_ref[...].T.astype(BF16)
    qi = qi_ref[...]
    for p in range(IDX_HEADS * IDX_DH // LANES):
        sl = slice(p * LANES, (p + 1) * LANES)
        qio_ref[:, sl] = rope_idx(qi[:, sl]).astype(BF16)
    ki = rope_idx(sm_ref[...])
    kio_ref[...] = jnp.where(lane < IDX_DH, ki, pltpu.roll(ki, IDX_DH, axis=1)).astype(BF16)


def dsa_prep(proj, tables, seq):
    t = proj.shape[0]
    tm = min(DSA_KEY_CHUNK, seq)
    aw = ATT_HEADS * ATT_DH
    iw = IDX_HEADS * IDX_DH
    tab = pl.BlockSpec((tm, LANES), lambda i: (i, 0))
    return pl.pallas_call(
        _dsa_prep_kernel,
        grid=(t // tm,),
        in_specs=[pl.BlockSpec((tm, aw), lambda i: (i, EV_QB // aw)),
                  pl.BlockSpec((tm, aw), lambda i: (i, EV_KB // aw)),
                  pl.BlockSpec((tm, aw), lambda i: (i, EV_VB // aw)),
                  pl.BlockSpec((tm, iw), lambda i: (i, EV_QI // iw)),
                  pl.BlockSpec((tm, LANES), lambda i: (i, EV_SM // LANES)),
                  tab, tab, tab, tab],
        out_specs=[pl.BlockSpec((tm, aw), lambda i: (i, 0)),
                   pl.BlockSpec((tm, aw), lambda i: (i, 0)),
                   pl.BlockSpec((1, aw, tm), lambda i: (i, 0, 0)),
                   pl.BlockSpec((tm, iw), lambda i: (i, 0)),
                   pl.BlockSpec((tm, LANES), lambda i: (i, 0)),
                   pl.BlockSpec((1, SUBLANES, LANES), lambda i: (i, 0, 0))],
        out_shape=[jax.ShapeDtypeStruct((t, aw), BF16),
                   jax.ShapeDtypeStruct((t, aw), BF16),
                   jax.ShapeDtypeStruct((t // tm, aw, tm), BF16),
                   jax.ShapeDtypeStruct((t, iw), BF16),
                   jax.ShapeDtypeStruct((t, LANES), BF16),
                   jax.ShapeDtypeStruct((t // tm, SUBLANES, LANES), F32)],
        compiler_params=_cparams(("arbitrary",)),
        name="dsa_prep",
    )(proj, proj, proj, proj, proj, *tables)


def _fold_rows(x, op):
    parts = [x[r:r + SUBLANES] for r in range(0, x.shape[0], SUBLANES)]
    while len(parts) > 1:
        parts = [op(parts[j], parts[j + 1]) for j in range(0, len(parts), 2)]
    return parts[0]


def _dsa_kernel(qb_ref, qi_ref, sm_ref, kn_ref, k_ref, vt_ref, ki_ref, o_ref,
                key_ref, l_ref, acc_ref, *, kc, topk):
    nq = Q_BLOCK
    i = pl.program_id(1)
    nkc = (i * nq + nq + kc - 1) // kc
    q_pos = i * nq + lax.broadcasted_iota(I32, (1, nq), 1)
    key0 = lax.broadcasted_iota(I32, (kc, nq), 0)
    lane = lax.broadcasted_iota(I32, (nq, LANES), 1)
    wi_t = sm_ref[...].T * (IDX_HEADS * IDX_DH) ** -0.5
    qi = qi_ref[...]
    zero_b = jnp.zeros((nq, LANES), BF16)
    qi_heads = []
    for h in range(IDX_HEADS):
        pair = qi[:, (h // 2) * LANES:(h // 2 + 1) * LANES]
        own = (lane < IDX_DH) if h % 2 == 0 else (lane >= IDX_DH)
        qi_heads.append(jnp.where(own, pair, zero_b))

    def score_body(c, carry):
        start = pl.multiple_of(c * kc, kc)
        kic = ki_ref[0, pl.ds(start, kc), :]
        dots = [_dot_nt(kic, qi_heads[h]) for h in range(IDX_HEADS)]
        s = jnp.zeros((kc, nq), F32)
        for h in range(IDX_HEADS):
            s = s + wi_t[SM_WI + h:SM_WI + h + 1, :] * jnp.maximum(dots[h], 0.0)
        s = s + 0.0
        s = jnp.where(key0 <= q_pos - c * kc, s, NEG_INF)
        bits = pltpu.bitcast(s, I32)
        key_ref[c] = bits ^ ((bits >> 31) & 0x7FFFFFFF)
        return carry

    lax.fori_loop(0, nkc, score_body, 0)

    def count(pred):
        def body(c, acc):
            return acc + _fold_rows(jnp.where(pred(key_ref[c], c), 1, 0).astype(I32), jnp.add)
        acc = lax.fori_loop(0, nkc, body, jnp.zeros((SUBLANES, nq), I32))
        return jnp.sum(acc, axis=0, keepdims=True)

    thr = jnp.where(count(lambda key, c: key >= 0) >= topk, 0, INT_MIN).astype(I32)

    def bit_body(it, thr):
        cand = thr + (jnp.int32(1) << (30 - it))
        return jnp.where(count(lambda key, c: key >= cand) >= topk, cand, thr)

    thr = lax.fori_loop(0, 31, bit_body, thr)

    n_gt = count(lambda key, c: key > thr)
    n_eq = count(lambda key, c: key == thr)
    need = topk - n_gt
    big = jnp.int32(2 ** 30)

    def tie_search():
        def idx_body(it, x):
            cand = x + (jnp.int32(1) << (14 - it))
            below = count(lambda key, c: (key0 < jnp.where(key == thr, cand - c * kc, 0)))
            return jnp.where(below < need, cand, x)
        return lax.fori_loop(0, 15, idx_body, jnp.zeros((1, nq), I32))

    any_tie = jnp.max(jnp.where(n_eq > need, 1, 0)) > 0
    last = lax.cond(any_tie, tie_search, lambda: jnp.full((1, nq), big, I32))

    last = jnp.where(thr == KEY_NEG_INF, -1, last)
    zero_f = jnp.float32(0.0)
    ninf_f = jnp.float32(NEG_INF)
    qb = qb_ref[...]

    def select_bias(c):
        key = key_ref[c]
        tie_ok = jnp.where(key0 <= last - c * kc, zero_f, ninf_f)
        return jnp.where(key > thr, zero_f, jnp.where(key == thr, tie_ok, ninf_f))

    def chunk_logits(c, h, bias):
        start = pl.multiple_of(c * kc, kc)
        kch = k_ref[0, pl.ds(start, kc), h * ATT_DH:(h + 1) * ATT_DH]
        return _dot_nt(kch, qb[:, h * ATT_DH:(h + 1) * ATT_DH]) + bias

    def attend(shifts):
        l_ref[...] = jnp.zeros(l_ref.shape, F32)
        acc_ref[...] = jnp.zeros(acc_ref.shape, F32)

        def body(c, carry):
            bias = select_bias(c)
            heads = range(ATT_HEADS)
            logits = [chunk_logits(c, h, bias) for h in heads]
            ps = [jnp.exp(logits[h] - shifts[h]) for h in heads]
            for h in heads:
                l_ref[h] = l_ref[h] + _fold_rows(ps[h], jnp.add)
                vth = vt_ref[0, c, h * ATT_DH:(h + 1) * ATT_DH, :]
                acc_ref[h] = acc_ref[h] + _dot(vth, ps[h].astype(BF16))
            return carry

        lax.fori_loop(0, nkc, body, 0)
        return [jnp.sum(l_ref[h], axis=0, keepdims=True) for h in range(ATT_HEADS)]

    ones_b = jnp.ones((SUBLANES, ATT_DH), BF16)
    bounds = []
    for h in range(ATT_HEADS):
        qh = qb[:, h * ATT_DH:(h + 1) * ATT_DH].astype(F32)
        qsq = _dot_nt(ones_b, (qh * qh).astype(BF16))[0:1, :]
        bounds.append(jnp.sqrt(qsq * kn_ref[0][0:1, h:h + 1]) * 1.02)
    sums = attend(bounds)

    def write_out(sums):
        for h in range(ATT_HEADS):
            o_ref[:, h * ATT_DH:(h + 1) * ATT_DH] = (acc_ref[h] / sums[h]).T

    write_out(sums)
    lowest = jnp.minimum(jnp.minimum(sums[0], sums[1]), jnp.minimum(sums[2], sums[3]))

    @pl.when(jnp.min(lowest) <= 0.0)
    def _():
        l_ref[...] = jnp.full(l_ref.shape, NEG_INF, F32)

        def max_body(c, carry):
            bias = select_bias(c)
            for h in range(ATT_HEADS):
                l_ref[h] = jnp.maximum(l_ref[h], _fold_rows(chunk_logits(c, h, bias), jnp.maximum))
            return carry

        lax.fori_loop(0, nkc, max_body, 0)
        exact = [jnp.max(l_ref[h], axis=0, keepdims=True) for h in range(ATT_HEADS)]
        write_out(attend(exact))


def dsa_attention(proj, qb, kb, vt, qi, ki, knorm, batch, seq):
    t = proj.shape[0]
    aw = ATT_HEADS * ATT_DH
    iw = IDX_HEADS * IDX_DH
    kc = min(DSA_KEY_CHUNK, seq)
    nkc = seq // kc
    topk = min(TOPK_MAX, seq // 4)
    nq = seq // Q_BLOCK
    row = lambda b, i: b * nq + i
    kb3, ki3 = (a.reshape(batch, seq, a.shape[1]) for a in (kb, ki))
    vt4 = vt.reshape(batch, nkc, aw, kc)
    kmax = jnp.max(knorm.reshape(batch, nkc, SUBLANES, LANES), axis=1)
    resident = lambda shape: pl.BlockSpec((1,) + shape, lambda b, i: (b,) + (0,) * len(shape),
                                          pipeline_mode=pl.Buffered(1))
    return pl.pallas_call(
        functools.partial(_dsa_kernel, kc=kc, topk=topk),
        grid=(batch, nq),
        in_specs=[pl.BlockSpec((Q_BLOCK, aw), lambda b, i: (row(b, i), 0)),
                  pl.BlockSpec((Q_BLOCK, iw), lambda b, i: (row(b, i), 0)),
                  pl.BlockSpec((Q_BLOCK, LANES), lambda b, i: (row(b, i), EV_SM // LANES)),
                  pl.BlockSpec((1, SUBLANES, LANES), lambda b, i: (b, 0, 0)),
                  resident((seq, aw)), resident((nkc, aw, kc)), resident((seq, LANES))],
        out_specs=pl.BlockSpec((Q_BLOCK, aw), lambda b, i: (row(b, i), 0)),
        out_shape=jax.ShapeDtypeStruct((t, aw), F32),
        scratch_shapes=[pltpu.VMEM((nkc, kc, Q_BLOCK), I32),
                        pltpu.VMEM((ATT_HEADS, SUBLANES, Q_BLOCK), F32),
                        pltpu.VMEM((ATT_HEADS, ATT_DH, Q_BLOCK), F32)],
        compiler_params=_cparams(("arbitrary", "arbitrary")),
        name="dsa_attention",
    )(qb, qi, proj, kmax, kb3, vt4, ki3)


def _router_kernel(x_ref, g_ref, sh_ref, sc_ref, wr_ref, br_ref, h_ref, route_ref, cnt_ref,
                   carry_ref):
    @pl.when(pl.program_id(0) == 0)
    def _():
        carry_ref[...] = jnp.zeros_like(carry_ref)

    h = _norm_mod(x_ref[...], g_ref[...], sh_ref[0], sc_ref[0])
    h_ref[...] = h
    tm = h.shape[0]
    logits = _dot(h, wr_ref[...], HI) + br_ref[...]
    lane = lax.broadcasted_iota(I32, (tm, LANES), 1)
    far = jnp.int32(LANES)

    def first_max(vals):
        m = jnp.max(vals, axis=1, keepdims=True)
        return m, jnp.min(jnp.where(vals == m, lane, far), axis=1, keepdims=True)

    is_group = lane < N_GROUPS
    gmax, gsel = first_max(jnp.where(is_group, logits, NEG_INF))
    gsum = jnp.sum(jnp.where(is_group, jnp.exp(logits - gmax), 0.0), axis=1, keepdims=True)
    lo = N_GROUPS + EXPERTS_PER_GROUP * gsel
    el = jnp.where((lane >= lo) & (lane < lo + EXPERTS_PER_GROUP), logits, NEG_INF)
    m1, i1 = first_max(el)
    m2, i2 = first_max(jnp.where(lane == i1, NEG_INF, el))
    e2 = jnp.exp(m2 - m1)
    gate1 = 1.0 / (gsum * (1.0 + e2))
    gate2 = e2 * gate1
    eid1 = i1 - N_GROUPS
    eid2 = i2 - N_GROUPS
    oh1 = lane == eid1
    oh2 = lane == eid2
    onehot = jnp.where(oh1 | oh2, 1.0, 0.0)
    r = lax.broadcasted_iota(I32, (tm, tm), 0)
    cidx = lax.broadcasted_iota(I32, (tm, tm), 1)
    below = jnp.where(r > cidx, 1.0, 0.0).astype(BF16)
    cum = _dot(below, onehot.astype(BF16)) + carry_ref[0:1, :]
    rank1 = jnp.sum(jnp.where(oh1, cum, 0.0), axis=1, keepdims=True)
    rank2 = jnp.sum(jnp.where(oh2, cum, 0.0), axis=1, keepdims=True)
    carry_ref[...] = carry_ref[...] + jnp.sum(onehot, axis=0, keepdims=True)
    cnt_ref[...] = carry_ref[...]
    out = jnp.zeros((tm, LANES), F32)
    for j, val in enumerate((eid1.astype(F32), eid2.astype(F32), gate1, gate2, rank1, rank2)):
        out = jnp.where(lane == j, val, out)
    route_ref[...] = out


def moe_router(x, g, shift, scale, wr, br, seq):
    t, d = x.shape
    tm = 512
    per_b = seq // tm
    return pl.pallas_call(
        _router_kernel,
        grid=(t // tm,),
        in_specs=[pl.BlockSpec((tm, d), lambda i: (i, 0)),
                  pl.BlockSpec((1, d), lambda i: (0, 0)),
                  pl.BlockSpec((1, 1, d), lambda i: (i // per_b, 0, 0)),
                  pl.BlockSpec((1, 1, d), lambda i: (i // per_b, 0, 0)),
                  pl.BlockSpec((d, LANES), lambda i: (0, 0)),
                  pl.BlockSpec((1, LANES), lambda i: (0, 0))],
        out_specs=[pl.BlockSpec((tm, d), lambda i: (i, 0)),
                   pl.BlockSpec((tm, LANES), lambda i: (i, 0)),
                   pl.BlockSpec((SUBLANES, LANES), lambda i: (0, 0))],
        out_shape=[jax.ShapeDtypeStruct((t, d), F32),
                   jax.ShapeDtypeStruct((t, LANES), F32),
                   jax.ShapeDtypeStruct((SUBLANES, LANES), F32)],
        scratch_shapes=[pltpu.VMEM((SUBLANES, LANES), F32)],
        compiler_params=_cparams(("arbitrary",)),
        name="moe_router",
    )(x, g, shift, scale, wr, br)


def _zero_kernel(o_ref):
    o_ref[...] = jnp.zeros_like(o_ref)


def zero_rows(rows, d):
    tm = 1024
    return pl.pallas_call(
        _zero_kernel,
        grid=(rows // tm,),
        out_specs=pl.BlockSpec((tm, d), lambda i: (i, 0)),
        out_shape=jax.ShapeDtypeStruct((rows, d), F32),
        compiler_params=_cparams(("arbitrary",)),
        name="zero_rows",
    )()


def _dispatch_kernel(dest_ref, h_ref, xs_in_ref, xs_ref, sem, *, tm):
    del xs_in_ref
    base = pl.program_id(0) * tm * 2

    def row_copy(r, j):
        d = dest_ref[base + 2 * r + j]
        return pltpu.make_async_copy(h_ref.at[pl.ds(r, 1)], xs_ref.at[pl.ds(d, 1)], sem)

    def start(r, carry):
        row_copy(r, 0).start()
        row_copy(r, 1).start()
        return carry

    def wait(r, carry):
        row_copy(r, 0).wait()
        row_copy(r, 1).wait()
        return carry

    lax.fori_loop(0, tm, start, 0)
    lax.fori_loop(0, tm, wait, 0)


def moe_dispatch(dest, h, xs_zero):
    t, d = h.shape
    tm = 256
    return pl.pallas_call(
        functools.partial(_dispatch_kernel, tm=tm),
        grid_spec=pltpu.PrefetchScalarGridSpec(
            num_scalar_prefetch=1,
            grid=(t // tm,),
            in_specs=[pl.BlockSpec((tm, d), lambda i, dest: (i, 0)),
                      pl.BlockSpec(memory_space=pl.ANY)],
            out_specs=pl.BlockSpec(memory_space=pl.ANY),
            scratch_shapes=[pltpu.SemaphoreType.DMA(())]),
        out_shape=jax.ShapeDtypeStruct(xs_zero.shape, F32),
        input_output_aliases={2: 0},
        compiler_params=_cparams(("arbitrary",)),
        name="moe_dispatch",
    )(dest, h, xs_zero)


def _expert_kernel(be_ref, nb_ref, xs_ref, w1_ref, w3_ref, w2_ref, ys_ref, w1b, w3b, w2b):
    i = pl.program_id(0)
    prev = be_ref[jnp.maximum(i - 1, 0)]

    @pl.when((i == 0) | (be_ref[i] != prev))
    def _():
        w1b[...] = w1_ref[0, 0].astype(BF16)
        w3b[...] = w3_ref[0, 0].astype(BF16)
        w2b[...] = w2_ref[0, 0].astype(BF16)

    @pl.when(i < nb_ref[0])
    def _():
        x = xs_ref[...].astype(BF16)
        a = _dot(x, w1b[...])
        hid = _silu(a) * _dot(x, w3b[...])
        ys_ref[...] = _dot(hid.astype(BF16), w2b[...])

    @pl.when(i >= nb_ref[0])
    def _():
        ys_ref[...] = jnp.zeros_like(ys_ref)


def moe_experts(blk_exp, n_used, xs, w1, w3, w2, layer):
    p, d = xs.shape
    de = w1.shape[3]
    rows = MOE_ROWS
    return pl.pallas_call(
        _expert_kernel,
        grid_spec=pltpu.PrefetchScalarGridSpec(
            num_scalar_prefetch=2,
            grid=(p // rows,),
            in_specs=[pl.BlockSpec((rows, d), lambda i, be, nb: (i, 0)),
                      pl.BlockSpec((1, 1, d, de), lambda i, be, nb: (layer, be[i], 0, 0)),
                      pl.BlockSpec((1, 1, d, de), lambda i, be, nb: (layer, be[i], 0, 0)),
                      pl.BlockSpec((1, 1, de, d), lambda i, be, nb: (layer, be[i], 0, 0))],
            out_specs=pl.BlockSpec((rows, d), lambda i, be, nb: (i, 0)),
            scratch_shapes=[pltpu.VMEM((d, de), BF16), pltpu.VMEM((d, de), BF16),
                            pltpu.VMEM((de, d), BF16)]),
        out_shape=jax.ShapeDtypeStruct((p, d), F32),
        compiler_params=_cparams(("arbitrary",)),
        name="moe_experts",
    )(blk_exp, n_used, xs, w1, w3, w2)


def _combine_kernel(dest_ref, ys_ref, route_ref, x_ref, gt_ref, o_ref, buf_ref, sem, *, tm):
    base = pl.program_id(0) * tm * 2

    def row_copy(r, j):
        d = dest_ref[base + 2 * r + j]
        return pltpu.make_async_copy(ys_ref.at[pl.ds(d, 1)], buf_ref.at[j, pl.ds(r, 1)], sem)

    def start(r, carry):
        row_copy(r, 0).start()
        row_copy(r, 1).start()
        return carry

    def wait(r, carry):
        row_copy(r, 0).wait()
        row_copy(r, 1).wait()
        return carry

    lax.fori_loop(0, tm, start, 0)
    lax.fori_loop(0, tm, wait, 0)
    route = route_ref[...]
    y = route[:, 2:3] * buf_ref[0] + route[:, 3:4] * buf_ref[1]
    o_ref[...] = x_ref[...] + gt_ref[0] * y


def moe_combine(dest, ys, route, x, gate, seq):
    t, d = x.shape
    tm = 256
    per_b = seq // tm
    return pl.pallas_call(
        functools.partial(_combine_kernel, tm=tm),
        grid_spec=pltpu.PrefetchScalarGridSpec(
            num_scalar_prefetch=1,
            grid=(t // tm,),
            in_specs=[pl.BlockSpec(memory_space=pl.ANY),
                      pl.BlockSpec((tm, LANES), lambda i, dest: (i, 0)),
                      pl.BlockSpec((tm, d), lambda i, dest: (i, 0)),
                      pl.BlockSpec((1, 1, d), lambda i, dest: (i // per_b, 0, 0))],
            out_specs=pl.BlockSpec((tm, d), lambda i, dest: (i, 0)),
            scratch_shapes=[pltpu.VMEM((2, tm, d), F32), pltpu.SemaphoreType.DMA(())]),
        out_shape=jax.ShapeDtypeStruct((t, d), F32),
        compiler_params=_cparams(("arbitrary",)),
        name="moe_combine",
    )(dest, ys, route, x, gate)


def hier_moe_layer(x, g, shift, scale, gate, wg, bg, we, be, w1, w3, w2, layer, seq):
    t, d = x.shape
    wr = jnp.zeros((d, LANES), F32)
    wr = wr.at[:, :N_GROUPS].set(wg)
    wr = wr.at[:, N_GROUPS:N_GROUPS + N_EXPERTS].set(we.transpose(1, 0, 2).reshape(d, N_EXPERTS))
    br = jnp.zeros((1, LANES), F32)
    br = br.at[0, :N_GROUPS].set(bg).at[0, N_GROUPS:N_GROUPS + N_EXPERTS].set(be.reshape(-1))
    h, route, cnt = moe_router(x, g, shift, scale, wr, br, seq)
    counts = cnt[0, :N_EXPERTS].astype(I32)
    padded = ((counts + MOE_ROWS - 1) // MOE_ROWS) * MOE_ROWS
    pad_end = jnp.cumsum(padded)
    pad_start = pad_end - padded
    eid = route[:, 0:2].astype(I32)
    rank = route[:, 4:6].astype(I32)
    experts = jnp.arange(N_EXPERTS, dtype=I32)
    start_of = jnp.sum(jnp.where(eid[:, :, None] == experts, pad_start, 0), axis=-1)
    dest = (start_of + rank).reshape(-1)
    p_rows = 2 * t + N_EXPERTS * MOE_ROWS
    nblk = p_rows // MOE_ROWS
    blk_row = jnp.arange(nblk, dtype=I32)[:, None] * MOE_ROWS
    blk_exp = jnp.minimum(jnp.sum((pad_end[None, :] <= blk_row).astype(I32), axis=1),
                          N_EXPERTS - 1)
    n_used = (pad_end[-1:] // MOE_ROWS).astype(I32)
    xs = moe_dispatch(dest, h, zero_rows(p_rows, d))
    ys = moe_experts(blk_exp, n_used, xs, w1, w3, w2, layer)
    return moe_combine(dest, ys, route, x, gate, seq)


def _even_weight(w_in):
    d = w_in.shape[0]
    sizes = [512, 512, 512, 512, 4, 4, 512, 512, 512, 256, 64, 4]
    offs = np.concatenate([[0], np.cumsum(sizes)])
    qa, ka, va, za, ba, aa, qb, kb, vb, qi, ki, wi = (w_in[:, offs[j]:offs[j + 1]]
                                                      for j in range(len(sizes)))
    pad = jnp.zeros((d, EV_N - EV_SM - 76), w_in.dtype)
    return jnp.concatenate([qa, ka, va, za, qb, kb, vb, qi, ki, ba, aa, wi, pad], axis=1)


def kernel(x, c, positions, ada_w, ada_b, norm_mix, norm_ffn, even_w_in, gdn_conv_w, gdn_a_log, gdn_dt_bias, gdn_out_norm, even_w_out, gla_w_in, gla_w_gate2, gla_b_gate2, gla_out_norm, gla_w_out, router_group_w, router_group_b, router_exp_w, router_exp_b, exp_w1, exp_w3, exp_w2, final_norm):
    batch, seq, d = x.shape
    depth = ada_w.shape[0]
    t = batch * seq
    xt = x.reshape(t, d)
    mod = ada_modulation(c, ada_w, ada_b)
    tables = rope_tables(positions)
    for layer in range(depth):
        sh_m, sc_m, gt_m, sh_f, sc_f, gt_f = (mod[layer, :, j * d:(j + 1) * d].reshape(batch, 1, d)
                                              for j in range(6))
        i = layer // 2
        g_mix = norm_mix[layer].reshape(1, d)
        if layer % 2 == 0:
            w_in = _even_weight(even_w_in[i]).astype(BF16)
            proj = in_projection(xt, g_mix, sh_m, sc_m, w_in, seq)
            oa = gdn_mixer(proj, gdn_conv_w[i], gdn_a_log[i], gdn_dt_bias[i], gdn_out_norm[i],
                           batch, seq)
            qb, kb, vt, qi, ki, knorm = dsa_prep(proj, tables, seq)
            ob = dsa_attention(proj, qb, kb, vt, qi, ki, knorm, batch, seq)
            w_out = even_w_out[i].astype(BF16)
            half = GDN_HEADS * GDN_DK
            xt = out_projection([oa, ob], [w_out[:half], w_out[half:]], xt, gt_m, seq)
        else:
            w_in = jnp.pad(gla_w_in[i], ((0, 0), (0, OD_N - gla_w_in.shape[2]))).astype(BF16)
            proj = in_projection(xt, g_mix, sh_m, sc_m, w_in, seq)
            o = gla_mixer(proj, gla_w_gate2[i], gla_b_gate2[i], gla_out_norm[i], batch, seq)
            xt = out_projection([o], [gla_w_out[i].astype(BF16)], xt, gt_m, seq)
        xt = hier_moe_layer(xt, norm_ffn[layer].reshape(1, d), sh_f, sc_f, gt_f,
                            router_group_w[layer], router_group_b[layer],
                            router_exp_w[layer], router_exp_b[layer],
                            exp_w1, exp_w3, exp_w2, layer, seq)
    return final_rms_norm(xt, final_norm.reshape(1, d)).reshape(batch, seq, d)
```

```python
import functools

import jax
import jax.numpy as jnp
import numpy as np
from jax import lax
from jax.experimental import pallas as pl
from jax.experimental.pallas import tpu as pltpu

F32 = jnp.float32
BF16 = jnp.bfloat16
I32 = jnp.int32
HI = lax.Precision.HIGHEST

LANES = 128
SUBLANES = 8
VMEM_LIMIT = 56 * 1024 * 1024

EPS = 1e-6
ROPE_THETA = 10000.0
GDN_HEADS, GDN_DK = 4, 128
CONV_WIDTH = 4
GDN_CHUNK = 64
ATT_HEADS, ATT_DH = 4, 128
IDX_HEADS, IDX_DH = 4, 64
TOPK_MAX = 256
Q_BLOCK = 128
DSA_KEY_CHUNK = 512
GLA_HEADS, GLA_DK, GLA_DV = 4, 128, 256
GLA_RANK = 16
GLA_TAU = 16.0
GLA_CHUNK = 64
GLA_SUB = 16
N_GROUPS, EXPERTS_PER_GROUP = 4, 8
N_EXPERTS = N_GROUPS * EXPERTS_PER_GROUP
MOE_ROWS = 256

EV_QA, EV_KA, EV_VA, EV_ZA = 0, 512, 1024, 1536
EV_QB, EV_KB, EV_VB, EV_QI, EV_SM = 2048, 2560, 3072, 3584, 3840
EV_N = 3968
SM_KI, SM_BA, SM_AA, SM_WI = 0, 64, 68, 72
OD_Q, OD_K, OD_V, OD_R, OD_G = 0, 512, 1024, 2048, 3072
OD_N = 3200

NEG_INF = float("-inf")
INT_MIN = -(2 ** 31)
KEY_NEG_INF = INT_MIN + 0x7FFFFF


def _cparams(sem):
    return pltpu.CompilerParams(dimension_semantics=sem, vmem_limit_bytes=VMEM_LIMIT)


def _sigmoid(x):
    return 1.0 / (1.0 + jnp.exp(-x))


def _silu(x):
    return x * _sigmoid(x)


def _softplus(x):
    return jnp.maximum(x, 0.0) + jnp.log(1.0 + jnp.exp(-jnp.abs(x)))


def _dot(a, b, precision=None):
    return jnp.dot(a, b, precision=precision, preferred_element_type=F32)


def _dot_nt(a, b, precision=None):
    return lax.dot_general(a, b, (((1,), (1,)), ((), ())), precision=precision,
                           preferred_element_type=F32)


def _dot_tn(a, b, precision=None):
    return lax.dot_general(a, b, (((0,), (0,)), ((), ())), precision=precision,
                           preferred_element_type=F32)


def _bdot(a, b):
    return _dot(a.astype(BF16), b.astype(BF16))


def _bdot_nt(a, b):
    return _dot_nt(a.astype(BF16), b.astype(BF16))


def _bdot_tn(a, b):
    return _dot_tn(a.astype(BF16), b.astype(BF16))


def _split2(a):
    hi = a.astype(BF16)
    return hi, (a - hi.astype(F32)).astype(BF16)


def _dot3(a, b):
    ah, al = _split2(a)
    bh, bl = _split2(b)
    return _dot(ah, bh) + (_dot(ah, bl) + _dot(al, bh))


def _dot_mask(mask_b, x):
    x0 = x.astype(BF16)
    r1 = x - x0.astype(F32)
    x1 = r1.astype(BF16)
    x2 = (r1 - x1.astype(F32)).astype(BF16)
    return _dot(mask_b, x0) + (_dot(mask_b, x1) + _dot(mask_b, x2))


def _norm_mod(x, g, shift, scale):
    ms = jnp.mean(x * x, axis=-1, keepdims=True)
    y = x * lax.rsqrt(ms + EPS) * g
    return y * (1.0 + scale) + shift


def _ada_kernel(c_ref, w_ref, b_ref, o_ref):
    o_ref[0] = _dot(_silu(c_ref[...]), w_ref[0], HI) + b_ref[0]


def ada_modulation(c, ada_w, ada_b):
    depth, d, n = ada_w.shape
    b = c.shape[0]
    cp = jnp.zeros((SUBLANES, d), F32).at[:b].set(c)
    tn = 1536
    out = pl.pallas_call(
        _ada_kernel,
        grid=(depth, n // tn),
        in_specs=[pl.BlockSpec((SUBLANES, d), lambda l, j: (0, 0)),
                  pl.BlockSpec((1, d, tn), lambda l, j: (l, 0, j)),
                  pl.BlockSpec((1, 1, tn), lambda l, j: (l, 0, j))],
        out_specs=pl.BlockSpec((1, SUBLANES, tn), lambda l, j: (l, 0, j)),
        out_shape=jax.ShapeDtypeStruct((depth, SUBLANES, n), F32),
        compiler_params=_cparams(("arbitrary", "arbitrary")),
        name="ada_modulation",
    )(cp, ada_w, ada_b.reshape(depth, 1, n))
    return out[:, :b]


def _inproj_kernel(x_ref, g_ref, sh_ref, sc_ref, w_ref, o_ref):
    h = _norm_mod(x_ref[...], g_ref[...], sh_ref[0], sc_ref[0])
    o_ref[...] = _dot(h.astype(BF16), w_ref[...])


def in_projection(x, g, shift, scale, w, seq):
    t, d = x.shape
    n = w.shape[1]
    tm = 512
    per_b = seq // tm
    return pl.pallas_call(
        _inproj_kernel,
        grid=(t // tm,),
        in_specs=[pl.BlockSpec((tm, d), lambda i: (i, 0)),
                  pl.BlockSpec((1, d), lambda i: (0, 0)),
                  pl.BlockSpec((1, 1, d), lambda i: (i // per_b, 0, 0)),
                  pl.BlockSpec((1, 1, d), lambda i: (i // per_b, 0, 0)),
                  pl.BlockSpec((d, n), lambda i: (0, 0))],
        out_specs=pl.BlockSpec((tm, n), lambda i: (i, 0)),
        out_shape=jax.ShapeDtypeStruct((t, n), F32),
        compiler_params=_cparams(("arbitrary",)),
        name="in_projection",
    )(x, g, shift, scale, w)


def _outproj_kernel(*refs, n_in):
    o_refs = refs[:n_in]
    w_refs = refs[n_in:2 * n_in]
    x_ref, gt_ref, xo_ref = refs[2 * n_in:]
    acc = _dot(o_refs[0][...].astype(BF16), w_refs[0][...])
    for o_ref, w_ref in zip(o_refs[1:], w_refs[1:]):
        acc = acc + _dot(o_ref[...].astype(BF16), w_ref[...])
    xo_ref[...] = x_ref[...] + gt_ref[0] * acc


def out_projection(parts, weights, x, gate, seq):
    t, d = x.shape
    tm = 512
    per_b = seq // tm
    n_in = len(parts)
    in_specs = [pl.BlockSpec((tm, p.shape[1]), lambda i: (i, 0)) for p in parts]
    in_specs += [pl.BlockSpec(w.shape, lambda i: (0, 0)) for w in weights]
    in_specs += [pl.BlockSpec((tm, d), lambda i: (i, 0)),
                 pl.BlockSpec((1, 1, d), lambda i: (i // per_b, 0, 0))]
    return pl.pallas_call(
        functools.partial(_outproj_kernel, n_in=n_in),
        grid=(t // tm,),
        in_specs=in_specs,
        out_specs=pl.BlockSpec((tm, d), lambda i: (i, 0)),
        out_shape=jax.ShapeDtypeStruct((t, d), F32),
        compiler_params=_cparams(("arbitrary",)),
        name="out_projection",
    )(*parts, *weights, x, gate)


def _tri_masks(c):
    row = lax.broadcasted_iota(I32, (c, c), 0)
    col = lax.broadcasted_iota(I32, (c, c), 1)
    return row >= col, row > col, row == col


def _gdn_kernel(qkv_ref, za_ref, sm_ref, cw_ref, nega_ref, dtb_ref, on_ref, o_ref,
                tail_ref, state_ref, *, rb):
    c = GDN_CHUNK
    dk = GDN_DK

    @pl.when(pl.program_id(1) == 0)
    def _():
        tail_ref[...] = jnp.zeros_like(tail_ref)
        state_ref[...] = jnp.zeros_like(state_ref)

    x = qkv_ref[...]
    xx = jnp.concatenate([tail_ref[...], x], axis=0)
    tail_ref[...] = x[rb - SUBLANES:, :]
    cw = cw_ref[...]
    y = x * cw[CONV_WIDTH - 1:CONV_WIDTH, :]
    for i in range(1, CONV_WIDTH):
        y = y + pltpu.roll(xx, i, axis=0)[SUBLANES:, :] * cw[CONV_WIDTH - 1 - i:CONV_WIDTH - i, :]
    y = _silu(y)

    sm = sm_ref[...]
    beta_t = _sigmoid(sm)
    g_t = nega_ref[...] * _softplus(sm + dtb_ref[...])
    za = za_ref[...]
    hw = GDN_HEADS * dk
    nch = rb // c
    row = lax.broadcasted_iota(I32, (rb, rb), 0)
    col = lax.broadcasted_iota(I32, (rb, rb), 1)
    shift = int(np.log2(c))
    same = (row >> shift) == (col >> shift)
    incl = same & (row >= col)
    strict = same & (row > col)
    eye = (row == col).astype(F32)
    incl_b = jnp.where(incl, 1.0, 0.0).astype(BF16)
    gc_all = _dot_mask(incl_b, g_t)

    heads = range(GDN_HEADS)
    qs, ks, vs, bbs, gcs, decays = [], [], [], [], [], []
    for h in heads:
        lo = h * dk
        q = y[:, lo:lo + dk]
        k = y[:, hw + lo:hw + lo + dk]
        qs.append(q * lax.rsqrt(jnp.sum(q * q, axis=-1, keepdims=True) + EPS) * dk ** -0.5)
        ks.append(k * lax.rsqrt(jnp.sum(k * k, axis=-1, keepdims=True) + EPS))
        vs.append(y[:, 2 * hw + lo:2 * hw + lo + dk])
        bbs.append(jnp.broadcast_to(beta_t[:, SM_BA + h:SM_BA + h + 1], (rb, dk)))
        gc = jnp.broadcast_to(gc_all[:, SM_AA + h:SM_AA + h + 1], (rb, dk))
        gc_t = jnp.concatenate([gc[r:r + dk].T for r in range(0, rb, dk)], axis=1)[:1]
        gc_i = jnp.concatenate([gc] * (rb // dk), axis=1)
        decays.append(jnp.where(incl, jnp.exp(jnp.where(incl, gc_i - gc_t, 0.0)), 0.0))
        gcs.append(gc)
    kbetas = [ks[h] * bbs[h] for h in heads]
    ypows = [jnp.where(strict, -(_bdot_nt(kbetas[h], ks[h]) * decays[h]), 0.0) for h in heads]
    tmats = [eye + ypows[h] for h in heads]
    for _ in range(int(np.log2(c)) - 1):
        ypows = [_bdot(ypows[h], ypows[h]) for h in heads]
        tmats = [tmats[h] + _bdot(tmats[h], ypows[h]) for h in heads]
    egcs = [jnp.exp(gcs[h]) for h in heads]
    wus = [_bdot(tmats[h], jnp.concatenate([kbetas[h] * egcs[h], vs[h] * bbs[h]], axis=1))
           for h in heads]
    aqks = [_bdot_nt(qs[h], ks[h]) * decays[h] for h in heads]
    qgs = [qs[h] * egcs[h] for h in heads]
    sts = [state_ref[h] for h in heads]
    v_news = [[] for _ in heads]
    o_inter = [[] for _ in heads]
    for ci in range(nch):
        r0 = ci * c
        for h in heads:
            gl = gcs[h][r0 + c - 1:r0 + c, :]
            kd = ks[h][r0:r0 + c] * jnp.exp(gl - gcs[h][r0:r0 + c])
            ws = _bdot(jnp.concatenate([wus[h][r0:r0 + c, :dk], qgs[h][r0:r0 + c]], axis=0), sts[h])
            v_new = wus[h][r0:r0 + c, dk:] - ws[:c]
            o_inter[h].append(ws[c:])
            sts[h] = sts[h] * jnp.exp(gl) + _bdot_tn(kd, v_new)
            v_news[h].append(v_new)
    for h in heads:
        lo = h * dk
        state_ref[h] = sts[h]
        o = jnp.concatenate(o_inter[h], axis=0) + _bdot(aqks[h], jnp.concatenate(v_news[h], axis=0))
        o = o * lax.rsqrt(jnp.mean(o * o, axis=-1, keepdims=True) + EPS) * on_ref[...]
        o_ref[:, lo:lo + dk] = o * _silu(za[:, lo:lo + dk])


def gdn_mixer(proj, conv_w, a_log, dt_bias, out_norm, batch, seq):
    t = proj.shape[0]
    rb = 256
    nsb = seq // rb
    hw = GDN_HEADS * GDN_DK
    nega =jnp.zeros((1, LANES), F32).at[0, SM_AA:SM_AA + GDN_HEADS].set(-jnp.exp(a_log))
    dtb = jnp.zeros((1, LANES), F32).at[0, SM_AA:SM_AA + GDN_HEADS].set(dt_bias)
    row = lambda b, s: b * nsb + s
    return pl.pallas_call(
        functools.partial(_gdn_kernel, rb=rb),
        grid=(batch, nsb),
        in_specs=[pl.BlockSpec((rb, 3 * hw), lambda b, s: (row(b, s), 0)),
                  pl.BlockSpec((rb, hw), lambda b, s: (row(b, s), EV_ZA // hw)),
                  pl.BlockSpec((rb, LANES), lambda b, s: (row(b, s), EV_SM // LANES)),
                  pl.BlockSpec((CONV_WIDTH, 3 * hw), lambda b, s: (0, 0)),
                  pl.BlockSpec((1, LANES), lambda b, s: (0, 0)),
                  pl.BlockSpec((1, LANES), lambda b, s: (0, 0)),
                  pl.BlockSpec((1, GDN_DK), lambda b, s: (0, 0))],
        out_specs=pl.BlockSpec((rb, hw), lambda b, s: (row(b, s), 0)),
        out_shape=jax.ShapeDtypeStruct((t, hw), F32),
        scratch_shapes=[pltpu.VMEM((SUBLANES, 3 * hw), F32),
                        pltpu.VMEM((GDN_HEADS, GDN_DK, GDN_DK), F32)],
        compiler_params=_cparams(("arbitrary", "arbitrary")),
        name="gdn_mixer",
    )(proj, proj, proj, conv_w, nega, dtb, out_norm.reshape(1, GDN_DK))


def _gla_kernel(qk_ref, v_ref, r_ref, gl_ref, wg_ref, bg_ref, on_ref, o_ref, state_ref, *, rb):
    c = GLA_CHUNK
    dk, dv = GLA_DK, GLA_DV
    sb = GLA_SUB

    @pl.when(pl.program_id(1) == 0)
    def _():
        state_ref[...] = jnp.zeros_like(state_ref)

    z = _dot3(gl_ref[...], wg_ref[...]) + bg_ref[...]
    log_a = -_softplus(-z) * (1.0 / GLA_TAU)
    qk = qk_ref[...]
    vv = v_ref[...]
    rr = r_ref[...]
    incl, _, _ = _tri_masks(c)
    incl_b = incl.astype(F32).astype(BF16)
    hw = GLA_HEADS * dk
    sub_row = lax.broadcasted_iota(I32, (sb, dk), 0)
    att_lane = lax.broadcasted_iota(I32, (sb, c), 1)

    for ci in range(rb // c):
        r0 = ci * c
        b_all = _dot_mask(incl_b, log_a[r0:r0 + c])
        for h in range(GLA_HEADS):
            q = qk[r0:r0 + c, h * dk:(h + 1) * dk] * dk ** -0.5
            k = qk[r0:r0 + c, hw + h * dk:hw + (h + 1) * dk]
            v = vv[r0:r0 + c, h * dv:(h + 1) * dv]
            b = b_all[:, h * dk:(h + 1) * dk]
            st = state_ref[h]
            o = _bdot_nt(q * jnp.exp(b), st)
            att_rows = []
            for bi in range(c // sb):
                s0 = bi * sb
                qi = q[s0:s0 + sb]
                bi_ = b[s0:s0 + sb]
                ref = b[s0:s0 + 1]
                if bi > 0:
                    qt = qi * jnp.exp(bi_ - ref)
                    kt = k * jnp.exp(jnp.minimum(ref - b, 0.0))
                    att = jnp.where(att_lane < s0, _bdot_nt(qt, kt), 0.0)
                else:
                    att = jnp.zeros((sb, c), F32)
                for j in range(sb):
                    kj = k[s0 + j:s0 + j + 1]
                    bj = b[s0 + j:s0 + j + 1]
                    e = jnp.exp(jnp.where(sub_row >= j, bi_ - bj, NEG_INF))
                    col = jnp.sum(qi * kj * e, axis=-1, keepdims=True)
                    att = jnp.where(att_lane == s0 + j, col, att)
                att_rows.append(att)
            o = o + _bdot(jnp.concatenate(att_rows, axis=0), v)
            bl = b[c - 1:c]
            state_ref[h] = st * jnp.exp(bl) + _bdot_tn(v, k * jnp.exp(bl - b))
            o = o * lax.rsqrt(jnp.mean(o * o, axis=-1, keepdims=True) + EPS) * on_ref[...]
            o_ref[r0:r0 + c, h * dv:(h + 1) * dv] = o * _silu(rr[r0:r0 + c, h * dv:(h + 1) * dv])


def gla_mixer(proj, w_gate2, b_gate2, out_norm, batch, seq):
    t = proj.shape[0]
    rb = GLA_CHUNK
    nsb = seq // rb
    hw = GLA_HEADS * GLA_DK
    hv = GLA_HEADS * GLA_DV
    wg = jnp.zeros((LANES, hw), F32).at[:GLA_RANK].set(w_gate2)
    row = lambda b, s: b * nsb + s
    return pl.pallas_call(
        functools.partial(_gla_kernel, rb=rb),
        grid=(batch, nsb),
        in_specs=[pl.BlockSpec((rb, 2 * hw), lambda b, s: (row(b, s), 0)),
                  pl.BlockSpec((rb, hv), lambda b, s: (row(b, s), OD_V // hv)),
                  pl.BlockSpec((rb, hv), lambda b, s: (row(b, s), OD_R // hv)),
                  pl.BlockSpec((rb, LANES), lambda b, s: (row(b, s), OD_G // LANES)),
                  pl.BlockSpec((LANES, hw), lambda b, s: (0, 0)),
                  pl.BlockSpec((1, hw), lambda b, s: (0, 0)),
                  pl.BlockSpec((1, GLA_DV), lambda b, s: (0, 0))],
        out_specs=pl.BlockSpec((rb, hv), lambda b, s: (row(b, s), 0)),
        out_shape=jax.ShapeDtypeStruct((t, hv), F32),
        scratch_shapes=[pltpu.VMEM((GLA_HEADS, GLA_DV, GLA_DK), F32)],
        compiler_params=_cparams(("arbitrary", "arbitrary")),
        name="gla_mixer",
    )(proj, proj, proj, proj, wg, b_gate2.reshape(1, hw), out_norm.reshape(1, GLA_DV))


def _rope_table_kernel(pos_ref, inva_ref, invi_ref, ca_ref, sa_ref, ci_ref, si_ref):
    pos = pos_ref[...].astype(F32)
    lane = lax.broadcasted_iota(I32, (pos.shape[0], LANES), 1)
    ang_a = pos * inva_ref[...]
    ca_ref[...] = jnp.cos(ang_a)
    sa_ref[...] = jnp.where(lane < ATT_DH // 2, -1.0, 1.0) * jnp.sin(ang_a)
    ang_i = pos * invi_ref[...]
    ci_ref[...] = jnp.cos(ang_i)
    si_ref[...] = jnp.where(lane % IDX_DH < IDX_DH // 2, -1.0, 1.0) * jnp.sin(ang_i)


def rope_tables(positions):
    t = positions.size
    inv_a = 1.0 / (ROPE_THETA ** (jnp.arange(0, ATT_DH, 2, dtype=F32) / ATT_DH))
    inv_i = 1.0 / (ROPE_THETA ** (jnp.arange(0, IDX_DH, 2, dtype=F32) / IDX_DH))
    inv_a = jnp.tile(inv_a, 2).reshape(1, LANES)
    inv_i = jnp.tile(inv_i, 4).reshape(1, LANES)
    tm = 512
    spec = pl.BlockSpec((tm, LANES), lambda i: (i, 0))
    vec = pl.BlockSpec((1, LANES), lambda i: (0, 0))
    return pl.pallas_call(
        _rope_table_kernel,
        grid=(t // tm,),
        in_specs=[pl.BlockSpec((tm, 1), lambda i: (i, 0)), vec, vec],
        out_specs=[spec] * 4,
        out_shape=[jax.ShapeDtypeStruct((t, LANES), F32)] * 4,
        compiler_params=_cparams(("arbitrary",)),
        name="rope_tables",
    )(positions.reshape(t, 1), inv_a, inv_i)


def _dsa_prep_kernel(qb_ref, kb_ref, vb_ref, qi_ref, sm_ref, ca_ref, sa_ref, ci_ref, si_ref,
                     qo_ref, ko_ref, vt_ref, qio_ref, kio_ref, kn_ref):
    ca, sa, ci, si = ca_ref[...], sa_ref[...], ci_ref[...], si_ref[...]
    lane = lax.broadcasted_iota(I32, ca.shape, 1)
    first_half = lane % IDX_DH < IDX_DH // 2
    knorm = jnp.zeros((SUBLANES, LANES), F32)
    lane8 = lax.broadcasted_iota(I32, (SUBLANES, LANES), 1)

    def rope_att(tile):
        return tile * ca + pltpu.roll(tile, ATT_DH // 2, axis=1) * sa

    def rope_idx(tile):
        partner = jnp.where(first_half, pltpu.roll(tile, LANES - IDX_DH // 2, axis=1),
                            pltpu.roll(tile, IDX_DH // 2, axis=1))
        return tile * ci + partner * si

    qb, kb = qb_ref[...], kb_ref[...]
    for h in range(ATT_HEADS):
        sl = slice(h * ATT_DH, (h + 1) * ATT_DH)
        qo_ref[:, sl] = (rope_att(qb[:, sl]) * ATT_DH ** -0.5).astype(BF16)
        kr = rope_att(kb[:, sl]).astype(BF16)
        ko_ref[:, sl] = kr
        krf = kr.astype(F32)
        knorm = jnp.where(lane8 == h, jnp.max(jnp.sum(krf * krf, axis=1, keepdims=True)), knorm)
    kn_ref[0] = knorm
    vt_ref[0] = vb_ref[...].T.astype(BF16)
    qi = qi_ref[...]
    for p in range(IDX_HEADS * IDX_DH // LANES):
        sl = slice(p * LANES, (p + 1) * LANES)
        qio_ref[:, sl] = rope_idx(qi[:, sl]).astype(BF16)
    ki = rope_idx(sm_ref[...])
    kio_ref[...] = jnp.where(lane < IDX_DH, ki, pltpu.roll(ki, IDX_DH, axis=1)).astype(BF16)


def dsa_prep(proj, tables, seq):
    t = proj.shape[0]
    tm = min(DSA_KEY_CHUNK, seq)
    aw = ATT_HEADS * ATT_DH
    iw = IDX_HEADS * IDX_DH
    tab = pl.BlockSpec((tm, LANES), lambda i: (i, 0))
    return pl.pallas_call(
        _dsa_prep_kernel,
        grid=(t // tm,),
        in_specs=[pl.BlockSpec((tm, aw), lambda i: (i, EV_QB // aw)),
                  pl.BlockSpec((tm, aw), lambda i: (i, EV_KB // aw)),
                  pl.BlockSpec((tm, aw), lambda i: (i, EV_VB // aw)),
                  pl.BlockSpec((tm, iw), lambda i: (i, EV_QI // iw)),
                  pl.BlockSpec((tm, LANES), lambda i: (i, EV_SM // LANES)),
                  tab, tab, tab, tab],
        out_specs=[pl.BlockSpec((tm, aw), lambda i: (i, 0)),
                   pl.BlockSpec((tm, aw), lambda i: (i, 0)),
                   pl.BlockSpec((1, aw, tm), lambda i: (i, 0, 0)),
                   pl.BlockSpec((tm, iw), lambda i: (i, 0)),
                   pl.BlockSpec((tm, LANES), lambda i: (i, 0)),
                   pl.BlockSpec((1, SUBLANES, LANES), lambda i: (i, 0, 0))],
        out_shape=[jax.ShapeDtypeStruct((t, aw), BF16),
                   jax.ShapeDtypeStruct((t, aw), BF16),
                   jax.ShapeDtypeStruct((t // tm, aw, tm), BF16),
                   jax.ShapeDtypeStruct((t, iw), BF16),
                   jax.ShapeDtypeStruct((t, LANES), BF16),
                   jax.ShapeDtypeStruct((t // tm, SUBLANES, LANES), F32)],
        compiler_params=_cparams(("arbitrary",)),
        name="dsa_prep",
    )(proj, proj, proj, proj, proj, *tables)


def _fold_rows(x, op):
    parts = [x[r:r + SUBLANES] for r in range(0, x.shape[0], SUBLANES)]
    while len(parts) > 1:
        parts = [op(parts[j], parts[j + 1]) for j in range(0, len(parts), 2)]
    return parts[0]


def _dsa_kernel(qb_ref, qi_ref, sm_ref, kn_ref, k_ref, vt_ref, ki_ref, o_ref,
                key_ref, l_ref, acc_ref, *, kc, topk):
    nq = Q_BLOCK
    i = pl.program_id(1)
    nkc = (i * nq + nq + kc - 1) // kc
    q_pos = i * nq + lax.broadcasted_iota(I32, (1, nq), 1)
    key0 = lax.broadcasted_iota(I32, (kc, nq), 0)
    lane = lax.broadcasted_iota(I32, (nq, LANES), 1)
    wi_t = sm_ref[...].T * (IDX_HEADS * IDX_DH) ** -0.5
    qi = qi_ref[...]
    zero_b = jnp.zeros((nq, LANES), BF16)
    qi_heads = []
    for h in range(IDX_HEADS):
        pair = qi[:, (h // 2) * LANES:(h // 2 + 1) * LANES]
        own = (lane < IDX_DH) if h % 2 == 0 else (lane >= IDX_DH)
        qi_heads.append(jnp.where(own, pair, zero_b))

    def score_body(c, carry):
        start = pl.multiple_of(c * kc, kc)
        kic = ki_ref[0, pl.ds(start, kc), :]
        dots = [_dot_nt(kic, qi_heads[h]) for h in range(IDX_HEADS)]
        s = jnp.zeros((kc, nq), F32)
        for h in range(IDX_HEADS):
            s = s + wi_t[SM_WI + h:SM_WI + h + 1, :] * jnp.maximum(dots[h], 0.0)
        s = s + 0.0
        s = jnp.where(key0 <= q_pos - c * kc, s, NEG_INF)
        bits = pltpu.bitcast(s, I32)
        key_ref[c] = bits ^ ((bits >> 31) & 0x7FFFFFFF)
        return carry

    lax.fori_loop(0, nkc, score_body, 0)

    def count(pred):
        def body(c, acc):
            return acc + _fold_rows(jnp.where(pred(key_ref[c], c), 1, 0).astype(I32), jnp.add)
        acc = lax.fori_loop(0, nkc, body, jnp.zeros((SUBLANES, nq), I32))
        return jnp.sum(acc, axis=0, keepdims=True)

    def bound_body(c, best):
        key = key_ref[c]
        for r in range(0, kc, topk):
            best = jnp.maximum(best, key[r:r + topk])
        return best

    best = lax.fori_loop(0, nkc, bound_body, jnp.full((topk, nq), INT_MIN, I32))
    lo = jnp.min(_fold_rows(best, jnp.minimum), axis=0, keepdims=True)
    hi = jnp.max(_fold_rows(best, jnp.maximum), axis=0, keepdims=True)
    nbits = jnp.max(32 - lax.clz(lo ^ hi))
    low_mask = jnp.where(nbits >= 32, -1, (jnp.int32(1) << jnp.minimum(nbits, 31)) - 1)
    thr = jnp.where(nbits >= 32, INT_MIN, lo & ~low_mask).astype(I32)

    def bit_body(it, thr):
        cand = thr + (jnp.int32(1) << (nbits - 1 - it))
        return jnp.where(count(lambda key, c: key >= cand) >= topk, cand, thr)

    thr = lax.fori_loop(0, nbits, bit_body, thr)

    n_gt = count(lambda key, c: key > thr)
    n_eq = count(lambda key, c: key == thr)
    need = topk - n_gt
    big = jnp.int32(2 ** 30)

    def tie_search():
        def idx_body(it, x):
            cand = x + (jnp.int32(1) << (14 - it))
            below = count(lambda key, c: (key0 < jnp.where(key == thr, cand - c * kc, 0)))
            return jnp.where(below < need, cand, x)
        return lax.fori_loop(0, 15, idx_body, jnp.zeros((1, nq), I32))

    any_tie = jnp.max(jnp.where(n_eq > need, 1, 0)) > 0
    last = lax.cond(any_tie, tie_search, lambda: jnp.full((1, nq), big, I32))

    last = jnp.where(thr == KEY_NEG_INF, -1, last)
    zero_f = jnp.float32(0.0)
    ninf_f = jnp.float32(NEG_INF)
    qb = qb_ref[...]

    def select_bias(c):
        key = key_ref[c]
        tie_ok = jnp.where(key0 <= last - c * kc, zero_f, ninf_f)
        return jnp.where(key > thr, zero_f, jnp.where(key == thr, tie_ok, ninf_f))

    def chunk_logits(c, h, bias):
        start = pl.multiple_of(c * kc, kc)
        kch = k_ref[0, pl.ds(start, kc), h * ATT_DH:(h + 1) * ATT_DH]
        return _dot_nt(kch, qb[:, h * ATT_DH:(h + 1) * ATT_DH]) + bias

    def attend(shifts):
        l_ref[...] = jnp.zeros(l_ref.shape, F32)
        acc_ref[...] = jnp.zeros(acc_ref.shape, F32)

        def body(c, carry):
            bias = select_bias(c)
            heads = range(ATT_HEADS)
            logits = [chunk_logits(c, h, bias) for h in heads]
            ps = [jnp.exp(logits[h] - shifts[h]) for h in heads]
            for h in heads:
                l_ref[h] = l_ref[h] + _fold_rows(ps[h], jnp.add)
                vth = vt_ref[0, c, h * ATT_DH:(h + 1) * ATT_DH, :]
                acc_ref[h] = acc_ref[h] + _dot(vth, ps[h].astype(BF16))
            return carry

        lax.fori_loop(0, nkc, body, 0)
        return [jnp.sum(l_ref[h], axis=0, keepdims=True) for h in range(ATT_HEADS)]

    ones_b = jnp.ones((SUBLANES, ATT_DH), BF16)
    bounds = []
    for h in range(ATT_HEADS):
        qh = qb[:, h * ATT_DH:(h + 1) * ATT_DH].astype(F32)
        qsq = _dot_nt(ones_b, (qh * qh).astype(BF16))[0:1, :]
        bounds.append(jnp.sqrt(qsq * kn_ref[0][0:1, h:h + 1]) * 1.02)
    sums = attend(bounds)

    def write_out(sums):
        for h in range(ATT_HEADS):
            o_ref[:, h * ATT_DH:(h + 1) * ATT_DH] = (acc_ref[h] / sums[h]).T

    write_out(sums)
    lowest = jnp.minimum(jnp.minimum(sums[0], sums[1]), jnp.minimum(sums[2], sums[3]))

    @pl.when(jnp.min(lowest) <= 0.0)
    def _():
        l_ref[...] = jnp.full(l_ref.shape, NEG_INF, F32)

        def max_body(c, carry):
            bias = select_bias(c)
            for h in range(ATT_HEADS):
                l_ref[h] = jnp.maximum(l_ref[h], _fold_rows(chunk_logits(c, h, bias), jnp.maximum))
            return carry

        lax.fori_loop(0, nkc, max_body, 0)
        exact = [jnp.max(l_ref[h], axis=0, keepdims=True) for h in range(ATT_HEADS)]
        write_out(attend(exact))


def dsa_attention(proj, qb, kb, vt, qi, ki, knorm, batch, seq):
    t = proj.shape[0]
    aw = ATT_HEADS * ATT_DH
    iw = IDX_HEADS * IDX_DH
    kc = min(DSA_KEY_CHUNK, seq)
    nkc = seq // kc
    topk = min(TOPK_MAX, seq // 4)
    nq = seq // Q_BLOCK
    row = lambda b, i: b * nq + i
    kb3, ki3 = (a.reshape(batch, seq, a.shape[1]) for a in (kb, ki))
    vt4 = vt.reshape(batch, nkc, aw, kc)
    kmax = jnp.max(knorm.reshape(batch, nkc, SUBLANES, LANES), axis=1)
    resident = lambda shape: pl.BlockSpec((1,) + shape, lambda b, i: (b,) + (0,) * len(shape),
                                          pipeline_mode=pl.Buffered(1))
    return pl.pallas_call(
        functools.partial(_dsa_kernel, kc=kc, topk=topk),
        grid=(batch, nq),
        in_specs=[pl.BlockSpec((Q_BLOCK, aw), lambda b, i: (row(b, i), 0)),
                  pl.BlockSpec((Q_BLOCK, iw), lambda b, i: (row(b, i), 0)),
                  pl.BlockSpec((Q_BLOCK, LANES), lambda b, i: (row(b, i), EV_SM // LANES)),
                  pl.BlockSpec((1, SUBLANES, LANES), lambda b, i: (b, 0, 0)),
                  resident((seq, aw)), resident((nkc, aw, kc)), resident((seq, LANES))],
        out_specs=pl.BlockSpec((Q_BLOCK, aw), lambda b, i: (row(b, i), 0)),
        out_shape=jax.ShapeDtypeStruct((t, aw), F32),
        scratch_shapes=[pltpu.VMEM((nkc, kc, Q_BLOCK), I32),
                        pltpu.VMEM((ATT_HEADS, SUBLANES, Q_BLOCK), F32),
                        pltpu.VMEM((ATT_HEADS, ATT_DH, Q_BLOCK), F32)],
        compiler_params=_cparams(("arbitrary", "arbitrary")),
        name="dsa_attention",
    )(qb, qi, proj, kmax, kb3, vt4, ki3)


def _router_kernel(x_ref, g_ref, sh_ref, sc_ref, wr_ref, br_ref, h_ref, route_ref, cnt_ref,
                   carry_ref):
    @pl.when(pl.program_id(0) == 0)
    def _():
        carry_ref[...] = jnp.zeros_like(carry_ref)

    h = _norm_mod(x_ref[...], g_ref[...], sh_ref[0], sc_ref[0])
    h_ref[...] = h
    tm = h.shape[0]
    logits = _dot(h, wr_ref[...], HI) + br_ref[...]
    lane = lax.broadcasted_iota(I32, (tm, LANES), 1)
    far = jnp.int32(LANES)

    def first_max(vals):
        m = jnp.max(vals, axis=1, keepdims=True)
        return m, jnp.min(jnp.where(vals == m, lane, far), axis=1, keepdims=True)

    is_group = lane < N_GROUPS
    gmax, gsel = first_max(jnp.where(is_group, logits, NEG_INF))
    gsum = jnp.sum(jnp.where(is_group, jnp.exp(logits - gmax), 0.0), axis=1, keepdims=True)
    lo = N_GROUPS + EXPERTS_PER_GROUP * gsel
    el = jnp.where((lane >= lo) & (lane < lo + EXPERTS_PER_GROUP), logits, NEG_INF)
    m1, i1 = first_max(el)
    m2, i2 = first_max(jnp.where(lane == i1, NEG_INF, el))
    e2 = jnp.exp(m2 - m1)
    gate1 = 1.0 / (gsum * (1.0 + e2))
    gate2 = e2 * gate1
    eid1 = i1 - N_GROUPS
    eid2 = i2 - N_GROUPS
    oh1 = lane == eid1
    oh2 = lane == eid2
    onehot = jnp.where(oh1 | oh2, 1.0, 0.0)
    r = lax.broadcasted_iota(I32, (tm, tm), 0)
    cidx = lax.broadcasted_iota(I32, (tm, tm), 1)
    below = jnp.where(r > cidx, 1.0, 0.0).astype(BF16)
    cum = _dot(below, onehot.astype(BF16)) + carry_ref[0:1, :]
    rank1 = jnp.sum(jnp.where(oh1, cum, 0.0), axis=1, keepdims=True)
    rank2 = jnp.sum(jnp.where(oh2, cum, 0.0), axis=1, keepdims=True)
    carry_ref[...] = carry_ref[...] + jnp.sum(onehot, axis=0, keepdims=True)
    cnt_ref[...] = carry_ref[...]
    out = jnp.zeros((tm, LANES), F32)
    for j, val in enumerate((eid1.astype(F32), eid2.astype(F32), gate1, gate2, rank1, rank2)):
        out = jnp.where(lane == j, val, out)
    route_ref[...] = out


def moe_router(x, g, shift, scale, wr, br, seq):
    t, d = x.shape
    tm = 512
    per_b = seq // tm
    return pl.pallas_call(
        _router_kernel,
        grid=(t // tm,),
        in_specs=[pl.BlockSpec((tm, d), lambda i: (i, 0)),
                  pl.BlockSpec((1, d), lambda i: (0, 0)),
                  pl.BlockSpec((1, 1, d), lambda i: (i // per_b, 0, 0)),
                  pl.BlockSpec((1, 1, d), lambda i: (i // per_b, 0, 0)),
                  pl.BlockSpec((d, LANES), lambda i: (0, 0)),
                  pl.BlockSpec((1, LANES), lambda i: (0, 0))],
        out_specs=[pl.BlockSpec((tm, d), lambda i: (i, 0)),
                   pl.BlockSpec((tm, LANES), lambda i: (i, 0)),
                   pl.BlockSpec((SUBLANES, LANES), lambda i: (0, 0))],
        out_shape=[jax.ShapeDtypeStruct((t, d), F32),
                   jax.ShapeDtypeStruct((t, LANES), F32),
                   jax.ShapeDtypeStruct((SUBLANES, LANES), F32)],
        scratch_shapes=[pltpu.VMEM((SUBLANES, LANES), F32)],
        compiler_params=_cparams(("arbitrary",)),
        name="moe_router",
    )(x, g, shift, scale, wr, br)


def _zero_kernel(o_ref):
    o_ref[...] = jnp.zeros_like(o_ref)


def zero_rows(rows, d):
    tm = 1024
    return pl.pallas_call(
        _zero_kernel,
        grid=(rows // tm,),
        out_specs=pl.BlockSpec((tm, d), lambda i: (i, 0)),
        out_shape=jax.ShapeDtypeStruct((rows, d), F32),
        compiler_params=_cparams(("arbitrary",)),
        name="zero_rows",
    )()


def _dispatch_kernel(dest_ref, seg_ref, h_ref, xs_in_ref, xs_ref, inv_ref, sem, *, tm):
    del xs_in_ref
    i = pl.program_id(0)
    base = i * tm * 2

    rows = MOE_ROWS
    n_assign = dest_ref.shape[0]

    @pl.when(i == 0)
    def _():
        def mark(p, carry):
            blk = p // rows
            spare = ((blk + MOE_OUT_BUFS - 1) % MOE_OUT_BUFS) * rows + (p - blk * rows)
            inv_ref[p] = n_assign + spare
            return carry
        lax.fori_loop(0, rows, mark, 0)
        for e in range(N_EXPERTS):
            lax.fori_loop(rows + seg_ref[e], rows + seg_ref[N_EXPERTS + e], mark, 0)
        lax.fori_loop(rows + seg_ref[2 * N_EXPERTS - 1], inv_ref.shape[0], mark, 0)

    def row_copy(r, d):
        return pltpu.make_async_copy(h_ref.at[pl.ds(r, 1)], xs_ref.at[pl.ds(d, 1)], sem)

    def start(r, carry):
        for j in range(2):
            a = base + 2 * r + j
            d = dest_ref[a]
            inv_ref[rows + d] = a
            row_copy(r, d).start()
        return carry

    def wait(r, carry):
        row_copy(0, 0).wait()
        row_copy(0, 0).wait()
        return carry

    lax.fori_loop(0, tm, start, 0)
    lax.fori_loop(0, tm, wait, 0)


def moe_dispatch(dest, seg, h, xs_zero):
    t, d = h.shape
    tm = 256
    return pl.pallas_call(
        functools.partial(_dispatch_kernel, tm=tm),
        grid_spec=pltpu.PrefetchScalarGridSpec(
            num_scalar_prefetch=2,
            grid=(t // tm,),
            in_specs=[pl.BlockSpec((tm, d), lambda i, dest, seg: (i, 0)),
                      pl.BlockSpec(memory_space=pl.ANY)],
            out_specs=[pl.BlockSpec(memory_space=pl.ANY),
                       pl.BlockSpec(memory_space=pltpu.SMEM)],
            scratch_shapes=[pltpu.SemaphoreType.DMA(())]),
        out_shape=[jax.ShapeDtypeStruct(xs_zero.shape, F32),
                   jax.ShapeDtypeStruct((xs_zero.shape[0] + MOE_ROWS,), I32)],
        input_output_aliases={3: 0},
        compiler_params=_cparams(("arbitrary",)),
        name="moe_dispatch",
    )(dest, seg, h, xs_zero)


MOE_OUT_BUFS = 3


def _expert_kernel(be_ref, inv_ref, xs_ref, w1_ref, w3_ref, w2_ref, y_ref,
                   w1b, w3b, w2b, ybuf, sems, *, n_assign):
    i = pl.program_id(0)
    last = pl.num_programs(0) - 1
    rows = ybuf.shape[1]
    slot = i % MOE_OUT_BUFS
    prev_slot = (i + MOE_OUT_BUFS - 1) % MOE_OUT_BUFS
    prev = be_ref[jnp.maximum(i - 1, 0)]

    def row_copy(s, r, dst):
        return pltpu.make_async_copy(ybuf.at[s, pl.ds(r, 1)], y_ref.at[pl.ds(dst, 1)], sems.at[s])

    def send(s, block):
        for r in range(rows):
            row_copy(s, r, inv_ref[(block + 1) * rows + r]).start()

    def drain(s):
        for r in range(rows):
            row_copy(s, 0, 0).wait()

    @pl.when(i == 0)
    def _():
        ybuf[prev_slot] = jnp.zeros((rows, ybuf.shape[2]), F32)
        for s in range(MOE_OUT_BUFS):
            if s != MOE_OUT_BUFS - 1:
                for r in range(rows):
                    row_copy(prev_slot, r, n_assign + s * rows + r).start()
                drain(prev_slot)

    @pl.when(i >= MOE_OUT_BUFS - 1)
    def _():
        drain(slot)

    @pl.when((i == 0) | (be_ref[i] != prev))
    def _():
        w1b[...] = w1_ref[0, 0].astype(BF16)
        w3b[...] = w3_ref[0, 0].astype(BF16)
        w2b[...] = w2_ref[0, 0].astype(BF16)

    send(prev_slot, i - 1)
    x = xs_ref[...].astype(BF16)
    a = _dot(x, w1b[...])
    hid = _silu(a) * _dot(x, w3b[...])
    ybuf[slot] = _dot(hid.astype(BF16), w2b[...])

    @pl.when(i == last)
    def _():
        send(slot, i)
        for s in range(MOE_OUT_BUFS):
            drain(s)


def moe_experts(blk_exp, inv, xs, w1, w3, w2, layer, t):
    p, d = xs.shape
    de = w1.shape[3]
    rows = MOE_ROWS
    wmap = lambda i, be, inv: (layer, be[i], 0, 0)
    return pl.pallas_call(
        functools.partial(_expert_kernel, n_assign=2 * t),
        grid_spec=pltpu.PrefetchScalarGridSpec(
            num_scalar_prefetch=2,
            grid=(p // rows,),
            in_specs=[pl.BlockSpec((rows, d), lambda i, be, inv: (i, 0)),
                      pl.BlockSpec((1, 1, d, de), wmap),
                      pl.BlockSpec((1, 1, d, de), wmap),
                      pl.BlockSpec((1, 1, de, d), wmap)],
            out_specs=pl.BlockSpec(memory_space=pl.ANY),
            scratch_shapes=[pltpu.VMEM((d, de), BF16), pltpu.VMEM((d, de), BF16),
                            pltpu.VMEM((de, d), BF16), pltpu.VMEM((MOE_OUT_BUFS, rows, d), F32),
                            pltpu.SemaphoreType.DMA((MOE_OUT_BUFS,))]),
        out_shape=jax.ShapeDtypeStruct((2 * t + MOE_OUT_BUFS * rows, d), F32),
        compiler_params=_cparams(("arbitrary",)),
        name="moe_experts",
    )(blk_exp, inv, xs, w1, w3, w2)


def _combine_kernel(y_ref, route_ref, x_ref, gt_ref, *rest, final):
    d = x_ref.shape[1]
    route = route_ref[...]
    y = route[:, 2:3] * y_ref[:, :d] + route[:, 3:4] * y_ref[:, d:]
    out = x_ref[...] + gt_ref[0] * y
    if final:
        fn_ref, o_ref = rest
        out = out * lax.rsqrt(jnp.mean(out * out, axis=-1, keepdims=True) + EPS) * fn_ref[...]
    else:
        (o_ref,) = rest
    o_ref[...] = out


def moe_combine(y2, route, x, gate, seq, final_gain=None):
    t, d = x.shape
    tm = 512
    per_b = seq // tm
    final = final_gain is not None
    in_specs = [pl.BlockSpec((tm, 2 * d), lambda i: (i, 0)),
                pl.BlockSpec((tm, LANES), lambda i: (i, 0)),
                pl.BlockSpec((tm, d), lambda i: (i, 0)),
                pl.BlockSpec((1, 1, d), lambda i: (i // per_b, 0, 0))]
    args = [y2.reshape(-1, 2 * d), route, x, gate]
    if final:
        in_specs.append(pl.BlockSpec((1, d), lambda i: (0, 0)))
        args.append(final_gain)
    return pl.pallas_call(
        functools.partial(_combine_kernel, final=final),
        grid=(t // tm,),
        in_specs=in_specs,
        out_specs=pl.BlockSpec((tm, d), lambda i: (i, 0)),
        out_shape=jax.ShapeDtypeStruct((t, d), F32),
        compiler_params=_cparams(("arbitrary",)),
        name="moe_combine",
    )(*args)


def hier_moe_layer(x, g, shift, scale, gate, wg, bg, we, be, w1, w3, w2, layer, seq,
                   final_gain=None):
    t, d = x.shape
    wr = jnp.zeros((d, LANES), F32)
    wr = wr.at[:, :N_GROUPS].set(wg)
    wr = wr.at[:, N_GROUPS:N_GROUPS + N_EXPERTS].set(we.transpose(1, 0, 2).reshape(d, N_EXPERTS))
    br = jnp.zeros((1, LANES), F32)
    br = br.at[0, :N_GROUPS].set(bg).at[0, N_GROUPS:N_GROUPS + N_EXPERTS].set(be.reshape(-1))
    h, route, cnt = moe_router(x, g, shift, scale, wr, br, seq)
    counts = cnt[0, :N_EXPERTS].astype(I32)
    padded = ((counts + MOE_ROWS - 1) // MOE_ROWS) * MOE_ROWS
    pad_end = jnp.cumsum(padded)
    pad_start = pad_end - padded
    eid = route[:, 0:2].astype(I32)
    rank = route[:, 4:6].astype(I32)
    experts = jnp.arange(N_EXPERTS, dtype=I32)
    start_of = jnp.sum(jnp.where(eid[:, :, None] == experts, pad_start, 0), axis=-1)
    dest = (start_of + rank).reshape(-1)
    p_rows = 2 * t + N_EXPERTS * MOE_ROWS
    nblk = p_rows // MOE_ROWS
    blk_row = jnp.arange(nblk, dtype=I32)[:, None] * MOE_ROWS
    blk_exp = jnp.minimum(jnp.sum((pad_end[None, :] <= blk_row).astype(I32), axis=1),
                          N_EXPERTS - 1)
    seg = jnp.concatenate([pad_start + counts, pad_end]).astype(I32)
    xs, inv = moe_dispatch(dest, seg, h, zero_rows(p_rows, d))
    y2 = moe_experts(blk_exp, inv, xs, w1, w3, w2, layer, t)
    return moe_combine(y2, route, x, gate, seq, final_gain)


def _even_weight(w_in):
    d = w_in.shape[0]
    sizes = [512, 512, 512, 512, 4, 4, 512, 512, 512, 256, 64, 4]
    offs = np.concatenate([[0], np.cumsum(sizes)])
    qa, ka, va, za, ba, aa, qb, kb, vb, qi, ki, wi = (w_in[:, offs[j]:offs[j + 1]]
                                                      for j in range(len(sizes)))
    pad = jnp.zeros((d, EV_N - EV_SM - 76), w_in.dtype)
    return jnp.concatenate([qa, ka, va, za, qb, kb, vb, qi, ki, ba, aa, wi, pad], axis=1)


def kernel(x, c, positions, ada_w, ada_b, norm_mix, norm_ffn, even_w_in, gdn_conv_w, gdn_a_log, gdn_dt_bias, gdn_out_norm, even_w_out, gla_w_in, gla_w_gate2, gla_b_gate2, gla_out_norm, gla_w_out, router_group_w, router_group_b, router_exp_w, router_exp_b, exp_w1, exp_w3, exp_w2, final_norm):
    batch, seq, d = x.shape
    depth = ada_w.shape[0]
    t = batch * seq
    xt = x.reshape(t, d)
    mod = ada_modulation(c, ada_w, ada_b)
    tables = rope_tables(positions)
    for layer in range(depth):
        sh_m, sc_m, gt_m, sh_f, sc_f, gt_f = (mod[layer, :, j * d:(j + 1) * d].reshape(batch, 1, d)
                                              for j in range(6))
        i = layer // 2
        g_mix = norm_mix[layer].reshape(1, d)
        if layer % 2 == 0:
            w_in = _even_weight(even_w_in[i]).astype(BF16)
            proj = in_projection(xt, g_mix, sh_m, sc_m, w_in, seq)
            oa = gdn_mixer(proj, gdn_conv_w[i], gdn_a_log[i], gdn_dt_bias[i], gdn_out_norm[i],
                           batch, seq)
            qb, kb, vt, qi, ki, knorm = dsa_prep(proj, tables, seq)
            ob = dsa_attention(proj, qb, kb, vt, qi, ki, knorm, batch, seq)
            w_out = even_w_out[i].astype(BF16)
            half = GDN_HEADS * GDN_DK
            xt = out_projection([oa, ob], [w_out[:half], w_out[half:]], xt, gt_m, seq)
        else:
            w_in = jnp.pad(gla_w_in[i], ((0, 0), (0, OD_N - gla_w_in.shape[2]))).astype(BF16)
            proj = in_projection(xt, g_mix, sh_m, sc_m, w_in, seq)
            o = gla_mixer(proj, gla_w_gate2[i], gla_b_gate2[i], gla_out_norm[i], batch, seq)
            xt = out_projection([o], [gla_w_out[i].astype(BF16)], xt, gt_m, seq)
        xt = hier_moe_layer(xt, norm_ffn[layer].reshape(1, d), sh_f, sc_f, gt_f,
                            router_group_w[layer], router_group_b[layer],
                            router_exp_w[layer], router_exp_b[layer],
                            exp_w1, exp_w3, exp_w2, layer, seq,
                            final_norm.reshape(1, d) if layer == depth - 1 else None)
    return xt.reshape(batch, seq, d)
```

```python
import functools

import jax
import jax.numpy as jnp
import numpy as np
from jax import lax
from jax.experimental import pallas as pl
from jax.experimental.pallas import tpu as pltpu

F32 = jnp.float32
BF16 = jnp.bfloat16
I32 = jnp.int32
HI = lax.Precision.HIGHEST

LANES = 128
SUBLANES = 8
VMEM_LIMIT = 56 * 1024 * 1024

EPS = 1e-6
ROPE_THETA = 10000.0
GDN_HEADS, GDN_DK = 4, 128
CONV_WIDTH = 4
GDN_CHUNK = 64
ATT_HEADS, ATT_DH = 4, 128
IDX_HEADS, IDX_DH = 4, 64
TOPK_MAX = 256
Q_BLOCK = 128
DSA_KEY_CHUNK = 512
GLA_HEADS, GLA_DK, GLA_DV = 4, 128, 256
GLA_RANK = 16
GLA_TAU = 16.0
GLA_CHUNK = 64
GLA_SUB = 16
N_GROUPS, EXPERTS_PER_GROUP = 4, 8
N_EXPERTS = N_GROUPS * EXPERTS_PER_GROUP
MOE_ROWS = 256

EV_QA, EV_KA, EV_VA, EV_ZA = 0, 512, 1024, 1536
EV_QB, EV_KB, EV_VB, EV_QI, EV_SM = 2048, 2560, 3072, 3584, 3840
EV_N = 3968
SM_KI, SM_BA, SM_AA, SM_WI = 0, 64, 68, 72
OD_Q, OD_K, OD_V, OD_R, OD_G = 0, 512, 1024, 2048, 3072
OD_N = 3200

NEG_INF = float("-inf")
INT_MIN = -(2 ** 31)
KEY_NEG_INF = INT_MIN + 0x7FFFFF


def _cparams(sem):
    return pltpu.CompilerParams(dimension_semantics=sem, vmem_limit_bytes=VMEM_LIMIT)


def _sigmoid(x):
    return 1.0 / (1.0 + jnp.exp(-x))


def _silu(x):
    return x * _sigmoid(x)


def _softplus(x):
    return jnp.maximum(x, 0.0) + jnp.log(1.0 + jnp.exp(-jnp.abs(x)))


def _dot(a, b, precision=None):
    return jnp.dot(a, b, precision=precision, preferred_element_type=F32)


def _dot_nt(a, b, precision=None):
    return lax.dot_general(a, b, (((1,), (1,)), ((), ())), precision=precision,
                           preferred_element_type=F32)


def _dot_tn(a, b, precision=None):
    return lax.dot_general(a, b, (((0,), (0,)), ((), ())), precision=precision,
                           preferred_element_type=F32)


def _bdot(a, b):
    return _dot(a.astype(BF16), b.astype(BF16))


def _bdot_nt(a, b):
    return _dot_nt(a.astype(BF16), b.astype(BF16))


def _bdot_tn(a, b):
    return _dot_tn(a.astype(BF16), b.astype(BF16))


def _split2(a):
    hi = a.astype(BF16)
    return hi, (a - hi.astype(F32)).astype(BF16)


def _dot3(a, b):
    ah, al = _split2(a)
    bh, bl = _split2(b)
    return _dot(ah, bh) + (_dot(ah, bl) + _dot(al, bh))


def _dot_mask(mask_b, x):
    x0 = x.astype(BF16)
    r1 = x - x0.astype(F32)
    x1 = r1.astype(BF16)
    x2 = (r1 - x1.astype(F32)).astype(BF16)
    return _dot(mask_b, x0) + (_dot(mask_b, x1) + _dot(mask_b, x2))


def _norm_mod(x, g, shift, scale):
    ms = jnp.mean(x * x, axis=-1, keepdims=True)
    y = x * lax.rsqrt(ms + EPS) * g
    return y * (1.0 + scale) + shift


def _ada_kernel(c_ref, w_ref, b_ref, o_ref):
    o_ref[0] = _dot(_silu(c_ref[...]), w_ref[0], HI) + b_ref[0]


def ada_modulation(c, ada_w, ada_b):
    depth, d, n = ada_w.shape
    b = c.shape[0]
    cp = jnp.zeros((SUBLANES, d), F32).at[:b].set(c)
    tn = 1536
    out = pl.pallas_call(
        _ada_kernel,
        grid=(depth, n // tn),
        in_specs=[pl.BlockSpec((SUBLANES, d), lambda l, j: (0, 0)),
                  pl.BlockSpec((1, d, tn), lambda l, j: (l, 0, j)),
                  pl.BlockSpec((1, 1, tn), lambda l, j: (l, 0, j))],
        out_specs=pl.BlockSpec((1, SUBLANES, tn), lambda l, j: (l, 0, j)),
        out_shape=jax.ShapeDtypeStruct((depth, SUBLANES, n), F32),
        compiler_params=_cparams(("arbitrary", "arbitrary")),
        name="ada_modulation",
    )(cp, ada_w, ada_b.reshape(depth, 1, n))
    return out[:, :b]


def _inproj_kernel(x_ref, g_ref, sh_ref, sc_ref, w_ref, o_ref):
    h = _norm_mod(x_ref[...], g_ref[...], sh_ref[0], sc_ref[0])
    o_ref[...] = _dot(h.astype(BF16), w_ref[...])


def in_projection(x, g, shift, scale, w, seq):
    t, d = x.shape
    n = w.shape[1]
    tm = 512
    per_b = seq // tm
    return pl.pallas_call(
        _inproj_kernel,
        grid=(t // tm,),
        in_specs=[pl.BlockSpec((tm, d), lambda i: (i, 0)),
                  pl.BlockSpec((1, d), lambda i: (0, 0)),
                  pl.BlockSpec((1, 1, d), lambda i: (i // per_b, 0, 0)),
                  pl.BlockSpec((1, 1, d), lambda i: (i // per_b, 0, 0)),
                  pl.BlockSpec((d, n), lambda i: (0, 0))],
        out_specs=pl.BlockSpec((tm, n), lambda i: (i, 0)),
        out_shape=jax.ShapeDtypeStruct((t, n), F32),
        compiler_params=_cparams(("arbitrary",)),
        name="in_projection",
    )(x, g, shift, scale, w)


def _outproj_kernel(*refs, n_in):
    o_refs = refs[:n_in]
    w_refs = refs[n_in:2 * n_in]
    x_ref, gt_ref, xo_ref = refs[2 * n_in:]
    acc = _dot(o_refs[0][...].astype(BF16), w_refs[0][...])
    for o_ref, w_ref in zip(o_refs[1:], w_refs[1:]):
        acc = acc + _dot(o_ref[...].astype(BF16), w_ref[...])
    xo_ref[...] = x_ref[...] + gt_ref[0] * acc


def out_projection(parts, weights, x, gate, seq):
    t, d = x.shape
    tm = 512
    per_b = seq // tm
    n_in = len(parts)
    in_specs = [pl.BlockSpec((tm, p.shape[1]), lambda i: (i, 0)) for p in parts]
    in_specs += [pl.BlockSpec(w.shape, lambda i: (0, 0)) for w in weights]
    in_specs += [pl.BlockSpec((tm, d), lambda i: (i, 0)),
                 pl.BlockSpec((1, 1, d), lambda i: (i // per_b, 0, 0))]
    return pl.pallas_call(
        functools.partial(_outproj_kernel, n_in=n_in),
        grid=(t // tm,),
        in_specs=in_specs,
        out_specs=pl.BlockSpec((tm, d), lambda i: (i, 0)),
        out_shape=jax.ShapeDtypeStruct((t, d), F32),
        compiler_params=_cparams(("arbitrary",)),
        name="out_projection",
    )(*parts, *weights, x, gate)


def _tri_masks(c):
    row = lax.broadcasted_iota(I32, (c, c), 0)
    col = lax.broadcasted_iota(I32, (c, c), 1)
    return row >= col, row > col, row == col


def _gdn_kernel(qkv_ref, za_ref, sm_ref, cw_ref, nega_ref, dtb_ref, on_ref, o_ref,
                tail_ref, state_ref, *, rb):
    c = GDN_CHUNK
    dk = GDN_DK

    @pl.when(pl.program_id(1) == 0)
    def _():
        tail_ref[...] = jnp.zeros_like(tail_ref)
        state_ref[...] = jnp.zeros_like(state_ref)

    x = qkv_ref[...]
    xx = jnp.concatenate([tail_ref[...], x], axis=0)
    tail_ref[...] = x[rb - SUBLANES:, :]
    cw = cw_ref[...]
    y = x * cw[CONV_WIDTH - 1:CONV_WIDTH, :]
    for i in range(1, CONV_WIDTH):
        y = y + pltpu.roll(xx, i, axis=0)[SUBLANES:, :] * cw[CONV_WIDTH - 1 - i:CONV_WIDTH - i, :]
    y = _silu(y)

    sm = sm_ref[...]
    beta_t = _sigmoid(sm)
    g_t = nega_ref[...] * _softplus(sm + dtb_ref[...])
    za = za_ref[...]
    hw = GDN_HEADS * dk
    nch = rb // c
    row = lax.broadcasted_iota(I32, (rb, rb), 0)
    col = lax.broadcasted_iota(I32, (rb, rb), 1)
    shift = int(np.log2(c))
    same = (row >> shift) == (col >> shift)
    incl = same & (row >= col)
    strict = same & (row > col)
    eye = (row == col).astype(F32)
    incl_b = jnp.where(incl, 1.0, 0.0).astype(BF16)
    gc_all = _dot_mask(incl_b, g_t)

    heads = range(GDN_HEADS)
    qs, ks, vs, bbs, gcs, decays = [], [], [], [], [], []
    for h in heads:
        lo = h * dk
        q = y[:, lo:lo + dk]
        k = y[:, hw + lo:hw + lo + dk]
        qs.append(q * lax.rsqrt(jnp.sum(q * q, axis=-1, keepdims=True) + EPS) * dk ** -0.5)
        ks.append(k * lax.rsqrt(jnp.sum(k * k, axis=-1, keepdims=True) + EPS))
        vs.append(y[:, 2 * hw + lo:2 * hw + lo + dk])
        bbs.append(jnp.broadcast_to(beta_t[:, SM_BA + h:SM_BA + h + 1], (rb, dk)))
        gc = jnp.broadcast_to(gc_all[:, SM_AA + h:SM_AA + h + 1], (rb, dk))
        gc_t = jnp.concatenate([gc[r:r + dk].T for r in range(0, rb, dk)], axis=1)[:1]
        gc_i = jnp.concatenate([gc] * (rb // dk), axis=1)
        decays.append(jnp.where(incl, jnp.exp(jnp.where(incl, gc_i - gc_t, 0.0)), 0.0))
        gcs.append(gc)
    kbetas = [ks[h] * bbs[h] for h in heads]
    ypows = [jnp.where(strict, -(_bdot_nt(kbetas[h], ks[h]) * decays[h]), 0.0) for h in heads]
    tmats = [eye + ypows[h] for h in heads]
    for _ in range(int(np.log2(c)) - 1):
        ypows = [_bdot(ypows[h], ypows[h]) for h in heads]
        tmats = [tmats[h] + _bdot(tmats[h], ypows[h]) for h in heads]
    egcs = [jnp.exp(gcs[h]) for h in heads]
    wus = [_bdot(tmats[h], jnp.concatenate([kbetas[h] * egcs[h], vs[h] * bbs[h]], axis=1))
           for h in heads]
    aqks = [_bdot_nt(qs[h], ks[h]) * decays[h] for h in heads]
    qgs = [qs[h] * egcs[h] for h in heads]
    sts = [state_ref[h] for h in heads]
    v_news = [[] for _ in heads]
    o_inter = [[] for _ in heads]
    for ci in range(nch):
        r0 = ci * c
        for h in heads:
            gl = gcs[h][r0 + c - 1:r0 + c, :]
            kd = ks[h][r0:r0 + c] * jnp.exp(gl - gcs[h][r0:r0 + c])
            ws = _bdot(jnp.concatenate([wus[h][r0:r0 + c, :dk], qgs[h][r0:r0 + c]], axis=0), sts[h])
            v_new = wus[h][r0:r0 + c, dk:] - ws[:c]
            o_inter[h].append(ws[c:])
            sts[h] = sts[h] * jnp.exp(gl) + _bdot_tn(kd, v_new)
            v_news[h].append(v_new)
    for h in heads:
        lo = h * dk
        state_ref[h] = sts[h]
        o = jnp.concatenate(o_inter[h], axis=0) + _bdot(aqks[h], jnp.concatenate(v_news[h], axis=0))
        o = o * lax.rsqrt(jnp.mean(o * o, axis=-1, keepdims=True) + EPS) * on_ref[...]
        o_ref[:, lo:lo + dk] = o * _silu(za[:, lo:lo + dk])


def gdn_mixer(proj, conv_w, a_log, dt_bias, out_norm, batch, seq):
    t = proj.shape[0]
    rb = 256
    nsb = seq // rb
    hw = GDN_HEADS * GDN_DK
    nega =jnp.zeros((1, LANES), F32).at[0, SM_AA:SM_AA + GDN_HEADS].set(-jnp.exp(a_log))
    dtb = jnp.zeros((1, LANES), F32).at[0, SM_AA:SM_AA + GDN_HEADS].set(dt_bias)
    row = lambda b, s: b * nsb + s
    return pl.pallas_call(
        functools.partial(_gdn_kernel, rb=rb),
        grid=(batch, nsb),
        in_specs=[pl.BlockSpec((rb, 3 * hw), lambda b, s: (row(b, s), 0)),
                  pl.BlockSpec((rb, hw), lambda b, s: (row(b, s), EV_ZA // hw)),
                  pl.BlockSpec((rb, LANES), lambda b, s: (row(b, s), EV_SM // LANES)),
                  pl.BlockSpec((CONV_WIDTH, 3 * hw), lambda b, s: (0, 0)),
                  pl.BlockSpec((1, LANES), lambda b, s: (0, 0)),
                  pl.BlockSpec((1, LANES), lambda b, s: (0, 0)),
                  pl.BlockSpec((1, GDN_DK), lambda b, s: (0, 0))],
        out_specs=pl.BlockSpec((rb, hw), lambda b, s: (row(b, s), 0)),
        out_shape=jax.ShapeDtypeStruct((t, hw), F32),
        scratch_shapes=[pltpu.VMEM((SUBLANES, 3 * hw), F32),
                        pltpu.VMEM((GDN_HEADS, GDN_DK, GDN_DK), F32)],
        compiler_params=_cparams(("arbitrary", "arbitrary")),
        name="gdn_mixer",
    )(proj, proj, proj, conv_w, nega, dtb, out_norm.reshape(1, GDN_DK))


def _gla_kernel(qk_ref, v_ref, r_ref, gl_ref, wg_ref, bg_ref, on_ref, o_ref, state_ref, *, rb):
    c = GLA_CHUNK
    dk, dv = GLA_DK, GLA_DV
    sb = GLA_SUB

    @pl.when(pl.program_id(1) == 0)
    def _():
        state_ref[...] = jnp.zeros_like(state_ref)

    z = _dot3(gl_ref[...], wg_ref[...]) + bg_ref[...]
    log_a = -_softplus(-z) * (1.0 / GLA_TAU)
    qk = qk_ref[...]
    vv = v_ref[...]
    rr = r_ref[...]
    incl, _, _ = _tri_masks(c)
    incl_b = incl.astype(F32).astype(BF16)
    hw = GLA_HEADS * dk
    sub_row = lax.broadcasted_iota(I32, (sb, dk), 0)
    att_lane = lax.broadcasted_iota(I32, (sb, c), 1)

    for ci in range(rb // c):
        r0 = ci * c
        b_all = _dot_mask(incl_b, log_a[r0:r0 + c])
        for h in range(GLA_HEADS):
            q = qk[r0:r0 + c, h * dk:(h + 1) * dk] * dk ** -0.5
            k = qk[r0:r0 + c, hw + h * dk:hw + (h + 1) * dk]
            v = vv[r0:r0 + c, h * dv:(h + 1) * dv]
            b = b_all[:, h * dk:(h + 1) * dk]
            st = state_ref[h]
            o = _bdot_nt(q * jnp.exp(b), st)
            att_rows = []
            for bi in range(c // sb):
                s0 = bi * sb
                qi = q[s0:s0 + sb]
                bi_ = b[s0:s0 + sb]
                ref = b[s0:s0 + 1]
                if bi > 0:
                    qt = qi * jnp.exp(bi_ - ref)
                    kt = k * jnp.exp(jnp.minimum(ref - b, 0.0))
                    att = jnp.where(att_lane < s0, _bdot_nt(qt, kt), 0.0)
                else:
                    att = jnp.zeros((sb, c), F32)
                for j in range(sb):
                    kj = k[s0 + j:s0 + j + 1]
                    bj = b[s0 + j:s0 + j + 1]
                    e = jnp.exp(jnp.where(sub_row >= j, bi_ - bj, NEG_INF))
                    col = jnp.sum(qi * kj * e, axis=-1, keepdims=True)
                    att = jnp.where(att_lane == s0 + j, col, att)
                att_rows.append(att)
            o = o + _bdot(jnp.concatenate(att_rows, axis=0), v)
            bl = b[c - 1:c]
            state_ref[h] = st * jnp.exp(bl) + _bdot_tn(v, k * jnp.exp(bl - b))
            o = o * lax.rsqrt(jnp.mean(o * o, axis=-1, keepdims=True) + EPS) * on_ref[...]
            o_ref[r0:r0 + c, h * dv:(h + 1) * dv] = o * _silu(rr[r0:r0 + c, h * dv:(h + 1) * dv])


def gla_mixer(proj, w_gate2, b_gate2, out_norm, batch, seq):
    t = proj.shape[0]
    rb = GLA_CHUNK
    nsb = seq // rb
    hw = GLA_HEADS * GLA_DK
    hv = GLA_HEADS * GLA_DV
    wg = jnp.zeros((LANES, hw), F32).at[:GLA_RANK].set(w_gate2)
    row = lambda b, s: b * nsb + s
    return pl.pallas_call(
        functools.partial(_gla_kernel, rb=rb),
        grid=(batch, nsb),
        in_specs=[pl.BlockSpec((rb, 2 * hw), lambda b, s: (row(b, s), 0)),
                  pl.BlockSpec((rb, hv), lambda b, s: (row(b, s), OD_V // hv)),
                  pl.BlockSpec((rb, hv), lambda b, s: (row(b, s), OD_R // hv)),
                  pl.BlockSpec((rb, LANES), lambda b, s: (row(b, s), OD_G // LANES)),
                  pl.BlockSpec((LANES, hw), lambda b, s: (0, 0)),
                  pl.BlockSpec((1, hw), lambda b, s: (0, 0)),
                  pl.BlockSpec((1, GLA_DV), lambda b, s: (0, 0))],
        out_specs=pl.BlockSpec((rb, hv), lambda b, s: (row(b, s), 0)),
        out_shape=jax.ShapeDtypeStruct((t, hv), F32),
        scratch_shapes=[pltpu.VMEM((GLA_HEADS, GLA_DV, GLA_DK), F32)],
        compiler_params=_cparams(("arbitrary", "arbitrary")),
        name="gla_mixer",
    )(proj, proj, proj, proj, wg, b_gate2.reshape(1, hw), out_norm.reshape(1, GLA_DV))


def _rope_table_kernel(pos_ref, inva_ref, invi_ref, ca_ref, sa_ref, ci_ref, si_ref):
    pos = pos_ref[...].astype(F32)
    lane = lax.broadcasted_iota(I32, (pos.shape[0], LANES), 1)
    ang_a = pos * inva_ref[...]
    ca_ref[...] = jnp.cos(ang_a)
    sa_ref[...] = jnp.where(lane < ATT_DH // 2, -1.0, 1.0) * jnp.sin(ang_a)
    ang_i = pos * invi_ref[...]
    ci_ref[...] = jnp.cos(ang_i)
    si_ref[...] = jnp.where(lane % IDX_DH < IDX_DH // 2, -1.0, 1.0) * jnp.sin(ang_i)


def rope_tables(positions):
    t = positions.size
    inv_a = 1.0 / (ROPE_THETA ** (jnp.arange(0, ATT_DH, 2, dtype=F32) / ATT_DH))
    inv_i = 1.0 / (ROPE_THETA ** (jnp.arange(0, IDX_DH, 2, dtype=F32) / IDX_DH))
    inv_a = jnp.tile(inv_a, 2).reshape(1, LANES)
    inv_i = jnp.tile(inv_i, 4).reshape(1, LANES)
    tm = 512
    spec = pl.BlockSpec((tm, LANES), lambda i: (i, 0))
    vec = pl.BlockSpec((1, LANES), lambda i: (0, 0))
    return pl.pallas_call(
        _rope_table_kernel,
        grid=(t // tm,),
        in_specs=[pl.BlockSpec((tm, 1), lambda i: (i, 0)), vec, vec],
        out_specs=[spec] * 4,
        out_shape=[jax.ShapeDtypeStruct((t, LANES), F32)] * 4,
        compiler_params=_cparams(("arbitrary",)),
        name="rope_tables",
    )(positions.reshape(t, 1), inv_a, inv_i)


def _dsa_prep_kernel(qb_ref, kb_ref, vb_ref, qi_ref, sm_ref, ca_ref, sa_ref, ci_ref, si_ref,
                     qo_ref, ko_ref, vt_ref, qio_ref, kio_ref, kn_ref):
    ca, sa, ci, si = ca_ref[...], sa_ref[...], ci_ref[...], si_ref[...]
    lane = lax.broadcasted_iota(I32, ca.shape, 1)
    first_half = lane % IDX_DH < IDX_DH // 2
    knorm = jnp.zeros((SUBLANES, LANES), F32)
    lane8 = lax.broadcasted_iota(I32, (SUBLANES, LANES), 1)

    def rope_att(tile):
        return tile * ca + pltpu.roll(tile, ATT_DH // 2, axis=1) * sa

    def rope_idx(tile):
        partner = jnp.where(first_half, pltpu.roll(tile, LANES - IDX_DH // 2, axis=1),
                            pltpu.roll(tile, IDX_DH // 2, axis=1))
        return tile * ci + partner * si

    qb, kb = qb_ref[...], kb_ref[...]
    for h in range(ATT_HEADS):
        sl = slice(h * ATT_DH, (h + 1) * ATT_DH)
        qo_ref[:, sl] = (rope_att(qb[:, sl]) * ATT_DH ** -0.5).astype(BF16)
        kr = rope_att(kb[:, sl]).astype(BF16)
        ko_ref[:, sl] = kr
        krf = kr.astype(F32)
        knorm = jnp.where(lane8 == h, jnp.max(jnp.sum(krf * krf, axis=1, keepdims=True)), knorm)
    kn_ref[0] = knorm
    vt_ref[0] = vb_ref[...].T.astype(BF16)
    qi = qi_ref[...]
    for p in range(IDX_HEADS * IDX_DH // LANES):
        sl = slice(p * LANES, (p + 1) * LANES)
        qio_ref[:, sl] = rope_idx(qi[:, sl]).astype(BF16)
    ki = rope_idx(sm_ref[...])
    kio_ref[...] = jnp.where(lane < IDX_DH, ki, pltpu.roll(ki, IDX_DH, axis=1)).astype(BF16)


def dsa_prep(proj, tables, seq):
    t = proj.shape[0]
    tm = min(DSA_KEY_CHUNK, seq)
    aw = ATT_HEADS * ATT_DH
    iw = IDX_HEADS * IDX_DH
    tab = pl.BlockSpec((tm, LANES), lambda i: (i, 0))
    return pl.pallas_call(
        _dsa_prep_kernel,
        grid=(t // tm,),
        in_specs=[pl.BlockSpec((tm, aw), lambda i: (i, EV_QB // aw)),
                  pl.BlockSpec((tm, aw), lambda i: (i, EV_KB // aw)),
                  pl.BlockSpec((tm, aw), lambda i: (i, EV_VB // aw)),
                  pl.BlockSpec((tm, iw), lambda i: (i, EV_QI // iw)),
                  pl.BlockSpec((tm, LANES), lambda i: (i, EV_SM // LANES)),
                  tab, tab, tab, tab],
        out_specs=[pl.BlockSpec((tm, aw), lambda i: (i, 0)),
                   pl.BlockSpec((tm, aw), lambda i: (i, 0)),
                   pl.BlockSpec((1, aw, tm), lambda i: (i, 0, 0)),
                   pl.BlockSpec((tm, iw), lambda i: (i, 0)),
                   pl.BlockSpec((tm, LANES), lambda i: (i, 0)),
                   pl.BlockSpec((1, SUBLANES, LANES), lambda i: (i, 0, 0))],
        out_shape=[jax.ShapeDtypeStruct((t, aw), BF16),
                   jax.ShapeDtypeStruct((t, aw), BF16),
                   jax.ShapeDtypeStruct((t // tm, aw, tm), BF16),
                   jax.ShapeDtypeStruct((t, iw), BF16),
                   jax.ShapeDtypeStruct((t, LANES), BF16),
                   jax.ShapeDtypeStruct((t // tm, SUBLANES, LANES), F32)],
        compiler_params=_cparams(("arbitrary",)),
        name="dsa_prep",
    )(proj, proj, proj, proj, proj, *tables)


def _fold_rows(x, op):
    parts = [x[r:r + SUBLANES] for r in range(0, x.shape[0], SUBLANES)]
    while len(parts) > 1:
        parts = [op(parts[j], parts[j + 1]) for j in range(0, len(parts), 2)]
    return parts[0]


def _dsa_kernel(qb_ref, qi_ref, sm_ref, kn_ref, k_ref, vt_ref, ki_ref, o_ref,
                key_ref, l_ref, acc_ref, *, kc, topk):
    nq = Q_BLOCK
    i = pl.program_id(1)
    nkc = (i * nq + nq + kc - 1) // kc
    q_pos = i * nq + lax.broadcasted_iota(I32, (1, nq), 1)
    key0 = lax.broadcasted_iota(I32, (kc, nq), 0)
    lane = lax.broadcasted_iota(I32, (nq, LANES), 1)
    wi_t = sm_ref[...].T * (IDX_HEADS * IDX_DH) ** -0.5
    qi = qi_ref[...]
    zero_b = jnp.zeros((nq, LANES), BF16)
    qi_heads = []
    for h in range(IDX_HEADS):
        pair = qi[:, (h // 2) * LANES:(h // 2 + 1) * LANES]
        own = (lane < IDX_DH) if h % 2 == 0 else (lane >= IDX_DH)
        qi_heads.append(jnp.where(own, pair, zero_b))

    def score_body(c, carry):
        start = pl.multiple_of(c * kc, kc)
        kic = ki_ref[0, pl.ds(start, kc), :]
        dots = [_dot_nt(kic, qi_heads[h]) for h in range(IDX_HEADS)]
        s = jnp.zeros((kc, nq), F32)
        for h in range(IDX_HEADS):
            s = s + wi_t[SM_WI + h:SM_WI + h + 1, :] * jnp.maximum(dots[h], 0.0)
        s = s + 0.0
        s = jnp.where(key0 <= q_pos - c * kc, s, NEG_INF)
        bits = pltpu.bitcast(s, I32)
        key_ref[c] = bits ^ ((bits >> 31) & 0x7FFFFFFF)
        return carry

    lax.fori_loop(0, nkc, score_body, 0)

    def count(pred):
        def body(c, acc):
            return acc + _fold_rows(jnp.where(pred(key_ref[c], c), 1, 0).astype(I32), jnp.add)
        acc = lax.fori_loop(0, nkc, body, jnp.zeros((SUBLANES, nq), I32))
        return jnp.sum(acc, axis=0, keepdims=True)

    thr = jnp.where(count(lambda key, c: key >= 0) >= topk, 0, INT_MIN).astype(I32)

    def bit_body(it, thr):
        cand = thr + (jnp.int32(1) << (30 - it))
        return jnp.where(count(lambda key, c: key >= cand) >= topk, cand, thr)

    thr = lax.fori_loop(0, 31, bit_body, thr)

    def gt_eq_body(c, accs):
        key = key_ref[c]
        gt = _fold_rows(jnp.where(key > thr, 1, 0).astype(I32), jnp.add)
        eq = _fold_rows(jnp.where(key == thr, 1, 0).astype(I32), jnp.add)
        return accs[0] + gt, accs[1] + eq

    zero8 = jnp.zeros((SUBLANES, nq), I32)
    gt8, eq8 = lax.fori_loop(0, nkc, gt_eq_body, (zero8, zero8))
    n_gt = jnp.sum(gt8, axis=0, keepdims=True)
    n_eq = jnp.sum(eq8, axis=0, keepdims=True)
    need = topk - n_gt
    big = jnp.int32(2 ** 30)

    def tie_search():
        def idx_body(it, x):
            cand = x + (jnp.int32(1) << (14 - it))
            below = count(lambda key, c: (key0 < jnp.where(key == thr, cand - c * kc, 0)))
            return jnp.where(below < need, cand, x)
        return lax.fori_loop(0, 15, idx_body, jnp.zeros((1, nq), I32))

    any_tie = jnp.max(jnp.where(n_eq > need, 1, 0)) > 0
    last = lax.cond(any_tie, tie_search, lambda: jnp.full((1, nq), big, I32))

    last = jnp.where(thr == KEY_NEG_INF, -1, last)
    zero_f = jnp.float32(0.0)
    ninf_f = jnp.float32(NEG_INF)
    qb = qb_ref[...]

    def select_bias(c):
        key = key_ref[c]
        tie_ok = jnp.where(key0 <= last - c * kc, zero_f, ninf_f)
        return jnp.where(key > thr, zero_f, jnp.where(key == thr, tie_ok, ninf_f))

    def chunk_logits(c, h, bias):
        start = pl.multiple_of(c * kc, kc)
        kch = k_ref[0, pl.ds(start, kc), h * ATT_DH:(h + 1) * ATT_DH]
        return _dot_nt(kch, qb[:, h * ATT_DH:(h + 1) * ATT_DH]) + bias

    def attend(shifts):
        l_ref[...] = jnp.zeros(l_ref.shape, F32)
        acc_ref[...] = jnp.zeros(acc_ref.shape, F32)

        def body(c, carry):
            bias = select_bias(c)
            heads = range(ATT_HEADS)
            logits = [chunk_logits(c, h, bias) for h in heads]
            ps = [jnp.exp(logits[h] - shifts[h]) for h in heads]
            for h in heads:
                l_ref[h] = l_ref[h] + _fold_rows(ps[h], jnp.add)
                vth = vt_ref[0, c, h * ATT_DH:(h + 1) * ATT_DH, :]
                acc_ref[h] = acc_ref[h] + _dot(vth, ps[h].astype(BF16))
            return carry

        lax.fori_loop(0, nkc, body, 0)
        return [jnp.sum(l_ref[h], axis=0, keepdims=True) for h in range(ATT_HEADS)]

    ones_b = jnp.ones((SUBLANES, ATT_DH), BF16)
    bounds = []
    for h in range(ATT_HEADS):
        qh = qb[:, h * ATT_DH:(h + 1) * ATT_DH].astype(F32)
        qsq = _dot_nt(ones_b, (qh * qh).astype(BF16))[0:1, :]
        bounds.append(jnp.sqrt(qsq * kn_ref[0][0:1, h:h + 1]) * 1.02)
    sums = attend(bounds)

    def write_out(sums):
        for h in range(ATT_HEADS):
            o_ref[:, h * ATT_DH:(h + 1) * ATT_DH] = (acc_ref[h] / sums[h]).T

    write_out(sums)
    lowest = jnp.minimum(jnp.minimum(sums[0], sums[1]), jnp.minimum(sums[2], sums[3]))

    @pl.when(jnp.min(lowest) <= 0.0)
    def _():
        l_ref[...] = jnp.full(l_ref.shape, NEG_INF, F32)

        def max_body(c, carry):
            bias = select_bias(c)
            for h in range(ATT_HEADS):
                l_ref[h] = jnp.maximum(l_ref[h], _fold_rows(chunk_logits(c, h, bias), jnp.maximum))
            return carry

        lax.fori_loop(0, nkc, max_body, 0)
        exact = [jnp.max(l_ref[h], axis=0, keepdims=True) for h in range(ATT_HEADS)]
        write_out(attend(exact))


def dsa_attention(proj, qb, kb, vt, qi, ki, knorm, batch, seq):
    t = proj.shape[0]
    aw = ATT_HEADS * ATT_DH
    iw = IDX_HEADS * IDX_DH
    kc = min(DSA_KEY_CHUNK, seq)
    nkc = seq // kc
    topk = min(TOPK_MAX, seq // 4)
    nq = seq // Q_BLOCK
    row = lambda b, i: b * nq + i
    kb3, ki3 = (a.reshape(batch, seq, a.shape[1]) for a in (kb, ki))
    vt4 = vt.reshape(batch, nkc, aw, kc)
    kmax = jnp.max(knorm.reshape(batch, nkc, SUBLANES, LANES), axis=1)
    resident = lambda shape: pl.BlockSpec((1,) + shape, lambda b, i: (b,) + (0,) * len(shape),
                                          pipeline_mode=pl.Buffered(1))
    return pl.pallas_call(
        functools.partial(_dsa_kernel, kc=kc, topk=topk),
        grid=(batch, nq),
        in_specs=[pl.BlockSpec((Q_BLOCK, aw), lambda b, i: (row(b, i), 0)),
                  pl.BlockSpec((Q_BLOCK, iw), lambda b, i: (row(b, i), 0)),
                  pl.BlockSpec((Q_BLOCK, LANES), lambda b, i: (row(b, i), EV_SM // LANES)),
                  pl.BlockSpec((1, SUBLANES, LANES), lambda b, i: (b, 0, 0)),
                  resident((seq, aw)), resident((nkc, aw, kc)), resident((seq, LANES))],
        out_specs=pl.BlockSpec((Q_BLOCK, aw), lambda b, i: (row(b, i), 0)),
        out_shape=jax.ShapeDtypeStruct((t, aw), F32),
        scratch_shapes=[pltpu.VMEM((nkc, kc, Q_BLOCK), I32),
                        pltpu.VMEM((ATT_HEADS, SUBLANES, Q_BLOCK), F32),
                        pltpu.VMEM((ATT_HEADS, ATT_DH, Q_BLOCK), F32)],
        compiler_params=_cparams(("arbitrary", "arbitrary")),
        name="dsa_attention",
    )(qb, qi, proj, kmax, kb3, vt4, ki3)


def _router_kernel(x_ref, g_ref, sh_ref, sc_ref, wr_ref, br_ref, h_ref, route_ref, cnt_ref,
                   carry_ref):
    @pl.when(pl.program_id(0) == 0)
    def _():
        carry_ref[...] = jnp.zeros_like(carry_ref)

    h = _norm_mod(x_ref[...], g_ref[...], sh_ref[0], sc_ref[0])
    h_ref[...] = h
    tm = h.shape[0]
    logits = _dot(h, wr_ref[...], HI) + br_ref[...]
    lane = lax.broadcasted_iota(I32, (tm, LANES), 1)
    far = jnp.int32(LANES)

    def first_max(vals):
        m = jnp.max(vals, axis=1, keepdims=True)
        return m, jnp.min(jnp.where(vals == m, lane, far), axis=1, keepdims=True)

    is_group = lane < N_GROUPS
    gmax, gsel = first_max(jnp.where(is_group, logits, NEG_INF))
    gsum = jnp.sum(jnp.where(is_group, jnp.exp(logits - gmax), 0.0), axis=1, keepdims=True)
    lo = N_GROUPS + EXPERTS_PER_GROUP * gsel
    el = jnp.where((lane >= lo) & (lane < lo + EXPERTS_PER_GROUP), logits, NEG_INF)
    m1, i1 = first_max(el)
    m2, i2 = first_max(jnp.where(lane == i1, NEG_INF, el))
    e2 = jnp.exp(m2 - m1)
    gate1 = 1.0 / (gsum * (1.0 + e2))
    gate2 = e2 * gate1
    eid1 = i1 - N_GROUPS
    eid2 = i2 - N_GROUPS
    oh1 = lane == eid1
    oh2 = lane == eid2
    onehot = jnp.where(oh1 | oh2, 1.0, 0.0)
    r = lax.broadcasted_iota(I32, (tm, tm), 0)
    cidx = lax.broadcasted_iota(I32, (tm, tm), 1)
    below = jnp.where(r > cidx, 1.0, 0.0).astype(BF16)
    cum = _dot(below, onehot.astype(BF16)) + carry_ref[0:1, :]
    rank1 = jnp.sum(jnp.where(oh1, cum, 0.0), axis=1, keepdims=True)
    rank2 = jnp.sum(jnp.where(oh2, cum, 0.0), axis=1, keepdims=True)
    carry_ref[...] = carry_ref[...] + jnp.sum(onehot, axis=0, keepdims=True)
    cnt_ref[...] = carry_ref[...]
    out = jnp.zeros((tm, LANES), F32)
    for j, val in enumerate((eid1.astype(F32), eid2.astype(F32), gate1, gate2, rank1, rank2)):
        out = jnp.where(lane == j, val, out)
    route_ref[...] = out


def moe_router(x, g, shift, scale, wr, br, seq):
    t, d = x.shape
    tm = 512
    per_b = seq // tm
    return pl.pallas_call(
        _router_kernel,
        grid=(t // tm,),
        in_specs=[pl.BlockSpec((tm, d), lambda i: (i, 0)),
                  pl.BlockSpec((1, d), lambda i: (0, 0)),
                  pl.BlockSpec((1, 1, d), lambda i: (i // per_b, 0, 0)),
                  pl.BlockSpec((1, 1, d), lambda i: (i // per_b, 0, 0)),
                  pl.BlockSpec((d, LANES), lambda i: (0, 0)),
                  pl.BlockSpec((1, LANES), lambda i: (0, 0))],
        out_specs=[pl.BlockSpec((tm, d), lambda i: (i, 0)),
                   pl.BlockSpec((tm, LANES), lambda i: (i, 0)),
                   pl.BlockSpec((SUBLANES, LANES), lambda i: (0, 0))],
        out_shape=[jax.ShapeDtypeStruct((t, d), F32),
                   jax.ShapeDtypeStruct((t, LANES), F32),
                   jax.ShapeDtypeStruct((SUBLANES, LANES), F32)],
        scratch_shapes=[pltpu.VMEM((SUBLANES, LANES), F32)],
        compiler_params=_cparams(("arbitrary",)),
        name="moe_router",
    )(x, g, shift, scale, wr, br)


def _zero_kernel(o_ref):
    o_ref[...] = jnp.zeros_like(o_ref)


def zero_rows(rows, d):
    tm = 1024
    return pl.pallas_call(
        _zero_kernel,
        grid=(rows // tm,),
        out_specs=pl.BlockSpec((tm, d), lambda i: (i, 0)),
        out_shape=jax.ShapeDtypeStruct((rows, d), F32),
        compiler_params=_cparams(("arbitrary",)),
        name="zero_rows",
    )()


def _dispatch_kernel(dest_ref, seg_ref, h_ref, xs_in_ref, xs_ref, inv_ref, sem, *, tm):
    del xs_in_ref
    i = pl.program_id(0)
    base = i * tm * 2

    rows = MOE_ROWS
    n_assign = dest_ref.shape[0]

    @pl.when(i == 0)
    def _():
        def mark(p, carry):
            blk = p // rows
            spare = ((blk + MOE_OUT_BUFS - 1) % MOE_OUT_BUFS) * rows + (p - blk * rows)
            inv_ref[p] = n_assign + spare
            return carry
        lax.fori_loop(0, rows, mark, 0)
        for e in range(N_EXPERTS):
            lax.fori_loop(rows + seg_ref[e], rows + seg_ref[N_EXPERTS + e], mark, 0)
        lax.fori_loop(rows + seg_ref[2 * N_EXPERTS - 1], inv_ref.shape[0], mark, 0)

    def row_copy(r, d):
        return pltpu.make_async_copy(h_ref.at[pl.ds(r, 1)], xs_ref.at[pl.ds(d, 1)], sem)

    group = 4
    n_tok = n_assign // 2

    def start(g, carry):
        r0 = g * group
        ds = [[dest_ref[base + 2 * (r0 + k) + j] for j in range(2)] for k in range(group)]
        for k in range(group):
            for j in range(2):
                row_copy(r0 + k, ds[k][j]).start()
        for k in range(group):
            for j in range(2):
                inv_ref[rows + ds[k][j]] = j * n_tok + i * tm + r0 + k
        return carry

    def wait(r, carry):
        row_copy(0, 0).wait()
        row_copy(0, 0).wait()
        return carry

    lax.fori_loop(0, tm // group, start, 0)
    lax.fori_loop(0, tm, wait, 0)


def moe_dispatch(dest, seg, h, xs_zero):
    t, d = h.shape
    tm = 256
    return pl.pallas_call(
        functools.partial(_dispatch_kernel, tm=tm),
        grid_spec=pltpu.PrefetchScalarGridSpec(
            num_scalar_prefetch=2,
            grid=(t // tm,),
            in_specs=[pl.BlockSpec((tm, d), lambda i, dest, seg: (i, 0)),
                      pl.BlockSpec(memory_space=pl.ANY)],
            out_specs=[pl.BlockSpec(memory_space=pl.ANY),
                       pl.BlockSpec(memory_space=pltpu.SMEM)],
            scratch_shapes=[pltpu.SemaphoreType.DMA(())]),
        out_shape=[jax.ShapeDtypeStruct(xs_zero.shape, F32),
                   jax.ShapeDtypeStruct((xs_zero.shape[0] + MOE_ROWS,), I32)],
        input_output_aliases={3: 0},
        compiler_params=_cparams(("arbitrary",)),
        name="moe_dispatch",
    )(dest, seg, h, xs_zero)


MOE_OUT_BUFS = 3


def _expert_kernel(be_ref, inv_ref, xs_ref, w1_ref, w3_ref, w2_ref, y_ref,
                   w1b, w3b, w2b, ybuf, sems, *, n_assign):
    i = pl.program_id(0)
    last = pl.num_programs(0) - 1
    rows = ybuf.shape[1]
    slot = i % MOE_OUT_BUFS
    prev_slot = (i + MOE_OUT_BUFS - 1) % MOE_OUT_BUFS
    prev = be_ref[jnp.maximum(i - 1, 0)]

    def row_copy(s, r, dst):
        return pltpu.make_async_copy(ybuf.at[s, pl.ds(r, 1)], y_ref.at[pl.ds(dst, 1)], sems.at[s])

    def send(s, block):
        for r in range(rows):
            row_copy(s, r, inv_ref[(block + 1) * rows + r]).start()

    def drain(s):
        for r in range(rows):
            row_copy(s, 0, 0).wait()

    @pl.when(i == 0)
    def _():
        ybuf[prev_slot] = jnp.zeros((rows, ybuf.shape[2]), F32)
        for s in range(MOE_OUT_BUFS):
            if s != MOE_OUT_BUFS - 1:
                for r in range(rows):
                    row_copy(prev_slot, r, n_assign + s * rows + r).start()
                drain(prev_slot)

    @pl.when(i >= MOE_OUT_BUFS - 1)
    def _():
        drain(slot)

    @pl.when((i == 0) | (be_ref[i] != prev))
    def _():
        w1b[...] = w1_ref[0, 0].astype(BF16)
        w3b[...] = w3_ref[0, 0].astype(BF16)
        w2b[...] = w2_ref[0, 0].astype(BF16)

    send(prev_slot, i - 1)
    x = xs_ref[...].astype(BF16)
    a = _dot(x, w1b[...])
    hid = _silu(a) * _dot(x, w3b[...])
    ybuf[slot] = _dot(hid.astype(BF16), w2b[...])

    @pl.when(i == last)
    def _():
        send(slot, i)
        for s in range(MOE_OUT_BUFS):
            drain(s)


def moe_experts(blk_exp, inv, xs, w1, w3, w2, layer, t):
    p, d = xs.shape
    de = w1.shape[3]
    rows = MOE_ROWS
    wmap = lambda i, be, inv: (layer, be[i], 0, 0)
    return pl.pallas_call(
        functools.partial(_expert_kernel, n_assign=2 * t),
        grid_spec=pltpu.PrefetchScalarGridSpec(
            num_scalar_prefetch=2,
            grid=(p // rows,),
            in_specs=[pl.BlockSpec((rows, d), lambda i, be, inv: (i, 0)),
                      pl.BlockSpec((1, 1, d, de), wmap),
                      pl.BlockSpec((1, 1, d, de), wmap),
                      pl.BlockSpec((1, 1, de, d), wmap)],
            out_specs=pl.BlockSpec(memory_space=pl.ANY),
            scratch_shapes=[pltpu.VMEM((d, de), BF16), pltpu.VMEM((d, de), BF16),
                            pltpu.VMEM((de, d), BF16), pltpu.VMEM((MOE_OUT_BUFS, rows, d), F32),
                            pltpu.SemaphoreType.DMA((MOE_OUT_BUFS,))]),
        out_shape=jax.ShapeDtypeStruct((2 * t + MOE_OUT_BUFS * rows, d), F32),
        compiler_params=_cparams(("arbitrary",)),
        name="moe_experts",
    )(blk_exp, inv, xs, w1, w3, w2)


def _combine_kernel(y1_ref, y2_ref, route_ref, x_ref, gt_ref, *rest, final):
    route = route_ref[...]
    y = route[:, 2:3] * y1_ref[...] + route[:, 3:4] * y2_ref[...]
    out = x_ref[...] + gt_ref[0] * y
    if final:
        fn_ref, o_ref = rest
        out = out * lax.rsqrt(jnp.mean(out * out, axis=-1, keepdims=True) + EPS) * fn_ref[...]
    else:
        (o_ref,) = rest
    o_ref[...] = out


def moe_combine(y2, route, x, gate, seq, final_gain=None):
    t, d = x.shape
    tm = 512
    per_b = seq // tm
    final = final_gain is not None
    nt = t // tm
    in_specs = [pl.BlockSpec((tm, d), lambda i: (i, 0)),
                pl.BlockSpec((tm, d), lambda i: (nt + i, 0)),
                pl.BlockSpec((tm, LANES), lambda i: (i, 0)),
                pl.BlockSpec((tm, d), lambda i: (i, 0)),
                pl.BlockSpec((1, 1, d), lambda i: (i // per_b, 0, 0))]
    args = [y2, y2, route, x, gate]
    if final:
        in_specs.append(pl.BlockSpec((1, d), lambda i: (0, 0)))
        args.append(final_gain)
    return pl.pallas_call(
        functools.partial(_combine_kernel, final=final),
        grid=(t // tm,),
        in_specs=in_specs,
        out_specs=pl.BlockSpec((tm, d), lambda i: (i, 0)),
        out_shape=jax.ShapeDtypeStruct((t, d), F32),
        compiler_params=_cparams(("arbitrary",)),
        name="moe_combine",
    )(*args)


def hier_moe_layer(x, g, shift, scale, gate, wg, bg, we, be, w1, w3, w2, layer, seq,
                   final_gain=None):
    t, d = x.shape
    wr = jnp.zeros((d, LANES), F32)
    wr = wr.at[:, :N_GROUPS].set(wg)
    wr = wr.at[:, N_GROUPS:N_GROUPS + N_EXPERTS].set(we.transpose(1, 0, 2).reshape(d, N_EXPERTS))
    br = jnp.zeros((1, LANES), F32)
    br = br.at[0, :N_GROUPS].set(bg).at[0, N_GROUPS:N_GROUPS + N_EXPERTS].set(be.reshape(-1))
    h, route, cnt = moe_router(x, g, shift, scale, wr, br, seq)
    counts = cnt[0, :N_EXPERTS].astype(I32)
    padded = ((counts + MOE_ROWS - 1) // MOE_ROWS) * MOE_ROWS
    pad_end = jnp.cumsum(padded)
    pad_start = pad_end - padded
    eid = route[:, 0:2].astype(I32)
    rank = route[:, 4:6].astype(I32)
    experts = jnp.arange(N_EXPERTS, dtype=I32)
    start_of = jnp.sum(jnp.where(eid[:, :, None] == experts, pad_start, 0), axis=-1)
    dest = (start_of + rank).reshape(-1)
    p_rows = 2 * t + N_EXPERTS * MOE_ROWS
    nblk = p_rows // MOE_ROWS
    blk_row = jnp.arange(nblk, dtype=I32)[:, None] * MOE_ROWS
    blk_exp = jnp.minimum(jnp.sum((pad_end[None, :] <= blk_row).astype(I32), axis=1),
                          N_EXPERTS - 1)
    seg = jnp.concatenate([pad_start + counts, pad_end]).astype(I32)
    xs, inv = moe_dispatch(dest, seg, h, zero_rows(p_rows, d))
    y2 = moe_experts(blk_exp, inv, xs, w1, w3, w2, layer, t)
    return moe_combine(y2, route, x, gate, seq, final_gain)


def _even_weight(w_in):
    d = w_in.shape[0]
    sizes = [512, 512, 512, 512, 4, 4, 512, 512, 512, 256, 64, 4]
    offs = np.concatenate([[0], np.cumsum(sizes)])
    qa, ka, va, za, ba, aa, qb, kb, vb, qi, ki, wi = (w_in[:, offs[j]:offs[j + 1]]
                                                      for j in range(len(sizes)))
    pad = jnp.zeros((d, EV_N - EV_SM - 76), w_in.dtype)
    return jnp.concatenate([qa, ka, va, za, qb, kb, vb, qi, ki, ba, aa, wi, pad], axis=1)


def kernel(x, c, positions, ada_w, ada_b, norm_mix, norm_ffn, even_w_in, gdn_conv_w, gdn_a_log, gdn_dt_bias, gdn_out_norm, even_w_out, gla_w_in, gla_w_gate2, gla_b_gate2, gla_out_norm, gla_w_out, router_group_w, router_group_b, router_exp_w, router_exp_b, exp_w1, exp_w3, exp_w2, final_norm):
    batch, seq, d = x.shape
    depth = ada_w.shape[0]
    t = batch * seq
    xt = x.reshape(t, d)
    mod = ada_modulation(c, ada_w, ada_b)
    tables = rope_tables(positions)
    for layer in range(depth):
        sh_m, sc_m, gt_m, sh_f, sc_f, gt_f = (mod[layer, :, j * d:(j + 1) * d].reshape(batch, 1, d)
                                              for j in range(6))
        i = layer // 2
        g_mix = norm_mix[layer].reshape(1, d)
        if layer % 2 == 0:
            w_in = _even_weight(even_w_in[i]).astype(BF16)
            proj = in_projection(xt, g_mix, sh_m, sc_m, w_in, seq)
            oa = gdn_mixer(proj, gdn_conv_w[i], gdn_a_log[i], gdn_dt_bias[i], gdn_out_norm[i],
                           batch, seq)
            qb, kb, vt, qi, ki, knorm = dsa_prep(proj, tables, seq)
            ob = dsa_attention(proj, qb, kb, vt, qi, ki, knorm, batch, seq)
            w_out = even_w_out[i].astype(BF16)
            half = GDN_HEADS * GDN_DK
            xt = out_projection([oa, ob], [w_out[:half], w_out[half:]], xt, gt_m, seq)
        else:
            w_in = jnp.pad(gla_w_in[i], ((0, 0), (0, OD_N - gla_w_in.shape[2]))).astype(BF16)
            proj = in_projection(xt, g_mix, sh_m, sc_m, w_in, seq)
            o = gla_mixer(proj, gla_w_gate2[i], gla_b_gate2[i], gla_out_norm[i], batch, seq)
            xt = out_projection([o], [gla_w_out[i].astype(BF16)], xt, gt_m, seq)
        xt = hier_moe_layer(xt, norm_ffn[layer].reshape(1, d), sh_f, sc_f, gt_f,
                            router_group_w[layer], router_group_b[layer],
                            router_exp_w[layer], router_exp_b[layer],
                            exp_w1, exp_w3, exp_w2, layer, seq,
                            final_norm.reshape(1, d) if layer == depth - 1 else None)
    return xt.reshape(batch, seq, d)
```

```python
import functools

import jax
import jax.numpy as jnp
import numpy as np
from jax import lax
from jax.experimental import pallas as pl
from jax.experimental.pallas import tpu as pltpu

F32 = jnp.float32
BF16 = jnp.bfloat16
I32 = jnp.int32
HI = lax.Precision.HIGHEST

LANES = 128
SUBLANES = 8
VMEM_LIMIT = 56 * 1024 * 1024

EPS = 1e-6
ROPE_THETA = 10000.0
GDN_HEADS, GDN_DK = 4, 128
CONV_WIDTH = 4
GDN_CHUNK = 64
ATT_HEADS, ATT_DH = 4, 128
IDX_HEADS, IDX_DH = 4, 64
TOPK_MAX = 256
Q_BLOCK = 128
DSA_KEY_CHUNK = 512
GLA_HEADS, GLA_DK, GLA_DV = 4, 128, 256
GLA_RANK = 16
GLA_TAU = 16.0
GLA_CHUNK = 64
GLA_SUB = 16
N_GROUPS, EXPERTS_PER_GROUP = 4, 8
N_EXPERTS = N_GROUPS * EXPERTS_PER_GROUP
MOE_ROWS = 256

EV_QA, EV_KA, EV_VA, EV_ZA = 0, 512, 1024, 1536
EV_QB, EV_KB, EV_VB, EV_QI, EV_SM = 2048, 2560, 3072, 3584, 3840
EV_N = 3968
SM_KI, SM_BA, SM_AA, SM_WI = 0, 64, 68, 72
OD_Q, OD_K, OD_V, OD_R, OD_G = 0, 512, 1024, 2048, 3072
OD_N = 3200

NEG_INF = float("-inf")
INT_MIN = -(2 ** 31)
KEY_NEG_INF = INT_MIN + 0x7FFFFF


def _cparams(sem):
    return pltpu.CompilerParams(dimension_semantics=sem, vmem_limit_bytes=VMEM_LIMIT)


def _sigmoid(x):
    return 1.0 / (1.0 + jnp.exp(-x))


def _silu(x):
    return x * _sigmoid(x)


def _softplus(x):
    return jnp.maximum(x, 0.0) + jnp.log(1.0 + jnp.exp(-jnp.abs(x)))


def _dot(a, b, precision=None):
    return jnp.dot(a, b, precision=precision, preferred_element_type=F32)


def _dot_nt(a, b, precision=None):
    return lax.dot_general(a, b, (((1,), (1,)), ((), ())), precision=precision,
                           preferred_element_type=F32)


def _dot_tn(a, b, precision=None):
    return lax.dot_general(a, b, (((0,), (0,)), ((), ())), precision=precision,
                           preferred_element_type=F32)


def _bdot(a, b):
    return _dot(a.astype(BF16), b.astype(BF16))


def _bdot_nt(a, b):
    return _dot_nt(a.astype(BF16), b.astype(BF16))


def _bdot_tn(a, b):
    return _dot_tn(a.astype(BF16), b.astype(BF16))


def _split2(a):
    hi = a.astype(BF16)
    return hi, (a - hi.astype(F32)).astype(BF16)


def _dot3(a, b):
    ah, al = _split2(a)
    bh, bl = _split2(b)
    return _dot(ah, bh) + (_dot(ah, bl) + _dot(al, bh))


def _dot_mask(mask_b, x):
    x0 = x.astype(BF16)
    r1 = x - x0.astype(F32)
    x1 = r1.astype(BF16)
    x2 = (r1 - x1.astype(F32)).astype(BF16)
    return _dot(mask_b, x0) + (_dot(mask_b, x1) + _dot(mask_b, x2))


def _norm_mod(x, g, shift, scale):
    ms = jnp.mean(x * x, axis=-1, keepdims=True)
    y = x * lax.rsqrt(ms + EPS) * g
    return y * (1.0 + scale) + shift


def _ada_kernel(c_ref, w_ref, b_ref, o_ref):
    o_ref[0] = _dot(_silu(c_ref[...]), w_ref[0], HI) + b_ref[0]


def ada_modulation(c, ada_w, ada_b):
    depth, d, n = ada_w.shape
    b = c.shape[0]
    cp = jnp.zeros((SUBLANES, d), F32).at[:b].set(c)
    tn = 1536
    out = pl.pallas_call(
        _ada_kernel,
        grid=(depth, n // tn),
        in_specs=[pl.BlockSpec((SUBLANES, d), lambda l, j: (0, 0)),
                  pl.BlockSpec((1, d, tn), lambda l, j: (l, 0, j)),
                  pl.BlockSpec((1, 1, tn), lambda l, j: (l, 0, j))],
        out_specs=pl.BlockSpec((1, SUBLANES, tn), lambda l, j: (l, 0, j)),
        out_shape=jax.ShapeDtypeStruct((depth, SUBLANES, n), F32),
        compiler_params=_cparams(("arbitrary", "arbitrary")),
        name="ada_modulation",
    )(cp, ada_w, ada_b.reshape(depth, 1, n))
    return out[:, :b]


def _inproj_kernel(x_ref, g_ref, sh_ref, sc_ref, w_ref, o_ref):
    h = _norm_mod(x_ref[...], g_ref[...], sh_ref[0], sc_ref[0])
    o_ref[...] = _dot(h.astype(BF16), w_ref[...])


def in_projection(x, g, shift, scale, w, seq):
    t, d = x.shape
    n = w.shape[1]
    tm = 512
    per_b = seq // tm
    return pl.pallas_call(
        _inproj_kernel,
        grid=(t // tm,),
        in_specs=[pl.BlockSpec((tm, d), lambda i: (i, 0)),
                  pl.BlockSpec((1, d), lambda i: (0, 0)),
                  pl.BlockSpec((1, 1, d), lambda i: (i // per_b, 0, 0)),
                  pl.BlockSpec((1, 1, d), lambda i: (i // per_b, 0, 0)),
                  pl.BlockSpec((d, n), lambda i: (0, 0))],
        out_specs=pl.BlockSpec((tm, n), lambda i: (i, 0)),
        out_shape=jax.ShapeDtypeStruct((t, n), F32),
        compiler_params=_cparams(("arbitrary",)),
        name="in_projection",
    )(x, g, shift, scale, w)


def _outproj_kernel(*refs, n_in):
    o_refs = refs[:n_in]
    w_refs = refs[n_in:2 * n_in]
    x_ref, gt_ref, xo_ref = refs[2 * n_in:]
    acc = _dot(o_refs[0][...].astype(BF16), w_refs[0][...])
    for o_ref, w_ref in zip(o_refs[1:], w_refs[1:]):
        acc = acc + _dot(o_ref[...].astype(BF16), w_ref[...])
    xo_ref[...] = x_ref[...] + gt_ref[0] * acc


def out_projection(parts, weights, x, gate, seq):
    t, d = x.shape
    tm = 512
    per_b = seq // tm
    n_in = len(parts)
    in_specs = [pl.BlockSpec((tm, p.shape[1]), lambda i: (i, 0)) for p in parts]
    in_specs += [pl.BlockSpec(w.shape, lambda i: (0, 0)) for w in weights]
    in_specs += [pl.BlockSpec((tm, d), lambda i: (i, 0)),
                 pl.BlockSpec((1, 1, d), lambda i: (i // per_b, 0, 0))]
    return pl.pallas_call(
        functools.partial(_outproj_kernel, n_in=n_in),
        grid=(t // tm,),
        in_specs=in_specs,
        out_specs=pl.BlockSpec((tm, d), lambda i: (i, 0)),
        out_shape=jax.ShapeDtypeStruct((t, d), F32),
        compiler_params=_cparams(("arbitrary",)),
        name="out_projection",
    )(*parts, *weights, x, gate)


def _tri_masks(c):
    row = lax.broadcasted_iota(I32, (c, c), 0)
    col = lax.broadcasted_iota(I32, (c, c), 1)
    return row >= col, row > col, row == col


def _gdn_kernel(qkv_ref, za_ref, sm_ref, cw_ref, nega_ref, dtb_ref, on_ref, o_ref,
                tail_ref, state_ref, *, rb):
    c = GDN_CHUNK
    dk = GDN_DK

    @pl.when(pl.program_id(1) == 0)
    def _():
        tail_ref[...] = jnp.zeros_like(tail_ref)
        state_ref[...] = jnp.zeros_like(state_ref)

    x = qkv_ref[...]
    xx = jnp.concatenate([tail_ref[...], x], axis=0)
    tail_ref[...] = x[rb - SUBLANES:, :]
    cw = cw_ref[...]
    y = x * cw[CONV_WIDTH - 1:CONV_WIDTH, :]
    for i in range(1, CONV_WIDTH):
        y = y + pltpu.roll(xx, i, axis=0)[SUBLANES:, :] * cw[CONV_WIDTH - 1 - i:CONV_WIDTH - i, :]
    y = _silu(y)

    sm = sm_ref[...]
    beta_t = _sigmoid(sm)
    g_t = nega_ref[...] * _softplus(sm + dtb_ref[...])
    za = za_ref[...]
    hw = GDN_HEADS * dk
    nch = rb // c
    row = lax.broadcasted_iota(I32, (rb, rb), 0)
    col = lax.broadcasted_iota(I32, (rb, rb), 1)
    shift = int(np.log2(c))
    same = (row >> shift) == (col >> shift)
    incl = same & (row >= col)
    strict = same & (row > col)
    eye = (row == col).astype(F32)
    incl_b = jnp.where(incl, 1.0, 0.0).astype(BF16)
    gc_all = _dot_mask(incl_b, g_t)

    heads = range(GDN_HEADS)
    qs, ks, vs, bbs, gcs, decays = [], [], [], [], [], []
    for h in heads:
        lo = h * dk
        q = y[:, lo:lo + dk]
        k = y[:, hw + lo:hw + lo + dk]
        qs.append(q * lax.rsqrt(jnp.sum(q * q, axis=-1, keepdims=True) + EPS) * dk ** -0.5)
        ks.append(k * lax.rsqrt(jnp.sum(k * k, axis=-1, keepdims=True) + EPS))
        vs.append(y[:, 2 * hw + lo:2 * hw + lo + dk])
        bbs.append(jnp.broadcast_to(beta_t[:, SM_BA + h:SM_BA + h + 1], (rb, dk)))
        gc = jnp.broadcast_to(gc_all[:, SM_AA + h:SM_AA + h + 1], (rb, dk))
        gc_t = jnp.concatenate([gc[r:r + dk].T for r in range(0, rb, dk)], axis=1)[:1]
        gc_i = jnp.concatenate([gc] * (rb // dk), axis=1)
        decays.append(jnp.where(incl, jnp.exp(jnp.where(incl, gc_i - gc_t, 0.0)), 0.0))
        gcs.append(gc)
    kbetas = [ks[h] * bbs[h] for h in heads]
    ypows = [jnp.where(strict, -(_bdot_nt(kbetas[h], ks[h]) * decays[h]), 0.0) for h in heads]
    tmats = [eye + ypows[h] for h in heads]
    for _ in range(int(np.log2(c)) - 1):
        ypows = [_bdot(ypows[h], ypows[h]) for h in heads]
        tmats = [tmats[h] + _bdot(tmats[h], ypows[h]) for h in heads]
    egcs = [jnp.exp(gcs[h]) for h in heads]
    wus = [_bdot(tmats[h], jnp.concatenate([kbetas[h] * egcs[h], vs[h] * bbs[h]], axis=1))
           for h in heads]
    aqks = [_bdot_nt(qs[h], ks[h]) * decays[h] for h in heads]
    qgs = [qs[h] * egcs[h] for h in heads]
    sts = [state_ref[h] for h in heads]
    v_news = [[] for _ in heads]
    o_inter = [[] for _ in heads]
    for ci in range(nch):
        r0 = ci * c
        for h in heads:
            gl = gcs[h][r0 + c - 1:r0 + c, :]
            kd = ks[h][r0:r0 + c] * jnp.exp(gl - gcs[h][r0:r0 + c])
            ws = _bdot(jnp.concatenate([wus[h][r0:r0 + c, :dk], qgs[h][r0:r0 + c]], axis=0), sts[h])
            v_new = wus[h][r0:r0 + c, dk:] - ws[:c]
            o_inter[h].append(ws[c:])
            sts[h] = sts[h] * jnp.exp(gl) + _bdot_tn(kd, v_new)
            v_news[h].append(v_new)
    for h in heads:
        lo = h * dk
        state_ref[h] = sts[h]
        o = jnp.concatenate(o_inter[h], axis=0) + _bdot(aqks[h], jnp.concatenate(v_news[h], axis=0))
        o = o * lax.rsqrt(jnp.mean(o * o, axis=-1, keepdims=True) + EPS) * on_ref[...]
        o_ref[:, lo:lo + dk] = o * _silu(za[:, lo:lo + dk])


def gdn_mixer(proj, conv_w, a_log, dt_bias, out_norm, batch, seq):
    t = proj.shape[0]
    rb = 256
    nsb = seq // rb
    hw = GDN_HEADS * GDN_DK
    nega =jnp.zeros((1, LANES), F32).at[0, SM_AA:SM_AA + GDN_HEADS].set(-jnp.exp(a_log))
    dtb = jnp.zeros((1, LANES), F32).at[0, SM_AA:SM_AA + GDN_HEADS].set(dt_bias)
    row = lambda b, s: b * nsb + s
    return pl.pallas_call(
        functools.partial(_gdn_kernel, rb=rb),
        grid=(batch, nsb),
        in_specs=[pl.BlockSpec((rb, 3 * hw), lambda b, s: (row(b, s), 0)),
                  pl.BlockSpec((rb, hw), lambda b, s: (row(b, s), EV_ZA // hw)),
                  pl.BlockSpec((rb, LANES), lambda b, s: (row(b, s), EV_SM // LANES)),
                  pl.BlockSpec((CONV_WIDTH, 3 * hw), lambda b, s: (0, 0)),
                  pl.BlockSpec((1, LANES), lambda b, s: (0, 0)),
                  pl.BlockSpec((1, LANES), lambda b, s: (0, 0)),
                  pl.BlockSpec((1, GDN_DK), lambda b, s: (0, 0))],
        out_specs=pl.BlockSpec((rb, hw), lambda b, s: (row(b, s), 0)),
        out_shape=jax.ShapeDtypeStruct((t, hw), F32),
        scratch_shapes=[pltpu.VMEM((SUBLANES, 3 * hw), F32),
                        pltpu.VMEM((GDN_HEADS, GDN_DK, GDN_DK), F32)],
        compiler_params=_cparams(("arbitrary", "arbitrary")),
        name="gdn_mixer",
    )(proj, proj, proj, conv_w, nega, dtb, out_norm.reshape(1, GDN_DK))


def _gla_kernel(qk_ref, v_ref, r_ref, gl_ref, wg_ref, bg_ref, on_ref, o_ref, state_ref, *, rb):
    c = GLA_CHUNK
    dk, dv = GLA_DK, GLA_DV
    sb = GLA_SUB

    @pl.when(pl.program_id(1) == 0)
    def _():
        state_ref[...] = jnp.zeros_like(state_ref)

    z = _dot3(gl_ref[...], wg_ref[...]) + bg_ref[...]
    log_a = -_softplus(-z) * (1.0 / GLA_TAU)
    qk = qk_ref[...]
    vv = v_ref[...]
    rr = r_ref[...]
    incl, _, _ = _tri_masks(c)
    incl_b = incl.astype(F32).astype(BF16)
    hw = GLA_HEADS * dk
    sub_row = lax.broadcasted_iota(I32, (sb, dk), 0)
    att_lane = lax.broadcasted_iota(I32, (sb, c), 1)

    for ci in range(rb // c):
        r0 = ci * c
        b_all = _dot_mask(incl_b, log_a[r0:r0 + c])
        for h in range(GLA_HEADS):
            q = qk[r0:r0 + c, h * dk:(h + 1) * dk] * dk ** -0.5
            k = qk[r0:r0 + c, hw + h * dk:hw + (h + 1) * dk]
            v = vv[r0:r0 + c, h * dv:(h + 1) * dv]
            b = b_all[:, h * dk:(h + 1) * dk]
            st = state_ref[h]
            o = _bdot_nt(q * jnp.exp(b), st)
            att_rows = []
            for bi in range(c // sb):
                s0 = bi * sb
                qi = q[s0:s0 + sb]
                bi_ = b[s0:s0 + sb]
                ref = b[s0:s0 + 1]
                if bi > 0:
                    qt = qi * jnp.exp(bi_ - ref)
                    kt = k * jnp.exp(jnp.minimum(ref - b, 0.0))
                    att = jnp.where(att_lane < s0, _bdot_nt(qt, kt), 0.0)
                else:
                    att = jnp.zeros((sb, c), F32)
                for j in range(sb):
                    kj = k[s0 + j:s0 + j + 1]
                    bj = b[s0 + j:s0 + j + 1]
                    e = jnp.exp(jnp.where(sub_row >= j, bi_ - bj, NEG_INF))
                    col = jnp.sum(qi * kj * e, axis=-1, keepdims=True)
                    att = jnp.where(att_lane == s0 + j, col, att)
                att_rows.append(att)
            o = o + _bdot(jnp.concatenate(att_rows, axis=0), v)
            bl = b[c - 1:c]
            state_ref[h] = st * jnp.exp(bl) + _bdot_tn(v, k * jnp.exp(bl - b))
            o = o * lax.rsqrt(jnp.mean(o * o, axis=-1, keepdims=True) + EPS) * on_ref[...]
            o_ref[r0:r0 + c, h * dv:(h + 1) * dv] = o * _silu(rr[r0:r0 + c, h * dv:(h + 1) * dv])


def gla_mixer(proj, w_gate2, b_gate2, out_norm, batch, seq):
    t = proj.shape[0]
    rb = GLA_CHUNK
    nsb = seq // rb
    hw = GLA_HEADS * GLA_DK
    hv = GLA_HEADS * GLA_DV
    wg = jnp.zeros((LANES, hw), F32).at[:GLA_RANK].set(w_gate2)
    row = lambda b, s: b * nsb + s
    return pl.pallas_call(
        functools.partial(_gla_kernel, rb=rb),
        grid=(batch, nsb),
        in_specs=[pl.BlockSpec((rb, 2 * hw), lambda b, s: (row(b, s), 0)),
                  pl.BlockSpec((rb, hv), lambda b, s: (row(b, s), OD_V // hv)),
                  pl.BlockSpec((rb, hv), lambda b, s: (row(b, s), OD_R // hv)),
                  pl.BlockSpec((rb, LANES), lambda b, s: (row(b, s), OD_G // LANES)),
                  pl.BlockSpec((LANES, hw), lambda b, s: (0, 0)),
                  pl.BlockSpec((1, hw), lambda b, s: (0, 0)),
                  pl.BlockSpec((1, GLA_DV), lambda b, s: (0, 0))],
        out_specs=pl.BlockSpec((rb, hv), lambda b, s: (row(b, s), 0)),
        out_shape=jax.ShapeDtypeStruct((t, hv), F32),
        scratch_shapes=[pltpu.VMEM((GLA_HEADS, GLA_DV, GLA_DK), F32)],
        compiler_params=_cparams(("arbitrary", "arbitrary")),
        name="gla_mixer",
    )(proj, proj, proj, proj, wg, b_gate2.reshape(1, hw), out_norm.reshape(1, GLA_DV))


def _rope_table_kernel(pos_ref, inva_ref, invi_ref, ca_ref, sa_ref, ci_ref, si_ref):
    pos = pos_ref[...].astype(F32)
    lane = lax.broadcasted_iota(I32, (pos.shape[0], LANES), 1)
    ang_a = pos * inva_ref[...]
    ca_ref[...] = jnp.cos(ang_a)
    sa_ref[...] = jnp.where(lane < ATT_DH // 2, -1.0, 1.0) * jnp.sin(ang_a)
    ang_i = pos * invi_ref[...]
    ci_ref[...] = jnp.cos(ang_i)
    si_ref[...] = jnp.where(lane % IDX_DH < IDX_DH // 2, -1.0, 1.0) * jnp.sin(ang_i)


def rope_tables(positions):
    t = positions.size
    inv_a = 1.0 / (ROPE_THETA ** (jnp.arange(0, ATT_DH, 2, dtype=F32) / ATT_DH))
    inv_i = 1.0 / (ROPE_THETA ** (jnp.arange(0, IDX_DH, 2, dtype=F32) / IDX_DH))
    inv_a = jnp.tile(inv_a, 2).reshape(1, LANES)
    inv_i = jnp.tile(inv_i, 4).reshape(1, LANES)
    tm = 512
    spec = pl.BlockSpec((tm, LANES), lambda i: (i, 0))
    vec = pl.BlockSpec((1, LANES), lambda i: (0, 0))
    return pl.pallas_call(
        _rope_table_kernel,
        grid=(t // tm,),
        in_specs=[pl.BlockSpec((tm, 1), lambda i: (i, 0)), vec, vec],
        out_specs=[spec] * 4,
        out_shape=[jax.ShapeDtypeStruct((t, LANES), F32)] * 4,
        compiler_params=_cparams(("arbitrary",)),
        name="rope_tables",
    )(positions.reshape(t, 1), inv_a, inv_i)


def _dsa_prep_kernel(qb_ref, kb_ref, vb_ref, qi_ref, sm_ref, ca_ref, sa_ref, ci_ref, si_ref,
                     qo_ref, ko_ref, vt_ref, qio_ref, kio_ref, kn_ref):
    ca, sa, ci, si = ca_ref[...], sa_ref[...], ci_ref[...], si_ref[...]
    lane = lax.broadcasted_iota(I32, ca.shape, 1)
    first_half = lane % IDX_DH < IDX_DH // 2
    knorm = jnp.zeros((SUBLANES, LANES), F32)
    lane8 = lax.broadcasted_iota(I32, (SUBLANES, LANES), 1)

    def rope_att(tile):
        return tile * ca + pltpu.roll(tile, ATT_DH // 2, axis=1) * sa

    def rope_idx(tile):
        partner = jnp.where(first_half, pltpu.roll(tile, LANES - IDX_DH // 2, axis=1),
                            pltpu.roll(tile, IDX_DH // 2, axis=1))
        return tile * ci + partner * si

    qb, kb = qb_ref[...], kb_ref[...]
    for h in range(ATT_HEADS):
        sl = slice(h * ATT_DH, (h + 1) * ATT_DH)
        qo_ref[:, sl] = (rope_att(qb[:, sl]) * ATT_DH ** -0.5).astype(BF16)
        kr = rope_att(kb[:, sl]).astype(BF16)
        ko_ref[:, sl] = kr
        krf = kr.astype(F32)
        knorm = jnp.where(lane8 == h, jnp.max(jnp.sum(krf * krf, axis=1, keepdims=True)), knorm)
    kn_ref[0] = knorm
    vt_ref[0] = vb_ref[...].T.astype(BF16)
    qi = qi_ref[...]
    for p in range(IDX_HEADS * IDX_DH // LANES):
        sl = slice(p * LANES, (p + 1) * LANES)
        qio_ref[:, sl] = rope_idx(qi[:, sl]).astype(BF16)
    ki = rope_idx(sm_ref[...])
    kio_ref[...] = jnp.where(lane < IDX_DH, ki, pltpu.roll(ki, IDX_DH, axis=1)).astype(BF16)


def dsa_prep(proj, tables, seq):
    t = proj.shape[0]
    tm = min(DSA_KEY_CHUNK, seq)
    aw = ATT_HEADS * ATT_DH
    iw = IDX_HEADS * IDX_DH
    tab = pl.BlockSpec((tm, LANES), lambda i: (i, 0))
    return pl.pallas_call(
        _dsa_prep_kernel,
        grid=(t // tm,),
        in_specs=[pl.BlockSpec((tm, aw), lambda i: (i, EV_QB // aw)),
                  pl.BlockSpec((tm, aw), lambda i: (i, EV_KB // aw)),
                  pl.BlockSpec((tm, aw), lambda i: (i, EV_VB // aw)),
                  pl.BlockSpec((tm, iw), lambda i: (i, EV_QI // iw)),
                  pl.BlockSpec((tm, LANES), lambda i: (i, EV_SM // LANES)),
                  tab, tab, tab, tab],
        out_specs=[pl.BlockSpec((tm, aw), lambda i: (i, 0)),
                   pl.BlockSpec((tm, aw), lambda i: (i, 0)),
                   pl.BlockSpec((1, aw, tm), lambda i: (i, 0, 0)),
                   pl.BlockSpec((tm, iw), lambda i: (i, 0)),
                   pl.BlockSpec((tm, LANES), lambda i: (i, 0)),
                   pl.BlockSpec((1, SUBLANES, LANES), lambda i: (i, 0, 0))],
        out_shape=[jax.ShapeDtypeStruct((t, aw), BF16),
                   jax.ShapeDtypeStruct((t, aw), BF16),
                   jax.ShapeDtypeStruct((t // tm, aw, tm), BF16),
                   jax.ShapeDtypeStruct((t, iw), BF16),
                   jax.ShapeDtypeStruct((t, LANES), BF16),
                   jax.ShapeDtypeStruct((t // tm, SUBLANES, LANES), F32)],
        compiler_params=_cparams(("arbitrary",)),
        name="dsa_prep",
    )(proj, proj, proj, proj, proj, *tables)


def _fold_rows(x, op, group=SUBLANES):
    parts = [x[r:r + group] for r in range(0, x.shape[0], group)]
    while len(parts) > 1:
        parts = [op(parts[j], parts[j + 1]) for j in range(0, len(parts), 2)]
    return parts[0]


I16 = jnp.int16
I16_MIN, I16_MAX = -(2 ** 15), 2 ** 15 - 1
PACKED_ROWS = 2 * SUBLANES


def _dsa_kernel(qb_ref, qi_ref, sm_ref, kn_ref, k_ref, vt_ref, ki_ref, o_ref,
                key_ref, half_ref, l_ref, acc_ref, *, kc, topk):
    nq = Q_BLOCK
    i = pl.program_id(1)
    nkc = (i * nq + nq + kc - 1) // kc
    q_pos = i * nq + lax.broadcasted_iota(I32, (1, nq), 1)
    key0 = lax.broadcasted_iota(I32, (kc, nq), 0)
    lane = lax.broadcasted_iota(I32, (nq, LANES), 1)
    wi_t = sm_ref[...].T * (IDX_HEADS * IDX_DH) ** -0.5
    qi = qi_ref[...]
    zero_b = jnp.zeros((nq, LANES), BF16)
    qi_heads = []
    for h in range(IDX_HEADS):
        pair = qi[:, (h // 2) * LANES:(h // 2 + 1) * LANES]
        own = (lane < IDX_DH) if h % 2 == 0 else (lane >= IDX_DH)
        qi_heads.append(jnp.where(own, pair, zero_b))

    def score_body(c, carry):
        start = pl.multiple_of(c * kc, kc)
        kic = ki_ref[0, pl.ds(start, kc), :]
        dots = [_dot_nt(kic, qi_heads[h]) for h in range(IDX_HEADS)]
        s = jnp.zeros((kc, nq), F32)
        for h in range(IDX_HEADS):
            s = s + wi_t[SM_WI + h:SM_WI + h + 1, :] * jnp.maximum(dots[h], 0.0)
        s = s + 0.0
        s = jnp.where(key0 <= q_pos - c * kc, s, NEG_INF)
        bits = pltpu.bitcast(s, I32)
        key = bits ^ ((bits >> 31) & 0x7FFFFFFF)
        key_ref[c] = key
        half_ref[c] = (key >> 16).astype(I16)
        return carry

    lax.fori_loop(0, nkc, score_body, 0)

    def count(pred):
        def body(c, acc):
            return acc + _fold_rows(jnp.where(pred(key_ref[c], c), 1, 0).astype(I32), jnp.add)
        acc = lax.fori_loop(0, nkc, body, jnp.zeros((SUBLANES, nq), I32))
        return jnp.sum(acc, axis=0, keepdims=True)

    def search_half():
        one, zero = jnp.ones((), I16), jnp.zeros((), I16)

        def count_ge(cand):
            c16 = cand.astype(I16)

            def body(c, acc):
                return acc + _fold_rows(jnp.where(half_ref[c] >= c16, one, zero), jnp.add,
                                        PACKED_ROWS)

            acc = lax.fori_loop(0, nkc, body, jnp.zeros((PACKED_ROWS, nq), I16))
            return jnp.sum(acc.astype(I32), axis=0, keepdims=True)

        t = jnp.where(count_ge(jnp.zeros((1, nq), I32)) >= topk, 0, I16_MIN).astype(I32)

        def bit_body(it, t):
            cand = t + (jnp.int32(1) << (14 - it))
            return jnp.where(count_ge(cand) >= topk, cand, t)

        return lax.fori_loop(0, 15, bit_body, t)

    hi_thr = search_half()

    def low_body(c, carry):
        key = key_ref[c]
        hi = key >> 16
        low = (key & 0xFFFF) + I16_MIN
        half_ref[c] = jnp.where(hi > hi_thr, I16_MAX,
                                jnp.where(hi < hi_thr, I16_MIN, low)).astype(I16)
        return carry

    lax.fori_loop(0, nkc, low_body, 0)
    thr = (hi_thr << 16) + (search_half() - I16_MIN)

    def gt_eq_body(c, accs):
        key = key_ref[c]
        gt = _fold_rows(jnp.where(key > thr, 1, 0).astype(I32), jnp.add)
        eq = _fold_rows(jnp.where(key == thr, 1, 0).astype(I32), jnp.add)
        return accs[0] + gt, accs[1] + eq

    zero8 = jnp.zeros((SUBLANES, nq), I32)
    gt8, eq8 = lax.fori_loop(0, nkc, gt_eq_body, (zero8, zero8))
    n_gt = jnp.sum(gt8, axis=0, keepdims=True)
    n_eq = jnp.sum(eq8, axis=0, keepdims=True)
    need = topk - n_gt
    big = jnp.int32(2 ** 30)

    def tie_search():
        def idx_body(it, x):
            cand = x + (jnp.int32(1) << (14 - it))
            below = count(lambda key, c: (key0 < jnp.where(key == thr, cand - c * kc, 0)))
            return jnp.where(below < need, cand, x)
        return lax.fori_loop(0, 15, idx_body, jnp.zeros((1, nq), I32))

    any_tie = jnp.max(jnp.where(n_eq > need, 1, 0)) > 0
    last = lax.cond(any_tie, tie_search, lambda: jnp.full((1, nq), big, I32))

    last = jnp.where(thr == KEY_NEG_INF, -1, last)
    zero_f = jnp.float32(0.0)
    ninf_f = jnp.float32(NEG_INF)
    qb = qb_ref[...]

    def select_bias(c):
        key = key_ref[c]
        tie_ok = jnp.where(key0 <= last - c * kc, zero_f, ninf_f)
        return jnp.where(key > thr, zero_f, jnp.where(key == thr, tie_ok, ninf_f))

    def chunk_logits(c, h, bias):
        start = pl.multiple_of(c * kc, kc)
        kch = k_ref[0, pl.ds(start, kc), h * ATT_DH:(h + 1) * ATT_DH]
        return _dot_nt(kch, qb[:, h * ATT_DH:(h + 1) * ATT_DH]) + bias

    def attend(shifts):
        l_ref[...] = jnp.zeros(l_ref.shape, F32)
        acc_ref[...] = jnp.zeros(acc_ref.shape, F32)

        def body(c, carry):
            bias = select_bias(c)
            heads = range(ATT_HEADS)
            logits = [chunk_logits(c, h, bias) for h in heads]
            ps = [jnp.exp(logits[h] - shifts[h]) for h in heads]
            for h in heads:
                l_ref[h] = l_ref[h] + _fold_rows(ps[h], jnp.add)
                vth = vt_ref[0, c, h * ATT_DH:(h + 1) * ATT_DH, :]
                acc_ref[h] = acc_ref[h] + _dot(vth, ps[h].astype(BF16))
            return carry

        lax.fori_loop(0, nkc, body, 0)
        return [jnp.sum(l_ref[h], axis=0, keepdims=True) for h in range(ATT_HEADS)]

    ones_b = jnp.ones((SUBLANES, ATT_DH), BF16)
    bounds = []
    for h in range(ATT_HEADS):
        qh = qb[:, h * ATT_DH:(h + 1) * ATT_DH].astype(F32)
        qsq = _dot_nt(ones_b, (qh * qh).astype(BF16))[0:1, :]
        bounds.append(jnp.sqrt(qsq * kn_ref[0][0:1, h:h + 1]) * 1.02)
    sums = attend(bounds)

    def write_out(sums):
        for h in range(ATT_HEADS):
            o_ref[:, h * ATT_DH:(h + 1) * ATT_DH] = (acc_ref[h] / sums[h]).T

    write_out(sums)
    lowest = jnp.minimum(jnp.minimum(sums[0], sums[1]), jnp.minimum(sums[2], sums[3]))

    @pl.when(jnp.min(lowest) <= 0.0)
    def _():
        l_ref[...] = jnp.full(l_ref.shape, NEG_INF, F32)

        def max_body(c, carry):
            bias = select_bias(c)
            for h in range(ATT_HEADS):
                l_ref[h] = jnp.maximum(l_ref[h], _fold_rows(chunk_logits(c, h, bias), jnp.maximum))
            return carry

        lax.fori_loop(0, nkc, max_body, 0)
        exact = [jnp.max(l_ref[h], axis=0, keepdims=True) for h in range(ATT_HEADS)]
        write_out(attend(exact))


def dsa_attention(proj, qb, kb, vt, qi, ki, knorm, batch, seq):
    t = proj.shape[0]
    aw = ATT_HEADS * ATT_DH
    iw = IDX_HEADS * IDX_DH
    kc = min(DSA_KEY_CHUNK, seq)
    nkc = seq // kc
    topk = min(TOPK_MAX, seq // 4)
    nq = seq // Q_BLOCK
    row = lambda b, i: b * nq + i
    kb3, ki3 = (a.reshape(batch, seq, a.shape[1]) for a in (kb, ki))
    vt4 = vt.reshape(batch, nkc, aw, kc)
    kmax = jnp.max(knorm.reshape(batch, nkc, SUBLANES, LANES), axis=1)
    resident = lambda shape: pl.BlockSpec((1,) + shape, lambda b, i: (b,) + (0,) * len(shape),
                                          pipeline_mode=pl.Buffered(1))
    return pl.pallas_call(
        functools.partial(_dsa_kernel, kc=kc, topk=topk),
        grid=(batch, nq),
        in_specs=[pl.BlockSpec((Q_BLOCK, aw), lambda b, i: (row(b, i), 0)),
                  pl.BlockSpec((Q_BLOCK, iw), lambda b, i: (row(b, i), 0)),
                  pl.BlockSpec((Q_BLOCK, LANES), lambda b, i: (row(b, i), EV_SM // LANES)),
                  pl.BlockSpec((1, SUBLANES, LANES), lambda b, i: (b, 0, 0)),
                  resident((seq, aw)), resident((nkc, aw, kc)), resident((seq, LANES))],
        out_specs=pl.BlockSpec((Q_BLOCK, aw), lambda b, i: (row(b, i), 0)),
        out_shape=jax.ShapeDtypeStruct((t, aw), F32),
        scratch_shapes=[pltpu.VMEM((nkc, kc, Q_BLOCK), I32),
                        pltpu.VMEM((nkc, kc, Q_BLOCK), I16),
                        pltpu.VMEM((ATT_HEADS, SUBLANES, Q_BLOCK), F32),
                        pltpu.VMEM((ATT_HEADS, ATT_DH, Q_BLOCK), F32)],
        compiler_params=_cparams(("arbitrary", "arbitrary")),
        name="dsa_attention",
    )(qb, qi, proj, kmax, kb3, vt4, ki3)


def _router_kernel(x_ref, g_ref, sh_ref, sc_ref, wr_ref, br_ref, h_ref, route_ref, cnt_ref,
                   carry_ref):
    @pl.when(pl.program_id(0) == 0)
    def _():
        carry_ref[...] = jnp.zeros_like(carry_ref)

    h = _norm_mod(x_ref[...], g_ref[...], sh_ref[0], sc_ref[0])
    h_ref[...] = h
    tm = h.shape[0]
    logits = _dot3(h, wr_ref[...]) + br_ref[...]
    lane = lax.broadcasted_iota(I32, (tm, LANES), 1)
    far = jnp.int32(LANES)

    def first_max(vals):
        m = jnp.max(vals, axis=1, keepdims=True)
        return m, jnp.min(jnp.where(vals == m, lane, far), axis=1, keepdims=True)

    is_group = lane < N_GROUPS
    gmax, gsel = first_max(jnp.where(is_group, logits, NEG_INF))
    gsum = jnp.sum(jnp.where(is_group, jnp.exp(logits - gmax), 0.0), axis=1, keepdims=True)
    lo = N_GROUPS + EXPERTS_PER_GROUP * gsel
    el = jnp.where((lane >= lo) & (lane < lo + EXPERTS_PER_GROUP), logits, NEG_INF)
    m1, i1 = first_max(el)
    m2, i2 = first_max(jnp.where(lane == i1, NEG_INF, el))
    e2 = jnp.exp(m2 - m1)
    gate1 = 1.0 / (gsum * (1.0 + e2))
    gate2 = e2 * gate1
    eid1 = i1 - N_GROUPS
    eid2 = i2 - N_GROUPS
    oh1 = lane == eid1
    oh2 = lane == eid2
    onehot = jnp.where(oh1 | oh2, 1.0, 0.0)
    r = lax.broadcasted_iota(I32, (tm, tm), 0)
    cidx = lax.broadcasted_iota(I32, (tm, tm), 1)
    below = jnp.where(r > cidx, 1.0, 0.0).astype(BF16)
    cum = _dot(below, onehot.astype(BF16)) + carry_ref[0:1, :]
    rank1 = jnp.sum(jnp.where(oh1, cum, 0.0), axis=1, keepdims=True)
    rank2 = jnp.sum(jnp.where(oh2, cum, 0.0), axis=1, keepdims=True)
    carry_ref[...] = carry_ref[...] + jnp.sum(onehot, axis=0, keepdims=True)
    cnt_ref[...] = carry_ref[...]
    out = jnp.zeros((tm, LANES), F32)
    for j, val in enumerate((eid1.astype(F32), eid2.astype(F32), gate1, gate2, rank1, rank2)):
        out = jnp.where(lane == j, val, out)
    route_ref[...] = out


def moe_router(x, g, shift, scale, wr, br, seq):
    t, d = x.shape
    tm = 512
    per_b = seq // tm
    return pl.pallas_call(
        _router_kernel,
        grid=(t // tm,),
        in_specs=[pl.BlockSpec((tm, d), lambda i: (i, 0)),
                  pl.BlockSpec((1, d), lambda i: (0, 0)),
                  pl.BlockSpec((1, 1, d), lambda i: (i // per_b, 0, 0)),
                  pl.BlockSpec((1, 1, d), lambda i: (i // per_b, 0, 0)),
                  pl.BlockSpec((d, LANES), lambda i: (0, 0)),
                  pl.BlockSpec((1, LANES), lambda i: (0, 0))],
        out_specs=[pl.BlockSpec((tm, d), lambda i: (i, 0)),
                   pl.BlockSpec((tm, LANES), lambda i: (i, 0)),
                   pl.BlockSpec((SUBLANES, LANES), lambda i: (0, 0))],
        out_shape=[jax.ShapeDtypeStruct((t, d), F32),
                   jax.ShapeDtypeStruct((t, LANES), F32),
                   jax.ShapeDtypeStruct((SUBLANES, LANES), F32)],
        scratch_shapes=[pltpu.VMEM((SUBLANES, LANES), F32)],
        compiler_params=_cparams(("arbitrary",)),
        name="moe_router",
    )(x, g, shift, scale, wr, br)


def _zero_kernel(o_ref):
    o_ref[...] = jnp.zeros_like(o_ref)


def zero_rows(rows, d):
    tm = 1024
    return pl.pallas_call(
        _zero_kernel,
        grid=(rows // tm,),
        out_specs=pl.BlockSpec((tm, d), lambda i: (i, 0)),
        out_shape=jax.ShapeDtypeStruct((rows, d), F32),
        compiler_params=_cparams(("arbitrary",)),
        name="zero_rows",
    )()


def _dispatch_kernel(dest_ref, seg_ref, h_ref, xs_in_ref, xs_ref, inv_ref, sem, *, tm):
    del xs_in_ref
    i = pl.program_id(0)
    base = i * tm * 2

    rows = MOE_ROWS
    n_assign = dest_ref.shape[0]

    @pl.when(i == 0)
    def _():
        def mark(p, carry):
            blk = p // rows
            spare = ((blk + MOE_OUT_BUFS - 1) % MOE_OUT_BUFS) * rows + (p - blk * rows)
            inv_ref[p] = n_assign + spare
            return carry
        lax.fori_loop(0, rows, mark, 0)
        for e in range(N_EXPERTS):
            lax.fori_loop(rows + seg_ref[e], rows + seg_ref[N_EXPERTS + e], mark, 0)
        lax.fori_loop(rows + seg_ref[2 * N_EXPERTS - 1], inv_ref.shape[0], mark, 0)

    def row_copy(r, d):
        return pltpu.make_async_copy(h_ref.at[pl.ds(r, 1)], xs_ref.at[pl.ds(d, 1)], sem)

    n_tok = n_assign // 2

    def start(r, carry):
        for j in range(2):
            row_copy(r, dest_ref[base + 2 * r + j]).start()
        return carry

    group = 8

    def record(g, carry):
        r0 = g * group
        ds = [[dest_ref[base + 2 * (r0 + k) + j] for j in range(2)] for k in range(group)]
        for k in range(group):
            for j in range(2):
                inv_ref[rows + ds[k][j]] = j * n_tok + i * tm + r0 + k
        return carry

    def wait(r, carry):
        row_copy(0, 0).wait()
        row_copy(0, 0).wait()
        return carry

    lax.fori_loop(0, tm, start, 0)
    lax.fori_loop(0, tm // group, record, 0)
    lax.fori_loop(0, tm, wait, 0)


def moe_dispatch(dest, seg, h, xs_zero):
    t, d = h.shape
    tm = 256
    return pl.pallas_call(
        functools.partial(_dispatch_kernel, tm=tm),
        grid_spec=pltpu.PrefetchScalarGridSpec(
            num_scalar_prefetch=2,
            grid=(t // tm,),
            in_specs=[pl.BlockSpec((tm, d), lambda i, dest, seg: (i, 0)),
                      pl.BlockSpec(memory_space=pl.ANY)],
            out_specs=[pl.BlockSpec(memory_space=pl.ANY),
                       pl.BlockSpec(memory_space=pltpu.SMEM)],
            scratch_shapes=[pltpu.SemaphoreType.DMA(())]),
        out_shape=[jax.ShapeDtypeStruct(xs_zero.shape, F32),
                   jax.ShapeDtypeStruct((xs_zero.shape[0] + MOE_ROWS,), I32)],
        input_output_aliases={3: 0},
        compiler_params=_cparams(("arbitrary",)),
        name="moe_dispatch",
    )(dest, seg, h, xs_zero)


MOE_OUT_BUFS = 3


def _expert_kernel(be_ref, inv_ref, xs_ref, w1_ref, w3_ref, w2_ref, y_ref,
                   w1b, w3b, w2b, ybuf, sems, *, n_assign):
    i = pl.program_id(0)
    last = pl.num_programs(0) - 1
    rows = ybuf.shape[1]
    slot = i % MOE_OUT_BUFS
    prev_slot = (i + MOE_OUT_BUFS - 1) % MOE_OUT_BUFS
    prev = be_ref[jnp.maximum(i - 1, 0)]

    def row_copy(s, r, dst):
        return pltpu.make_async_copy(ybuf.at[s, pl.ds(r, 1)], y_ref.at[pl.ds(dst, 1)], sems.at[s])

    def send(s, block):
        for r in range(rows):
            row_copy(s, r, inv_ref[(block + 1) * rows + r]).start()

    def drain(s):
        for r in range(rows):
            row_copy(s, 0, 0).wait()

    @pl.when(i == 0)
    def _():
        ybuf[prev_slot] = jnp.zeros((rows, ybuf.shape[2]), F32)
        for s in range(MOE_OUT_BUFS):
            if s != MOE_OUT_BUFS - 1:
                for r in range(rows):
                    row_copy(prev_slot, r, n_assign + s * rows + r).start()
                drain(prev_slot)

    @pl.when(i >= MOE_OUT_BUFS - 1)
    def _():
        drain(slot)

    @pl.when((i == 0) | (be_ref[i] != prev))
    def _():
        w1b[...] = w1_ref[0, 0].astype(BF16)
        w3b[...] = w3_ref[0, 0].astype(BF16)
        w2b[...] = w2_ref[0, 0].astype(BF16)

    send(prev_slot, i - 1)
    x = xs_ref[...].astype(BF16)
    a = _dot(x, w1b[...])
    hid = _silu(a) * _dot(x, w3b[...])
    ybuf[slot] = _dot(hid.astype(BF16), w2b[...])

    @pl.when(i == last)
    def _():
        send(slot, i)
        for s in range(MOE_OUT_BUFS):
            drain(s)


def moe_experts(blk_exp, inv, xs, w1, w3, w2, layer, t):
    p, d = xs.shape
    de = w1.shape[3]
    rows = MOE_ROWS
    wmap = lambda i, be, inv: (layer, be[i], 0, 0)
    return pl.pallas_call(
        functools.partial(_expert_kernel, n_assign=2 * t),
        grid_spec=pltpu.PrefetchScalarGridSpec(
            num_scalar_prefetch=2,
            grid=(p // rows,),
            in_specs=[pl.BlockSpec((rows, d), lambda i, be, inv: (i, 0)),
                      pl.BlockSpec((1, 1, d, de), wmap),
                      pl.BlockSpec((1, 1, d, de), wmap),
                      pl.BlockSpec((1, 1, de, d), wmap)],
            out_specs=pl.BlockSpec(memory_space=pl.ANY),
            scratch_shapes=[pltpu.VMEM((d, de), BF16), pltpu.VMEM((d, de), BF16),
                            pltpu.VMEM((de, d), BF16), pltpu.VMEM((MOE_OUT_BUFS, rows, d), F32),
                            pltpu.SemaphoreType.DMA((MOE_OUT_BUFS,))]),
        out_shape=jax.ShapeDtypeStruct((2 * t + MOE_OUT_BUFS * rows, d), F32),
        compiler_params=_cparams(("arbitrary",)),
        name="moe_experts",
    )(blk_exp, inv, xs, w1, w3, w2)


def _combine_kernel(y1_ref, y2_ref, route_ref, x_ref, gt_ref, *rest, final):
    route = route_ref[...]
    y = route[:, 2:3] * y1_ref[...] + route[:, 3:4] * y2_ref[...]
    out = x_ref[...] + gt_ref[0] * y
    if final:
        fn_ref, o_ref = rest
        out = out * lax.rsqrt(jnp.mean(out * out, axis=-1, keepdims=True) + EPS) * fn_ref[...]
    else:
        (o_ref,) = rest
    o_ref[...] = out


def moe_combine(y2, route, x, gate, seq, final_gain=None):
    t, d = x.shape
    tm = 512
    per_b = seq // tm
    final = final_gain is not None
    nt = t // tm
    in_specs = [pl.BlockSpec((tm, d), lambda i: (i, 0)),
                pl.BlockSpec((tm, d), lambda i: (nt + i, 0)),
                pl.BlockSpec((tm, LANES), lambda i: (i, 0)),
                pl.BlockSpec((tm, d), lambda i: (i, 0)),
                pl.BlockSpec((1, 1, d), lambda i: (i // per_b, 0, 0))]
    args = [y2, y2, route, x, gate]
    if final:
        in_specs.append(pl.BlockSpec((1, d), lambda i: (0, 0)))
        args.append(final_gain)
    return pl.pallas_call(
        functools.partial(_combine_kernel, final=final),
        grid=(t // tm,),
        in_specs=in_specs,
        out_specs=pl.BlockSpec((tm, d), lambda i: (i, 0)),
        out_shape=jax.ShapeDtypeStruct((t, d), F32),
        compiler_params=_cparams(("arbitrary",)),
        name="moe_combine",
    )(*args)


def hier_moe_layer(x, g, shift, scale, gate, wg, bg, we, be, w1, w3, w2, layer, seq,
                   final_gain=None):
    t, d = x.shape
    wr = jnp.zeros((d, LANES), F32)
    wr = wr.at[:, :N_GROUPS].set(wg)
    wr = wr.at[:, N_GROUPS:N_GROUPS + N_EXPERTS].set(we.transpose(1, 0, 2).reshape(d, N_EXPERTS))
    br = jnp.zeros((1, LANES), F32)
    br = br.at[0, :N_GROUPS].set(bg).at[0, N_GROUPS:N_GROUPS + N_EXPERTS].set(be.reshape(-1))
    h, route, cnt = moe_router(x, g, shift, scale, wr, br, seq)
    counts = cnt[0, :N_EXPERTS].astype(I32)
    padded = ((counts + MOE_ROWS - 1) // MOE_ROWS) * MOE_ROWS
    pad_end = jnp.cumsum(padded)
    pad_start = pad_end - padded
    eid = route[:, 0:2].astype(I32)
    rank = route[:, 4:6].astype(I32)
    experts = jnp.arange(N_EXPERTS, dtype=I32)
    start_of = jnp.sum(jnp.where(eid[:, :, None] == experts, pad_start, 0), axis=-1)
    dest = (start_of + rank).reshape(-1)
    p_rows = 2 * t + N_EXPERTS * MOE_ROWS
    nblk = p_rows // MOE_ROWS
    blk_row = jnp.arange(nblk, dtype=I32)[:, None] * MOE_ROWS
    blk_exp = jnp.minimum(jnp.sum((pad_end[None, :] <= blk_row).astype(I32), axis=1),
                          N_EXPERTS - 1)
    seg = jnp.concatenate([pad_start + counts, pad_end]).astype(I32)
    xs, inv = moe_dispatch(dest, seg, h, zero_rows(p_rows, d))
    y2 = moe_experts(blk_exp, inv, xs, w1, w3, w2, layer, t)
    return moe_combine(y2, route, x, gate, seq, final_gain)


def _even_weight(w_in):
    d = w_in.shape[0]
    sizes = [512, 512, 512, 512, 4, 4, 512, 512, 512, 256, 64, 4]
    offs = np.concatenate([[0], np.cumsum(sizes)])
    qa, ka, va, za, ba, aa, qb, kb, vb, qi, ki, wi = (w_in[:, offs[j]:offs[j + 1]]
                                                      for j in range(len(sizes)))
    pad = jnp.zeros((d, EV_N - EV_SM - 76), w_in.dtype)
    return jnp.concatenate([qa, ka, va, za, qb, kb, vb, qi, ki, ba, aa, wi, pad], axis=1)


def kernel(x, c, positions, ada_w, ada_b, norm_mix, norm_ffn, even_w_in, gdn_conv_w, gdn_a_log, gdn_dt_bias, gdn_out_norm, even_w_out, gla_w_in, gla_w_gate2, gla_b_gate2, gla_out_norm, gla_w_out, router_group_w, router_group_b, router_exp_w, router_exp_b, exp_w1, exp_w3, exp_w2, final_norm):
    batch, seq, d = x.shape
    depth = ada_w.shape[0]
    t = batch * seq
    xt = x.reshape(t, d)
    mod = ada_modulation(c, ada_w, ada_b)
    tables = rope_tables(positions)
    for layer in range(depth):
        sh_m, sc_m, gt_m, sh_f, sc_f, gt_f = (mod[layer, :, j * d:(j + 1) * d].reshape(batch, 1, d)
                                              for j in range(6))
        i = layer // 2
        g_mix = norm_mix[layer].reshape(1, d)
        if layer % 2 == 0:
            w_in = _even_weight(even_w_in[i]).astype(BF16)
            proj = in_projection(xt, g_mix, sh_m, sc_m, w_in, seq)
            oa = gdn_mixer(proj, gdn_conv_w[i], gdn_a_log[i], gdn_dt_bias[i], gdn_out_norm[i],
                           batch, seq)
            qb, kb, vt, qi, ki, knorm = dsa_prep(proj, tables, seq)
            ob = dsa_attention(proj, qb, kb, vt, qi, ki, knorm, batch, seq)
            w_out = even_w_out[i].astype(BF16)
            half = GDN_HEADS * GDN_DK
            xt = out_projection([oa, ob], [w_out[:half], w_out[half:]], xt, gt_m, seq)
        else:
            w_in = jnp.pad(gla_w_in[i], ((0, 0), (0, OD_N - gla_w_in.shape[2]))).astype(BF16)
            proj = in_projection(xt, g_mix, sh_m, sc_m, w_in, seq)
            o = gla_mixer(proj, gla_w_gate2[i], gla_b_gate2[i], gla_out_norm[i], batch, seq)
            xt = out_projection([o], [gla_w_out[i].astype(BF16)], xt, gt_m, seq)
        xt = hier_moe_layer(xt, norm_ffn[layer].reshape(1, d), sh_f, sc_f, gt_f,
                            router_group_w[layer], router_group_b[layer],
                            router_exp_w[layer], router_exp_b[layer],
                            exp_w1, exp_w3, exp_w2, layer, seq,
                            final_norm.reshape(1, d) if layer == depth - 1 else None)
    return xt.reshape(batch, seq, d)
```

```python
import functools

import jax
import jax.numpy as jnp
import numpy as np
from jax import lax
from jax.experimental import pallas as pl
from jax.experimental.pallas import tpu as pltpu

F32 = jnp.float32
BF16 = jnp.bfloat16
I32 = jnp.int32
HI = lax.Precision.HIGHEST

LANES = 128
SUBLANES = 8
VMEM_LIMIT = 56 * 1024 * 1024

EPS = 1e-6
ROPE_THETA = 10000.0
GDN_HEADS, GDN_DK = 4, 128
CONV_WIDTH = 4
GDN_CHUNK = 64
ATT_HEADS, ATT_DH = 4, 128
IDX_HEADS, IDX_DH = 4, 64
TOPK_MAX = 256
Q_BLOCK = 256
DSA_KEY_CHUNK = 512
GLA_HEADS, GLA_DK, GLA_DV = 4, 128, 256
GLA_RANK = 16
GLA_TAU = 16.0
GLA_CHUNK = 64
GLA_SUB = 16
N_GROUPS, EXPERTS_PER_GROUP = 4, 8
N_EXPERTS = N_GROUPS * EXPERTS_PER_GROUP
MOE_ROWS = 256

EV_QA, EV_KA, EV_VA, EV_ZA = 0, 512, 1024, 1536
EV_QB, EV_KB, EV_VB, EV_QI, EV_SM = 2048, 2560, 3072, 3584, 3840
EV_N = 3968
SM_KI, SM_BA, SM_AA, SM_WI = 0, 64, 68, 72
OD_Q, OD_K, OD_V, OD_R, OD_G = 0, 512, 1024, 2048, 3072
OD_N = 3200

NEG_INF = float("-inf")
INT_MIN = -(2 ** 31)
KEY_NEG_INF = INT_MIN + 0x7FFFFF


def _cparams(sem):
    return pltpu.CompilerParams(dimension_semantics=sem, vmem_limit_bytes=VMEM_LIMIT)


def _sigmoid(x):
    return 1.0 / (1.0 + jnp.exp(-x))


def _silu(x):
    return x * _sigmoid(x)


def _softplus(x):
    return jnp.maximum(x, 0.0) + jnp.log(1.0 + jnp.exp(-jnp.abs(x)))


def _dot(a, b, precision=None):
    return jnp.dot(a, b, precision=precision, preferred_element_type=F32)


def _dot_nt(a, b, precision=None):
    return lax.dot_general(a, b, (((1,), (1,)), ((), ())), precision=precision,
                           preferred_element_type=F32)


def _dot_tn(a, b, precision=None):
    return lax.dot_general(a, b, (((0,), (0,)), ((), ())), precision=precision,
                           preferred_element_type=F32)


def _bdot(a, b):
    return _dot(a.astype(BF16), b.astype(BF16))


def _bdot_nt(a, b):
    return _dot_nt(a.astype(BF16), b.astype(BF16))


def _bdot_tn(a, b):
    return _dot_tn(a.astype(BF16), b.astype(BF16))


def _split2(a):
    hi = a.astype(BF16)
    return hi, (a - hi.astype(F32)).astype(BF16)


def _dot3(a, b):
    ah, al = _split2(a)
    bh, bl = _split2(b)
    return _dot(ah, bh) + (_dot(ah, bl) + _dot(al, bh))


def _dot_mask(mask_b, x):
    x0 = x.astype(BF16)
    r1 = x - x0.astype(F32)
    x1 = r1.astype(BF16)
    x2 = (r1 - x1.astype(F32)).astype(BF16)
    return _dot(mask_b, x0) + (_dot(mask_b, x1) + _dot(mask_b, x2))


def _norm_mod(x, g, shift, scale):
    ms = jnp.mean(x * x, axis=-1, keepdims=True)
    y = x * lax.rsqrt(ms + EPS) * g
    return y * (1.0 + scale) + shift


def _ada_kernel(c_ref, w_ref, b_ref, o_ref):
    o_ref[0] = _dot(_silu(c_ref[...]), w_ref[0], HI) + b_ref[0]


def ada_modulation(c, ada_w, ada_b):
    depth, d, n = ada_w.shape
    b = c.shape[0]
    cp = jnp.zeros((SUBLANES, d), F32).at[:b].set(c)
    tn = 1536
    out = pl.pallas_call(
        _ada_kernel,
        grid=(depth, n // tn),
        in_specs=[pl.BlockSpec((SUBLANES, d), lambda l, j: (0, 0)),
                  pl.BlockSpec((1, d, tn), lambda l, j: (l, 0, j)),
                  pl.BlockSpec((1, 1, tn), lambda l, j: (l, 0, j))],
        out_specs=pl.BlockSpec((1, SUBLANES, tn), lambda l, j: (l, 0, j)),
        out_shape=jax.ShapeDtypeStruct((depth, SUBLANES, n), F32),
        compiler_params=_cparams(("arbitrary", "arbitrary")),
        name="ada_modulation",
    )(cp, ada_w, ada_b.reshape(depth, 1, n))
    return out[:, :b]


def _inproj_kernel(x_ref, g_ref, sh_ref, sc_ref, w_ref, o_ref):
    h = _norm_mod(x_ref[...], g_ref[...], sh_ref[0], sc_ref[0])
    o_ref[...] = _dot(h.astype(BF16), w_ref[...])


def in_projection(x, g, shift, scale, w, seq):
    t, d = x.shape
    n = w.shape[1]
    tm = 512
    per_b = seq // tm
    return pl.pallas_call(
        _inproj_kernel,
        grid=(t // tm,),
        in_specs=[pl.BlockSpec((tm, d), lambda i: (i, 0)),
                  pl.BlockSpec((1, d), lambda i: (0, 0)),
                  pl.BlockSpec((1, 1, d), lambda i: (i // per_b, 0, 0)),
                  pl.BlockSpec((1, 1, d), lambda i: (i // per_b, 0, 0)),
                  pl.BlockSpec((d, n), lambda i: (0, 0))],
        out_specs=pl.BlockSpec((tm, n), lambda i: (i, 0)),
        out_shape=jax.ShapeDtypeStruct((t, n), F32),
        compiler_params=_cparams(("arbitrary",)),
        name="in_projection",
    )(x, g, shift, scale, w)


def _outproj_kernel(*refs, n_in):
    o_refs = refs[:n_in]
    w_refs = refs[n_in:2 * n_in]
    x_ref, gt_ref, xo_ref = refs[2 * n_in:]
    acc = _dot(o_refs[0][...].astype(BF16), w_refs[0][...])
    for o_ref, w_ref in zip(o_refs[1:], w_refs[1:]):
        acc = acc + _dot(o_ref[...].astype(BF16), w_ref[...])
    xo_ref[...] = x_ref[...] + gt_ref[0] * acc


def out_projection(parts, weights, x, gate, seq):
    t, d = x.shape
    tm = 512
    per_b = seq // tm
    n_in = len(parts)
    in_specs = [pl.BlockSpec((tm, p.shape[1]), lambda i: (i, 0)) for p in parts]
    in_specs += [pl.BlockSpec(w.shape, lambda i: (0, 0)) for w in weights]
    in_specs += [pl.BlockSpec((tm, d), lambda i: (i, 0)),
                 pl.BlockSpec((1, 1, d), lambda i: (i // per_b, 0, 0))]
    return pl.pallas_call(
        functools.partial(_outproj_kernel, n_in=n_in),
        grid=(t // tm,),
        in_specs=in_specs,
        out_specs=pl.BlockSpec((tm, d), lambda i: (i, 0)),
        out_shape=jax.ShapeDtypeStruct((t, d), F32),
        compiler_params=_cparams(("arbitrary",)),
        name="out_projection",
    )(*parts, *weights, x, gate)


def _tri_masks(c):
    row = lax.broadcasted_iota(I32, (c, c), 0)
    col = lax.broadcasted_iota(I32, (c, c), 1)
    return row >= col, row > col, row == col


def _gdn_kernel(qkv_ref, za_ref, sm_ref, cw_ref, nega_ref, dtb_ref, on_ref, o_ref,
                tail_ref, state_ref, *, rb):
    c = GDN_CHUNK
    dk = GDN_DK

    @pl.when(pl.program_id(1) == 0)
    def _():
        tail_ref[...] = jnp.zeros_like(tail_ref)
        state_ref[...] = jnp.zeros_like(state_ref)

    x = qkv_ref[...]
    xx = jnp.concatenate([tail_ref[...], x], axis=0)
    tail_ref[...] = x[rb - SUBLANES:, :]
    cw = cw_ref[...]
    y = x * cw[CONV_WIDTH - 1:CONV_WIDTH, :]
    for i in range(1, CONV_WIDTH):
        y = y + pltpu.roll(xx, i, axis=0)[SUBLANES:, :] * cw[CONV_WIDTH - 1 - i:CONV_WIDTH - i, :]
    y = _silu(y)

    sm = sm_ref[...]
    beta_t = _sigmoid(sm)
    g_t = nega_ref[...] * _softplus(sm + dtb_ref[...])
    za = za_ref[...]
    hw = GDN_HEADS * dk
    nch = rb // c
    row = lax.broadcasted_iota(I32, (rb, rb), 0)
    col = lax.broadcasted_iota(I32, (rb, rb), 1)
    shift = int(np.log2(c))
    same = (row >> shift) == (col >> shift)
    incl = same & (row >= col)
    strict = same & (row > col)
    eye = (row == col).astype(F32)
    incl_b = jnp.where(incl, 1.0, 0.0).astype(BF16)
    gc_all = _dot_mask(incl_b, g_t)

    heads = range(GDN_HEADS)
    qs, ks, vs, bbs, gcs, decays = [], [], [], [], [], []
    for h in heads:
        lo = h * dk
        q = y[:, lo:lo + dk]
        k = y[:, hw + lo:hw + lo + dk]
        qs.append(q * lax.rsqrt(jnp.sum(q * q, axis=-1, keepdims=True) + EPS) * dk ** -0.5)
        ks.append(k * lax.rsqrt(jnp.sum(k * k, axis=-1, keepdims=True) + EPS))
        vs.append(y[:, 2 * hw + lo:2 * hw + lo + dk])
        bbs.append(jnp.broadcast_to(beta_t[:, SM_BA + h:SM_BA + h + 1], (rb, dk)))
        gc = jnp.broadcast_to(gc_all[:, SM_AA + h:SM_AA + h + 1], (rb, dk))
        gc_t = jnp.concatenate([gc[r:r + dk].T for r in range(0, rb, dk)], axis=1)[:1]
        gc_i = jnp.concatenate([gc] * (rb // dk), axis=1)
        decays.append(jnp.where(incl, jnp.exp(jnp.where(incl, gc_i - gc_t, 0.0)), 0.0))
        gcs.append(gc)
    kbetas = [ks[h] * bbs[h] for h in heads]
    ypows = [jnp.where(strict, -(_bdot_nt(kbetas[h], ks[h]) * decays[h]), 0.0) for h in heads]
    tmats = [eye + ypows[h] for h in heads]
    for _ in range(int(np.log2(c)) - 1):
        ypows = [_bdot(ypows[h], ypows[h]) for h in heads]
        tmats = [tmats[h] + _bdot(tmats[h], ypows[h]) for h in heads]
    egcs = [jnp.exp(gcs[h]) for h in heads]
    wus = [_bdot(tmats[h], jnp.concatenate([kbetas[h] * egcs[h], vs[h] * bbs[h]], axis=1))
           for h in heads]
    aqks = [_bdot_nt(qs[h], ks[h]) * decays[h] for h in heads]
    qgs = [qs[h] * egcs[h] for h in heads]
    sts = [state_ref[h] for h in heads]
    v_news = [[] for _ in heads]
    o_inter = [[] for _ in heads]
    for ci in range(nch):
        r0 = ci * c
        for h in heads:
            gl = gcs[h][r0 + c - 1:r0 + c, :]
            kd = ks[h][r0:r0 + c] * jnp.exp(gl - gcs[h][r0:r0 + c])
            ws = _bdot(jnp.concatenate([wus[h][r0:r0 + c, :dk], qgs[h][r0:r0 + c]], axis=0), sts[h])
            v_new = wus[h][r0:r0 + c, dk:] - ws[:c]
            o_inter[h].append(ws[c:])
            sts[h] = sts[h] * jnp.exp(gl) + _bdot_tn(kd, v_new)
            v_news[h].append(v_new)
    for h in heads:
        lo = h * dk
        state_ref[h] = sts[h]
        o = jnp.concatenate(o_inter[h], axis=0) + _bdot(aqks[h], jnp.concatenate(v_news[h], axis=0))
        o = o * lax.rsqrt(jnp.mean(o * o, axis=-1, keepdims=True) + EPS) * on_ref[...]
        o_ref[:, lo:lo + dk] = o * _silu(za[:, lo:lo + dk])


def gdn_mixer(proj, conv_w, a_log, dt_bias, out_norm, batch, seq):
    t = proj.shape[0]
    rb = 256
    nsb = seq // rb
    hw = GDN_HEADS * GDN_DK
    nega =jnp.zeros((1, LANES), F32).at[0, SM_AA:SM_AA + GDN_HEADS].set(-jnp.exp(a_log))
    dtb = jnp.zeros((1, LANES), F32).at[0, SM_AA:SM_AA + GDN_HEADS].set(dt_bias)
    row = lambda b, s: b * nsb + s
    return pl.pallas_call(
        functools.partial(_gdn_kernel, rb=rb),
        grid=(batch, nsb),
        in_specs=[pl.BlockSpec((rb, 3 * hw), lambda b, s: (row(b, s), 0)),
                  pl.BlockSpec((rb, hw), lambda b, s: (row(b, s), EV_ZA // hw)),
                  pl.BlockSpec((rb, LANES), lambda b, s: (row(b, s), EV_SM // LANES)),
                  pl.BlockSpec((CONV_WIDTH, 3 * hw), lambda b, s: (0, 0)),
                  pl.BlockSpec((1, LANES), lambda b, s: (0, 0)),
                  pl.BlockSpec((1, LANES), lambda b, s: (0, 0)),
                  pl.BlockSpec((1, GDN_DK), lambda b, s: (0, 0))],
        out_specs=pl.BlockSpec((rb, hw), lambda b, s: (row(b, s), 0)),
        out_shape=jax.ShapeDtypeStruct((t, hw), F32),
        scratch_shapes=[pltpu.VMEM((SUBLANES, 3 * hw), F32),
                        pltpu.VMEM((GDN_HEADS, GDN_DK, GDN_DK), F32)],
        compiler_params=_cparams(("arbitrary", "arbitrary")),
        name="gdn_mixer",
    )(proj, proj, proj, conv_w, nega, dtb, out_norm.reshape(1, GDN_DK))


def _gla_kernel(qk_ref, v_ref, r_ref, gl_ref, wg_ref, bg_ref, on_ref, o_ref, state_ref, *, rb):
    c = GLA_CHUNK
    dk, dv = GLA_DK, GLA_DV
    sb = GLA_SUB

    @pl.when(pl.program_id(1) == 0)
    def _():
        state_ref[...] = jnp.zeros_like(state_ref)

    z = _dot3(gl_ref[...], wg_ref[...]) + bg_ref[...]
    log_a = -_softplus(-z) * (1.0 / GLA_TAU)
    qk = qk_ref[...]
    vv = v_ref[...]
    rr = r_ref[...]
    incl, _, _ = _tri_masks(c)
    incl_b = incl.astype(F32).astype(BF16)
    hw = GLA_HEADS * dk
    sub_row = lax.broadcasted_iota(I32, (sb, dk), 0)
    att_lane = lax.broadcasted_iota(I32, (sb, c), 1)

    for ci in range(rb // c):
        r0 = ci * c
        b_all = _dot_mask(incl_b, log_a[r0:r0 + c])
        for h in range(GLA_HEADS):
            q = qk[r0:r0 + c, h * dk:(h + 1) * dk] * dk ** -0.5
            k = qk[r0:r0 + c, hw + h * dk:hw + (h + 1) * dk]
            v = vv[r0:r0 + c, h * dv:(h + 1) * dv]
            b = b_all[:, h * dk:(h + 1) * dk]
            st = state_ref[h]
            o = _bdot_nt(q * jnp.exp(b), st)
            att_rows = []
            for bi in range(c // sb):
                s0 = bi * sb
                qi = q[s0:s0 + sb]
                bi_ = b[s0:s0 + sb]
                ref = b[s0:s0 + 1]
                if bi > 0:
                    qt = qi * jnp.exp(bi_ - ref)
                    kt = k * jnp.exp(jnp.minimum(ref - b, 0.0))
                    att = jnp.where(att_lane < s0, _bdot_nt(qt, kt), 0.0)
                else:
                    att = jnp.zeros((sb, c), F32)
                for j in range(sb):
                    kj = k[s0 + j:s0 + j + 1]
                    bj = b[s0 + j:s0 + j + 1]
                    e = jnp.exp(jnp.where(sub_row >= j, bi_ - bj, NEG_INF))
                    col = jnp.sum(qi * kj * e, axis=-1, keepdims=True)
                    att = jnp.where(att_lane == s0 + j, col, att)
                att_rows.append(att)
            o = o + _bdot(jnp.concatenate(att_rows, axis=0), v)
            bl = b[c - 1:c]
            state_ref[h] = st * jnp.exp(bl) + _bdot_tn(v, k * jnp.exp(bl - b))
            o = o * lax.rsqrt(jnp.mean(o * o, axis=-1, keepdims=True) + EPS) * on_ref[...]
            o_ref[r0:r0 + c, h * dv:(h + 1) * dv] = o * _silu(rr[r0:r0 + c, h * dv:(h + 1) * dv])


def gla_mixer(proj, w_gate2, b_gate2, out_norm, batch, seq):
    t = proj.shape[0]
    rb = 2 * GLA_CHUNK
    nsb = seq // rb
    hw = GLA_HEADS * GLA_DK
    hv = GLA_HEADS * GLA_DV
    wg = jnp.zeros((LANES, hw), F32).at[:GLA_RANK].set(w_gate2)
    row = lambda b, s: b * nsb + s
    return pl.pallas_call(
        functools.partial(_gla_kernel, rb=rb),
        grid=(batch, nsb),
        in_specs=[pl.BlockSpec((rb, 2 * hw), lambda b, s: (row(b, s), 0)),
                  pl.BlockSpec((rb, hv), lambda b, s: (row(b, s), OD_V // hv)),
                  pl.BlockSpec((rb, hv), lambda b, s: (row(b, s), OD_R // hv)),
                  pl.BlockSpec((rb, LANES), lambda b, s: (row(b, s), OD_G // LANES)),
                  pl.BlockSpec((LANES, hw), lambda b, s: (0, 0)),
                  pl.BlockSpec((1, hw), lambda b, s: (0, 0)),
                  pl.BlockSpec((1, GLA_DV), lambda b, s: (0, 0))],
        out_specs=pl.BlockSpec((rb, hv), lambda b, s: (row(b, s), 0)),
        out_shape=jax.ShapeDtypeStruct((t, hv), F32),
        scratch_shapes=[pltpu.VMEM((GLA_HEADS, GLA_DV, GLA_DK), F32)],
        compiler_params=_cparams(("arbitrary", "arbitrary")),
        name="gla_mixer",
    )(proj, proj, proj, proj, wg, b_gate2.reshape(1, hw), out_norm.reshape(1, GLA_DV))


def _rope_table_kernel(pos_ref, inva_ref, invi_ref, ca_ref, sa_ref, ci_ref, si_ref):
    pos = pos_ref[...].astype(F32)
    lane = lax.broadcasted_iota(I32, (pos.shape[0], LANES), 1)
    ang_a = pos * inva_ref[...]
    ca_ref[...] = jnp.cos(ang_a)
    sa_ref[...] = jnp.where(lane < ATT_DH // 2, -1.0, 1.0) * jnp.sin(ang_a)
    ang_i = pos * invi_ref[...]
    ci_ref[...] = jnp.cos(ang_i)
    si_ref[...] = jnp.where(lane % IDX_DH < IDX_DH // 2, -1.0, 1.0) * jnp.sin(ang_i)


def rope_tables(positions):
    t = positions.size
    inv_a = 1.0 / (ROPE_THETA ** (jnp.arange(0, ATT_DH, 2, dtype=F32) / ATT_DH))
    inv_i = 1.0 / (ROPE_THETA ** (jnp.arange(0, IDX_DH, 2, dtype=F32) / IDX_DH))
    inv_a = jnp.tile(inv_a, 2).reshape(1, LANES)
    inv_i = jnp.tile(inv_i, 4).reshape(1, LANES)
    tm = 512
    spec = pl.BlockSpec((tm, LANES), lambda i: (i, 0))
    vec = pl.BlockSpec((1, LANES), lambda i: (0, 0))
    return pl.pallas_call(
        _rope_table_kernel,
        grid=(t // tm,),
        in_specs=[pl.BlockSpec((tm, 1), lambda i: (i, 0)), vec, vec],
        out_specs=[spec] * 4,
        out_shape=[jax.ShapeDtypeStruct((t, LANES), F32)] * 4,
        compiler_params=_cparams(("arbitrary",)),
        name="rope_tables",
    )(positions.reshape(t, 1), inv_a, inv_i)


def _dsa_prep_kernel(qb_ref, kb_ref, vb_ref, qi_ref, sm_ref, ca_ref, sa_ref, ci_ref, si_ref,
                     qo_ref, ko_ref, vt_ref, qio_ref, kio_ref, kn_ref):
    ca, sa, ci, si = ca_ref[...], sa_ref[...], ci_ref[...], si_ref[...]
    lane = lax.broadcasted_iota(I32, ca.shape, 1)
    first_half = lane % IDX_DH < IDX_DH // 2
    knorm = jnp.zeros((SUBLANES, LANES), F32)
    lane8 = lax.broadcasted_iota(I32, (SUBLANES, LANES), 1)

    def rope_att(tile):
        return tile * ca + pltpu.roll(tile, ATT_DH // 2, axis=1) * sa

    def rope_idx(tile):
        partner = jnp.where(first_half, pltpu.roll(tile, LANES - IDX_DH // 2, axis=1),
                            pltpu.roll(tile, IDX_DH // 2, axis=1))
        return tile * ci + partner * si

    qb, kb = qb_ref[...], kb_ref[...]
    for h in range(ATT_HEADS):
        sl = slice(h * ATT_DH, (h + 1) * ATT_DH)
        qo_ref[:, sl] = (rope_att(qb[:, sl]) * ATT_DH ** -0.5).astype(BF16)
        kr = rope_att(kb[:, sl]).astype(BF16)
        ko_ref[:, sl] = kr
        krf = kr.astype(F32)
        knorm = jnp.where(lane8 == h, jnp.max(jnp.sum(krf * krf, axis=1, keepdims=True)), knorm)
    kn_ref[0] = knorm
    vt_ref[0] = vb_ref[...].T.astype(BF16)
    qi = qi_ref[...]
    for p in range(IDX_HEADS * IDX_DH // LANES):
        sl = slice(p * LANES, (p + 1) * LANES)
        qio_ref[:, sl] = rope_idx(qi[:, sl]).astype(BF16)
    ki = rope_idx(sm_ref[...])
    kio_ref[...] = jnp.where(lane < IDX_DH, ki, pltpu.roll(ki, IDX_DH, axis=1)).astype(BF16)


def dsa_prep(proj, tables, seq):
    t = proj.shape[0]
    tm = min(DSA_KEY_CHUNK, seq)
    aw = ATT_HEADS * ATT_DH
    iw = IDX_HEADS * IDX_DH
    tab = pl.BlockSpec((tm, LANES), lambda i: (i, 0))
    return pl.pallas_call(
        _dsa_prep_kernel,
        grid=(t // tm,),
        in_specs=[pl.BlockSpec((tm, aw), lambda i: (i, EV_QB // aw)),
                  pl.BlockSpec((tm, aw), lambda i: (i, EV_KB // aw)),
                  pl.BlockSpec((tm, aw), lambda i: (i, EV_VB // aw)),
                  pl.BlockSpec((tm, iw), lambda i: (i, EV_QI // iw)),
                  pl.BlockSpec((tm, LANES), lambda i: (i, EV_SM // LANES)),
                  tab, tab, tab, tab],
        out_specs=[pl.BlockSpec((tm, aw), lambda i: (i, 0)),
                   pl.BlockSpec((tm, aw), lambda i: (i, 0)),
                   pl.BlockSpec((1, aw, tm), lambda i: (i, 0, 0)),
                   pl.BlockSpec((tm, iw), lambda i: (i, 0)),
                   pl.BlockSpec((tm, LANES), lambda i: (i, 0)),
                   pl.BlockSpec((1, SUBLANES, LANES), lambda i: (i, 0, 0))],
        out_shape=[jax.ShapeDtypeStruct((t, aw), BF16),
                   jax.ShapeDtypeStruct((t, aw), BF16),
                   jax.ShapeDtypeStruct((t // tm, aw, tm), BF16),
                   jax.ShapeDtypeStruct((t, iw), BF16),
                   jax.ShapeDtypeStruct((t, LANES), BF16),
                   jax.ShapeDtypeStruct((t // tm, SUBLANES, LANES), F32)],
        compiler_params=_cparams(("arbitrary",)),
        name="dsa_prep",
    )(proj, proj, proj, proj, proj, *tables)


def _fold_rows(x, op):
    parts = [x[r:r + SUBLANES] for r in range(0, x.shape[0], SUBLANES)]
    while len(parts) > 1:
        parts = [op(parts[j], parts[j + 1]) for j in range(0, len(parts), 2)]
    return parts[0]


def _dsa_kernel(qb_ref, qi_ref, sm_ref, kn_ref, k_ref, vt_ref, ki_ref, o_ref,
                key_ref, l_ref, acc_ref, *, kc, topk):
    nq = Q_BLOCK
    i = pl.program_id(1)
    nkc = (i * nq + nq + kc - 1) // kc
    q_pos = i * nq + lax.broadcasted_iota(I32, (1, nq), 1)
    key0 = lax.broadcasted_iota(I32, (kc, nq), 0)
    lane = lax.broadcasted_iota(I32, (nq, LANES), 1)
    wi_t = sm_ref[...].T * (IDX_HEADS * IDX_DH) ** -0.5
    qi = qi_ref[...]
    zero_b = jnp.zeros((nq, LANES), BF16)
    qi_heads = []
    for h in range(IDX_HEADS):
        pair = qi[:, (h // 2) * LANES:(h // 2 + 1) * LANES]
        own = (lane < IDX_DH) if h % 2 == 0 else (lane >= IDX_DH)
        qi_heads.append(jnp.where(own, pair, zero_b))

    def score_body(c, carry):
        start = pl.multiple_of(c * kc, kc)
        kic = ki_ref[0, pl.ds(start, kc), :]
        dots = [_dot_nt(kic, qi_heads[h]) for h in range(IDX_HEADS)]
        s = jnp.zeros((kc, nq), F32)
        for h in range(IDX_HEADS):
            s = s + wi_t[SM_WI + h:SM_WI + h + 1, :] * jnp.maximum(dots[h], 0.0)
        s = s + 0.0
        s = jnp.where(key0 <= q_pos - c * kc, s, NEG_INF)
        bits = pltpu.bitcast(s, I32)
        key_ref[c] = bits ^ ((bits >> 31) & 0x7FFFFFFF)
        return carry

    lax.fori_loop(0, nkc, score_body, 0)

    halves = [slice(j * LANES, (j + 1) * LANES) for j in range(nq // LANES)]
    key0h = key0[:, :LANES]
    zero8 = jnp.zeros((SUBLANES, LANES), I32)

    def count(fns, *rows):
        def body(c, accs):
            out = []
            for j, sl in enumerate(halves):
                key = key_ref[c, :, sl]
                args = [r[:, sl] for r in rows]
                for f, fn in enumerate(fns):
                    hit = jnp.where(fn(key, c, *args), 1, 0).astype(I32)
                    out.append(accs[j * len(fns) + f] + _fold_rows(hit, jnp.add))
            return tuple(out)

        accs = lax.fori_loop(0, nkc, body, (zero8,) * (len(halves) * len(fns)))
        return [jnp.concatenate([jnp.sum(accs[j * len(fns) + f], axis=0, keepdims=True)
                                 for j in range(len(halves))], axis=1) for f in range(len(fns))]

    thr = jnp.where(count([lambda key, c: key >= 0])[0] >= topk, 0, INT_MIN).astype(I32)

    def bit_body(it, thr):
        cand = thr + (jnp.int32(1) << (30 - it))
        return jnp.where(count([lambda key, c, cd: key >= cd], cand)[0] >= topk, cand, thr)

    thr = lax.fori_loop(0, 31, bit_body, thr)

    n_gt, n_eq = count([lambda key, c, t: key > t, lambda key, c, t: key == t], thr)
    need = topk - n_gt
    big = jnp.int32(2 ** 30)

    def tie_search():
        def idx_body(it, x):
            cand = x + (jnp.int32(1) << (14 - it))
            below = count([lambda key, c, t, cd: key0h < jnp.where(key == t, cd - c * kc, 0)],
                          thr, cand)[0]
            return jnp.where(below < need, cand, x)
        return lax.fori_loop(0, 15, idx_body, jnp.zeros((1, nq), I32))

    any_tie = jnp.max(jnp.where(n_eq > need, 1, 0)) > 0
    last = lax.cond(any_tie, tie_search, lambda: jnp.full((1, nq), big, I32))

    last = jnp.where(thr == KEY_NEG_INF, -1, last)
    zero_f = jnp.float32(0.0)
    ninf_f = jnp.float32(NEG_INF)
    qb = qb_ref[...]

    def select_bias(c):
        key = key_ref[c]
        tie_ok = jnp.where(key0 <= last - c * kc, zero_f, ninf_f)
        return jnp.where(key > thr, zero_f, jnp.where(key == thr, tie_ok, ninf_f))

    def chunk_logits(c, h, bias):
        start = pl.multiple_of(c * kc, kc)
        kch = k_ref[0, pl.ds(start, kc), h * ATT_DH:(h + 1) * ATT_DH]
        return _dot_nt(kch, qb[:, h * ATT_DH:(h + 1) * ATT_DH]) + bias

    def attend(shifts):
        l_ref[...] = jnp.zeros(l_ref.shape, F32)
        acc_ref[...] = jnp.zeros(acc_ref.shape, F32)

        def body(c, carry):
            bias = select_bias(c)
            heads = range(ATT_HEADS)
            logits = [chunk_logits(c, h, bias) for h in heads]
            ps = [jnp.exp(logits[h] - shifts[h]) for h in heads]
            for h in heads:
                l_ref[h] = l_ref[h] + _fold_rows(ps[h], jnp.add)
                vth = vt_ref[0, c, h * ATT_DH:(h + 1) * ATT_DH, :]
                acc_ref[h] = acc_ref[h] + _dot(vth, ps[h].astype(BF16))
            return carry

        lax.fori_loop(0, nkc, body, 0)
        return [jnp.sum(l_ref[h], axis=0, keepdims=True) for h in range(ATT_HEADS)]

    ones_b = jnp.ones((SUBLANES, ATT_DH), BF16)
    bounds = []
    for h in range(ATT_HEADS):
        qh = qb[:, h * ATT_DH:(h + 1) * ATT_DH].astype(F32)
        qsq = _dot_nt(ones_b, (qh * qh).astype(BF16))[0:1, :]
        bounds.append(jnp.sqrt(qsq * kn_ref[0][0:1, h:h + 1]) * 1.02)
    sums = attend(bounds)

    def write_out(sums):
        for h in range(ATT_HEADS):
            o_ref[:, h * ATT_DH:(h + 1) * ATT_DH] = (acc_ref[h] / sums[h]).T

    write_out(sums)
    lowest = jnp.minimum(jnp.minimum(sums[0], sums[1]), jnp.minimum(sums[2], sums[3]))

    @pl.when(jnp.min(lowest) <= 0.0)
    def _():
        l_ref[...] = jnp.full(l_ref.shape, NEG_INF, F32)

        def max_body(c, carry):
            bias = select_bias(c)
            for h in range(ATT_HEADS):
                l_ref[h] = jnp.maximum(l_ref[h], _fold_rows(chunk_logits(c, h, bias), jnp.maximum))
            return carry

        lax.fori_loop(0, nkc, max_body, 0)
        exact = [jnp.max(l_ref[h], axis=0, keepdims=True) for h in range(ATT_HEADS)]
        write_out(attend(exact))


def dsa_attention(proj, qb, kb, vt, qi, ki, knorm, batch, seq):
    t = proj.shape[0]
    aw = ATT_HEADS * ATT_DH
    iw = IDX_HEADS * IDX_DH
    kc = min(DSA_KEY_CHUNK, seq)
    nkc = seq // kc
    topk = min(TOPK_MAX, seq // 4)
    nq = seq // Q_BLOCK
    row = lambda b, i: b * nq + i
    kb3, ki3 = (a.reshape(batch, seq, a.shape[1]) for a in (kb, ki))
    vt4 = vt.reshape(batch, nkc, aw, kc)
    kmax = jnp.max(knorm.reshape(batch, nkc, SUBLANES, LANES), axis=1)
    resident = lambda shape: pl.BlockSpec((1,) + shape, lambda b, i: (b,) + (0,) * len(shape),
                                          pipeline_mode=pl.Buffered(1))
    return pl.pallas_call(
        functools.partial(_dsa_kernel, kc=kc, topk=topk),
        grid=(batch, nq),
        in_specs=[pl.BlockSpec((Q_BLOCK, aw), lambda b, i: (row(b, i), 0)),
                  pl.BlockSpec((Q_BLOCK, iw), lambda b, i: (row(b, i), 0)),
                  pl.BlockSpec((Q_BLOCK, LANES), lambda b, i: (row(b, i), EV_SM // LANES)),
                  pl.BlockSpec((1, SUBLANES, LANES), lambda b, i: (b, 0, 0)),
                  resident((seq, aw)), resident((nkc, aw, kc)), resident((seq, LANES))],
        out_specs=pl.BlockSpec((Q_BLOCK, aw), lambda b, i: (row(b, i), 0)),
        out_shape=jax.ShapeDtypeStruct((t, aw), F32),
        scratch_shapes=[pltpu.VMEM((nkc, kc, Q_BLOCK), I32),
                        pltpu.VMEM((ATT_HEADS, SUBLANES, Q_BLOCK), F32),
                        pltpu.VMEM((ATT_HEADS, ATT_DH, Q_BLOCK), F32)],
        compiler_params=_cparams(("arbitrary", "arbitrary")),
        name="dsa_attention",
    )(qb, qi, proj, kmax, kb3, vt4, ki3)


def _router_kernel(x_ref, g_ref, sh_ref, sc_ref, wr_ref, br_ref, h_ref, route_ref, cnt_ref,
                   carry_ref):
    @pl.when(pl.program_id(0) == 0)
    def _():
        carry_ref[...] = jnp.zeros_like(carry_ref)

    h = _norm_mod(x_ref[...], g_ref[...], sh_ref[0], sc_ref[0])
    h_ref[...] = h
    tm = h.shape[0]
    logits = _dot3(h, wr_ref[...]) + br_ref[...]
    lane = lax.broadcasted_iota(I32, (tm, LANES), 1)
    far = jnp.int32(LANES)

    def first_max(vals):
        m = jnp.max(vals, axis=1, keepdims=True)
        return m, jnp.min(jnp.where(vals == m, lane, far), axis=1, keepdims=True)

    is_group = lane < N_GROUPS
    gmax, gsel = first_max(jnp.where(is_group, logits, NEG_INF))
    gsum = jnp.sum(jnp.where(is_group, jnp.exp(logits - gmax), 0.0), axis=1, keepdims=True)
    lo = N_GROUPS + EXPERTS_PER_GROUP * gsel
    el = jnp.where((lane >= lo) & (lane < lo + EXPERTS_PER_GROUP), logits, NEG_INF)
    m1, i1 = first_max(el)
    m2, i2 = first_max(jnp.where(lane == i1, NEG_INF, el))
    e2 = jnp.exp(m2 - m1)
    gate1 = 1.0 / (gsum * (1.0 + e2))
    gate2 = e2 * gate1
    eid1 = i1 - N_GROUPS
    eid2 = i2 - N_GROUPS
    oh1 = lane == eid1
    oh2 = lane == eid2
    onehot = jnp.where(oh1 | oh2, 1.0, 0.0)
    r = lax.broadcasted_iota(I32, (tm, tm), 0)
    cidx = lax.broadcasted_iota(I32, (tm, tm), 1)
    below = jnp.where(r > cidx, 1.0, 0.0).astype(BF16)
    cum = _dot(below, onehot.astype(BF16)) + carry_ref[0:1, :]
    rank1 = jnp.sum(jnp.where(oh1, cum, 0.0), axis=1, keepdims=True)
    rank2 = jnp.sum(jnp.where(oh2, cum, 0.0), axis=1, keepdims=True)
    carry_ref[...] = carry_ref[...] + jnp.sum(onehot, axis=0, keepdims=True)
    cnt_ref[...] = carry_ref[...]
    out = jnp.zeros((tm, LANES), F32)
    for j, val in enumerate((eid1.astype(F32), eid2.astype(F32), gate1, gate2, rank1, rank2)):
        out = jnp.where(lane == j, val, out)
    route_ref[...] = out


def moe_router(x, g, shift, scale, wr, br, seq):
    t, d = x.shape
    tm = 512
    per_b = seq // tm
    return pl.pallas_call(
        _router_kernel,
        grid=(t // tm,),
        in_specs=[pl.BlockSpec((tm, d), lambda i: (i, 0)),
                  pl.BlockSpec((1, d), lambda i: (0, 0)),
                  pl.BlockSpec((1, 1, d), lambda i: (i // per_b, 0, 0)),
                  pl.BlockSpec((1, 1, d), lambda i: (i // per_b, 0, 0)),
                  pl.BlockSpec((d, LANES), lambda i: (0, 0)),
                  pl.BlockSpec((1, LANES), lambda i: (0, 0))],
        out_specs=[pl.BlockSpec((tm, d), lambda i: (i, 0)),
                   pl.BlockSpec((tm, LANES), lambda i: (i, 0)),
                   pl.BlockSpec((SUBLANES, LANES), lambda i: (0, 0))],
        out_shape=[jax.ShapeDtypeStruct((t, d), F32),
                   jax.ShapeDtypeStruct((t, LANES), F32),
                   jax.ShapeDtypeStruct((SUBLANES, LANES), F32)],
        scratch_shapes=[pltpu.VMEM((SUBLANES, LANES), F32)],
        compiler_params=_cparams(("arbitrary",)),
        name="moe_router",
    )(x, g, shift, scale, wr, br)


def _zero_kernel(o_ref):
    o_ref[...] = jnp.zeros_like(o_ref)


def zero_rows(rows, d):
    tm = 1024
    return pl.pallas_call(
        _zero_kernel,
        grid=(rows // tm,),
        out_specs=pl.BlockSpec((tm, d), lambda i: (i, 0)),
        out_shape=jax.ShapeDtypeStruct((rows, d), F32),
        compiler_params=_cparams(("arbitrary",)),
        name="zero_rows",
    )()


def _dispatch_kernel(dest_ref, seg_ref, h_ref, xs_in_ref, xs_ref, inv_ref, sem, *, tm):
    del xs_in_ref
    i = pl.program_id(0)
    base = i * tm * 2

    rows = MOE_ROWS
    n_assign = dest_ref.shape[0]

    @pl.when(i == 0)
    def _():
        def mark(p, carry):
            blk = p // rows
            spare = ((blk + MOE_OUT_BUFS - 1) % MOE_OUT_BUFS) * rows + (p - blk * rows)
            inv_ref[p] = n_assign + spare
            return carry
        lax.fori_loop(0, rows, mark, 0)
        for e in range(N_EXPERTS):
            lax.fori_loop(rows + seg_ref[e], rows + seg_ref[N_EXPERTS + e], mark, 0)
        lax.fori_loop(rows + seg_ref[2 * N_EXPERTS - 1], inv_ref.shape[0], mark, 0)

    def row_copy(r, d):
        return pltpu.make_async_copy(h_ref.at[pl.ds(r, 1)], xs_ref.at[pl.ds(d, 1)], sem)

    n_tok = n_assign // 2

    group = 4

    def start(g, carry):
        r0 = g * group
        ds = [[dest_ref[base + 2 * (r0 + k) + j] for j in range(2)] for k in range(group)]
        for k in range(group):
            for j in range(2):
                row_copy(r0 + k, ds[k][j]).start()
        for k in range(group):
            for j in range(2):
                inv_ref[rows + ds[k][j]] = j * n_tok + i * tm + r0 + k
        return carry

    def wait(r, carry):
        row_copy(0, 0).wait()
        row_copy(0, 0).wait()
        return carry

    lax.fori_loop(0, tm // group, start, 0)
    lax.fori_loop(0, tm, wait, 0)


def moe_dispatch(dest, seg, h, xs_zero):
    t, d = h.shape
    tm = 256
    return pl.pallas_call(
        functools.partial(_dispatch_kernel, tm=tm),
        grid_spec=pltpu.PrefetchScalarGridSpec(
            num_scalar_prefetch=2,
            grid=(t // tm,),
            in_specs=[pl.BlockSpec((tm, d), lambda i, dest, seg: (i, 0)),
                      pl.BlockSpec(memory_space=pl.ANY)],
            out_specs=[pl.BlockSpec(memory_space=pl.ANY),
                       pl.BlockSpec(memory_space=pltpu.SMEM)],
            scratch_shapes=[pltpu.SemaphoreType.DMA(())]),
        out_shape=[jax.ShapeDtypeStruct(xs_zero.shape, F32),
                   jax.ShapeDtypeStruct((xs_zero.shape[0] + MOE_ROWS,), I32)],
        input_output_aliases={3: 0},
        compiler_params=_cparams(("arbitrary",)),
        name="moe_dispatch",
    )(dest, seg, h, xs_zero)


MOE_OUT_BUFS = 3


def _expert_kernel(be_ref, inv_ref, xs_ref, w1_ref, w3_ref, w2_ref, y_ref,
                   w1b, w3b, w2b, ybuf, sems, *, n_assign):
    i = pl.program_id(0)
    last = pl.num_programs(0) - 1
    rows = ybuf.shape[1]
    slot = i % MOE_OUT_BUFS
    prev_slot = (i + MOE_OUT_BUFS - 1) % MOE_OUT_BUFS
    prev = be_ref[jnp.maximum(i - 1, 0)]

    def row_copy(s, r, dst):
        return pltpu.make_async_copy(ybuf.at[s, pl.ds(r, 1)], y_ref.at[pl.ds(dst, 1)], sems.at[s])

    def send(s, block):
        for r in range(rows):
            row_copy(s, r, inv_ref[(block + 1) * rows + r]).start()

    def drain(s):
        for r in range(rows):
            row_copy(s, 0, 0).wait()

    @pl.when(i == 0)
    def _():
        ybuf[prev_slot] = jnp.zeros((rows, ybuf.shape[2]), F32)
        for s in range(MOE_OUT_BUFS):
            if s != MOE_OUT_BUFS - 1:
                for r in range(rows):
                    row_copy(prev_slot, r, n_assign + s * rows + r).start()
                drain(prev_slot)

    @pl.when(i >= MOE_OUT_BUFS - 1)
    def _():
        drain(slot)

    @pl.when((i == 0) | (be_ref[i] != prev))
    def _():
        w1b[...] = w1_ref[0, 0].astype(BF16)
        w3b[...] = w3_ref[0, 0].astype(BF16)
        w2b[...] = w2_ref[0, 0].astype(BF16)

    send(prev_slot, i - 1)
    x = xs_ref[...].astype(BF16)
    a = _dot(x, w1b[...])
    hid = _silu(a) * _dot(x, w3b[...])
    ybuf[slot] = _dot(hid.astype(BF16), w2b[...])

    @pl.when(i == last)
    def _():
        send(slot, i)
        for s in range(MOE_OUT_BUFS):
            drain(s)


def moe_experts(blk_exp, inv, xs, w1, w3, w2, layer, t):
    p, d = xs.shape
    de = w1.shape[3]
    rows = MOE_ROWS
    wmap = lambda i, be, inv: (layer, be[i], 0, 0)
    return pl.pallas_call(
        functools.partial(_expert_kernel, n_assign=2 * t),
        grid_spec=pltpu.PrefetchScalarGridSpec(
            num_scalar_prefetch=2,
            grid=(p // rows,),
            in_specs=[pl.BlockSpec((rows, d), lambda i, be, inv: (i, 0)),
                      pl.BlockSpec((1, 1, d, de), wmap),
                      pl.BlockSpec((1, 1, d, de), wmap),
                      pl.BlockSpec((1, 1, de, d), wmap)],
            out_specs=pl.BlockSpec(memory_space=pl.ANY),
            scratch_shapes=[pltpu.VMEM((d, de), BF16), pltpu.VMEM((d, de), BF16),
                            pltpu.VMEM((de, d), BF16), pltpu.VMEM((MOE_OUT_BUFS, rows, d), F32),
                            pltpu.SemaphoreType.DMA((MOE_OUT_BUFS,))]),
        out_shape=jax.ShapeDtypeStruct((2 * t + MOE_OUT_BUFS * rows, d), F32),
        compiler_params=_cparams(("arbitrary",)),
        name="moe_experts",
    )(blk_exp, inv, xs, w1, w3, w2)


def _combine_kernel(y1_ref, y2_ref, route_ref, x_ref, gt_ref, *rest, final):
    route = route_ref[...]
    y = route[:, 2:3] * y1_ref[...] + route[:, 3:4] * y2_ref[...]
    out = x_ref[...] + gt_ref[0] * y
    if final:
        fn_ref, o_ref = rest
        out = out * lax.rsqrt(jnp.mean(out * out, axis=-1, keepdims=True) + EPS) * fn_ref[...]
    else:
        (o_ref,) = rest
    o_ref[...] = out


def moe_combine(y2, route, x, gate, seq, final_gain=None):
    t, d = x.shape
    tm = 512
    per_b = seq // tm
    final = final_gain is not None
    nt = t // tm
    in_specs = [pl.BlockSpec((tm, d), lambda i: (i, 0)),
                pl.BlockSpec((tm, d), lambda i: (nt + i, 0)),
                pl.BlockSpec((tm, LANES), lambda i: (i, 0)),
                pl.BlockSpec((tm, d), lambda i: (i, 0)),
                pl.BlockSpec((1, 1, d), lambda i: (i // per_b, 0, 0))]
    args = [y2, y2, route, x, gate]
    if final:
        in_specs.append(pl.BlockSpec((1, d), lambda i: (0, 0)))
        args.append(final_gain)
    return pl.pallas_call(
        functools.partial(_combine_kernel, final=final),
        grid=(t // tm,),
        in_specs=in_specs,
        out_specs=pl.BlockSpec((tm, d), lambda i: (i, 0)),
        out_shape=jax.ShapeDtypeStruct((t, d), F32),
        compiler_params=_cparams(("arbitrary",)),
        name="moe_combine",
    )(*args)


def hier_moe_layer(x, g, shift, scale, gate, wg, bg, we, be, w1, w3, w2, layer, seq,
                   final_gain=None):
    t, d = x.shape
    wr = jnp.zeros((d, LANES), F32)
    wr = wr.at[:, :N_GROUPS].set(wg)
    wr = wr.at[:, N_GROUPS:N_GROUPS + N_EXPERTS].set(we.transpose(1, 0, 2).reshape(d, N_EXPERTS))
    br = jnp.zeros((1, LANES), F32)
    br = br.at[0, :N_GROUPS].set(bg).at[0, N_GROUPS:N_GROUPS + N_EXPERTS].set(be.reshape(-1))
    h, route, cnt = moe_router(x, g, shift, scale, wr, br, seq)
    counts = cnt[0, :N_EXPERTS].astype(I32)
    padded = ((counts + MOE_ROWS - 1) // MOE_ROWS) * MOE_ROWS
    pad_end = jnp.cumsum(padded)
    pad_start = pad_end - padded
    eid = route[:, 0:2].astype(I32)
    rank = route[:, 4:6].astype(I32)
    experts = jnp.arange(N_EXPERTS, dtype=I32)
    start_of = jnp.sum(jnp.where(eid[:, :, None] == experts, pad_start, 0), axis=-1)
    dest = (start_of + rank).reshape(-1)
    p_rows = 2 * t + N_EXPERTS * MOE_ROWS
    nblk = p_rows // MOE_ROWS
    blk_row = jnp.arange(nblk, dtype=I32)[:, None] * MOE_ROWS
    blk_exp = jnp.minimum(jnp.sum((pad_end[None, :] <= blk_row).astype(I32), axis=1),
                          N_EXPERTS - 1)
    seg = jnp.concatenate([pad_start + counts, pad_end]).astype(I32)
    xs, inv = moe_dispatch(dest, seg, h, zero_rows(p_rows, d))
    y2 = moe_experts(blk_exp, inv, xs, w1, w3, w2, layer, t)
    return moe_combine(y2, route, x, gate, seq, final_gain)


def _even_weight(w_in):
    d = w_in.shape[0]
    sizes = [512, 512, 512, 512, 4, 4, 512, 512, 512, 256, 64, 4]
    offs = np.concatenate([[0], np.cumsum(sizes)])
    qa, ka, va, za, ba, aa, qb, kb, vb, qi, ki, wi = (w_in[:, offs[j]:offs[j + 1]]
                                                      for j in range(len(sizes)))
    pad = jnp.zeros((d, EV_N - EV_SM - 76), w_in.dtype)
    return jnp.concatenate([qa, ka, va, za, qb, kb, vb, qi, ki, ba, aa, wi, pad], axis=1)


def kernel(x, c, positions, ada_w, ada_b, norm_mix, norm_ffn, even_w_in, gdn_conv_w, gdn_a_log, gdn_dt_bias, gdn_out_norm, even_w_out, gla_w_in, gla_w_gate2, gla_b_gate2, gla_out_norm, gla_w_out, router_group_w, router_group_b, router_exp_w, router_exp_b, exp_w1, exp_w3, exp_w2, final_norm):
    batch, seq, d = x.shape
    depth = ada_w.shape[0]
    t = batch * seq
    xt = x.reshape(t, d)
    mod = ada_modulation(c, ada_w, ada_b)
    tables = rope_tables(positions)
    for layer in range(depth):
        sh_m, sc_m, gt_m, sh_f, sc_f, gt_f = (mod[layer, :, j * d:(j + 1) * d].reshape(batch, 1, d)
                                              for j in range(6))
        i = layer // 2
        g_mix = norm_mix[layer].reshape(1, d)
        if layer % 2 == 0:
            w_in = _even_weight(even_w_in[i]).astype(BF16)
            proj = in_projection(xt, g_mix, sh_m, sc_m, w_in, seq)
            oa = gdn_mixer(proj, gdn_conv_w[i], gdn_a_log[i], gdn_dt_bias[i], gdn_out_norm[i],
                           batch, seq)
            qb, kb, vt, qi, ki, knorm = dsa_prep(proj, tables, seq)
            ob = dsa_attention(proj, qb, kb, vt, qi, ki, knorm, batch, seq)
            w_out = even_w_out[i].astype(BF16)
            half = GDN_HEADS * GDN_DK
            xt = out_projection([oa, ob], [w_out[:half], w_out[half:]], xt, gt_m, seq)
        else:
            w_in = jnp.pad(gla_w_in[i], ((0, 0), (0, OD_N - gla_w_in.shape[2]))).astype(BF16)
            proj = in_projection(xt, g_mix, sh_m, sc_m, w_in, seq)
            o = gla_mixer(proj, gla_w_gate2[i], gla_b_gate2[i], gla_out_norm[i], batch, seq)
            xt = out_projection([o], [gla_w_out[i].astype(BF16)], xt, gt_m, seq)
        xt = hier_moe_layer(xt, norm_ffn[layer].reshape(1, d), sh_f, sc_f, gt_f,
                            router_group_w[layer], router_group_b[layer],
                            router_exp_w[layer], router_exp_b[layer],
                            exp_w1, exp_w3, exp_w2, layer, seq,
                            final_norm.reshape(1, d) if layer == depth - 1 else None)
    return xt.reshape(batch, seq, d)
```

```python
import functools

import jax
import jax.numpy as jnp
import numpy as np
from jax import lax
from jax.experimental import pallas as pl
from jax.experimental.pallas import tpu as pltpu

F32 = jnp.float32
BF16 = jnp.bfloat16
I32 = jnp.int32
HI = lax.Precision.HIGHEST

LANES = 128
SUBLANES = 8
VMEM_LIMIT = 56 * 1024 * 1024

EPS = 1e-6
ROPE_THETA = 10000.0
GDN_HEADS, GDN_DK = 4, 128
CONV_WIDTH = 4
GDN_CHUNK = 64
ATT_HEADS, ATT_DH = 4, 128
IDX_HEADS, IDX_DH = 4, 64
TOPK_MAX = 256
Q_BLOCK = 256
DSA_KEY_CHUNK = 512
DSA_SUM_ROWS = 16
DSA_VT_ROWS = ATT_DH + DSA_SUM_ROWS
GLA_HEADS, GLA_DK, GLA_DV = 4, 128, 256
GLA_RANK = 16
GLA_TAU = 16.0
GLA_CHUNK = 64
GLA_SUB = 16
N_GROUPS, EXPERTS_PER_GROUP = 4, 8
N_EXPERTS = N_GROUPS * EXPERTS_PER_GROUP
MOE_ROWS = 128

EV_QA, EV_KA, EV_VA, EV_ZA = 0, 512, 1024, 1536
EV_QB, EV_KB, EV_VB, EV_QI, EV_SM = 2048, 2560, 3072, 3584, 3840
EV_N = 3968
SM_KI, SM_BA, SM_AA, SM_WI = 0, 64, 68, 72
OD_Q, OD_K, OD_V, OD_R, OD_G = 0, 512, 1024, 2048, 3072
OD_N = 3200

NEG_INF = float("-inf")
INT_MIN = -(2 ** 31)
KEY_NEG_INF = INT_MIN + 0x7FFFFF


def _cparams(sem):
    return pltpu.CompilerParams(dimension_semantics=sem, vmem_limit_bytes=VMEM_LIMIT)


def _sigmoid(x):
    return 1.0 / (1.0 + jnp.exp(-x))


def _silu(x):
    return x * _sigmoid(x)


def _softplus(x):
    return jnp.maximum(x, 0.0) + jnp.log(1.0 + jnp.exp(-jnp.abs(x)))


def _dot(a, b, precision=None):
    return jnp.dot(a, b, precision=precision, preferred_element_type=F32)


def _dot_nt(a, b, precision=None):
    return lax.dot_general(a, b, (((1,), (1,)), ((), ())), precision=precision,
                           preferred_element_type=F32)


def _dot_tn(a, b, precision=None):
    return lax.dot_general(a, b, (((0,), (0,)), ((), ())), precision=precision,
                           preferred_element_type=F32)


def _bdot(a, b):
    return _dot(a.astype(BF16), b.astype(BF16))


def _bdot_nt(a, b):
    return _dot_nt(a.astype(BF16), b.astype(BF16))


def _bdot_tn(a, b):
    return _dot_tn(a.astype(BF16), b.astype(BF16))


def _split2(a):
    hi = a.astype(BF16)
    return hi, (a - hi.astype(F32)).astype(BF16)


def _dot3(a, b):
    ah, al = _split2(a)
    bh, bl = _split2(b)
    return _dot(ah, bh) + (_dot(ah, bl) + _dot(al, bh))


def _dot_mask(mask_b, x):
    x0 = x.astype(BF16)
    r1 = x - x0.astype(F32)
    x1 = r1.astype(BF16)
    x2 = (r1 - x1.astype(F32)).astype(BF16)
    return _dot(mask_b, x0) + (_dot(mask_b, x1) + _dot(mask_b, x2))


def _norm_mod(x, g, shift, scale):
    ms = jnp.mean(x * x, axis=-1, keepdims=True)
    y = x * lax.rsqrt(ms + EPS) * g
    return y * (1.0 + scale) + shift


def _ada_kernel(c_ref, w_ref, b_ref, o_ref):
    o_ref[0] = _dot(_silu(c_ref[...]), w_ref[0], HI) + b_ref[0]


def ada_modulation(c, ada_w, ada_b):
    depth, d, n = ada_w.shape
    b = c.shape[0]
    cp = jnp.zeros((SUBLANES, d), F32).at[:b].set(c)
    tn = 1536
    out = pl.pallas_call(
        _ada_kernel,
        grid=(depth, n // tn),
        in_specs=[pl.BlockSpec((SUBLANES, d), lambda l, j: (0, 0)),
                  pl.BlockSpec((1, d, tn), lambda l, j: (l, 0, j)),
                  pl.BlockSpec((1, 1, tn), lambda l, j: (l, 0, j))],
        out_specs=pl.BlockSpec((1, SUBLANES, tn), lambda l, j: (l, 0, j)),
        out_shape=jax.ShapeDtypeStruct((depth, SUBLANES, n), F32),
        compiler_params=_cparams(("arbitrary", "arbitrary")),
        name="ada_modulation",
    )(cp, ada_w, ada_b.reshape(depth, 1, n))
    return out[:, :b]


def _inproj_kernel(x_ref, g_ref, sh_ref, sc_ref, w_ref, o_ref):
    h = _norm_mod(x_ref[...], g_ref[...], sh_ref[0], sc_ref[0])
    o_ref[...] = _dot(h.astype(BF16), w_ref[...])


def in_projection(x, g, shift, scale, w, seq):
    t, d = x.shape
    n = w.shape[1]
    tm = 512
    per_b = seq // tm
    return pl.pallas_call(
        _inproj_kernel,
        grid=(t // tm,),
        in_specs=[pl.BlockSpec((tm, d), lambda i: (i, 0)),
                  pl.BlockSpec((1, d), lambda i: (0, 0)),
                  pl.BlockSpec((1, 1, d), lambda i: (i // per_b, 0, 0)),
                  pl.BlockSpec((1, 1, d), lambda i: (i // per_b, 0, 0)),
                  pl.BlockSpec((d, n), lambda i: (0, 0))],
        out_specs=pl.BlockSpec((tm, n), lambda i: (i, 0)),
        out_shape=jax.ShapeDtypeStruct((t, n), F32),
        compiler_params=_cparams(("arbitrary",)),
        name="in_projection",
    )(x, g, shift, scale, w)


def _outproj_kernel(*refs, n_in):
    o_refs = refs[:n_in]
    w_refs = refs[n_in:2 * n_in]
    x_ref, gt_ref, xo_ref = refs[2 * n_in:]
    acc = _dot(o_refs[0][...].astype(BF16), w_refs[0][...])
    for o_ref, w_ref in zip(o_refs[1:], w_refs[1:]):
        acc = acc + _dot(o_ref[...].astype(BF16), w_ref[...])
    xo_ref[...] = x_ref[...] + gt_ref[0] * acc


def out_projection(parts, weights, x, gate, seq):
    t, d = x.shape
    tm = 512
    per_b = seq // tm
    n_in = len(parts)
    in_specs = [pl.BlockSpec((tm, p.shape[1]), lambda i: (i, 0)) for p in parts]
    in_specs += [pl.BlockSpec(w.shape, lambda i: (0, 0)) for w in weights]
    in_specs += [pl.BlockSpec((tm, d), lambda i: (i, 0)),
                 pl.BlockSpec((1, 1, d), lambda i: (i // per_b, 0, 0))]
    return pl.pallas_call(
        functools.partial(_outproj_kernel, n_in=n_in),
        grid=(t // tm,),
        in_specs=in_specs,
        out_specs=pl.BlockSpec((tm, d), lambda i: (i, 0)),
        out_shape=jax.ShapeDtypeStruct((t, d), F32),
        compiler_params=_cparams(("arbitrary",)),
        name="out_projection",
    )(*parts, *weights, x, gate)


def _tri_masks(c):
    row = lax.broadcasted_iota(I32, (c, c), 0)
    col = lax.broadcasted_iota(I32, (c, c), 1)
    return row >= col, row > col, row == col


def _gdn_kernel(qkv_ref, za_ref, sm_ref, cw_ref, nega_ref, dtb_ref, on_ref, o_ref,
                tail_ref, state_ref, *, rb):
    c = GDN_CHUNK
    dk = GDN_DK

    @pl.when(pl.program_id(1) == 0)
    def _():
        tail_ref[...] = jnp.zeros_like(tail_ref)
        state_ref[...] = jnp.zeros_like(state_ref)

    x = qkv_ref[...]
    xx = jnp.concatenate([tail_ref[...], x], axis=0)
    tail_ref[...] = x[rb - SUBLANES:, :]
    cw = cw_ref[...]
    y = x * cw[CONV_WIDTH - 1:CONV_WIDTH, :]
    for i in range(1, CONV_WIDTH):
        y = y + pltpu.roll(xx, i, axis=0)[SUBLANES:, :] * cw[CONV_WIDTH - 1 - i:CONV_WIDTH - i, :]
    y = _silu(y)

    sm = sm_ref[...]
    beta_t = _sigmoid(sm)
    g_t = nega_ref[...] * _softplus(sm + dtb_ref[...])
    za = za_ref[...]
    hw = GDN_HEADS * dk
    nch = rb // c
    row = lax.broadcasted_iota(I32, (rb, rb), 0)
    col = lax.broadcasted_iota(I32, (rb, rb), 1)
    shift = int(np.log2(c))
    same = (row >> shift) == (col >> shift)
    incl = same & (row >= col)
    strict = same & (row > col)
    eye = (row == col).astype(F32)
    incl_b = jnp.where(incl, 1.0, 0.0).astype(BF16)
    gc_all = _dot_mask(incl_b, g_t)

    heads = range(GDN_HEADS)
    qs, ks, vs, bbs, gcs, decays = [], [], [], [], [], []
    for h in heads:
        lo = h * dk
        q = y[:, lo:lo + dk]
        k = y[:, hw + lo:hw + lo + dk]
        qs.append(q * lax.rsqrt(jnp.sum(q * q, axis=-1, keepdims=True) + EPS) * dk ** -0.5)
        ks.append(k * lax.rsqrt(jnp.sum(k * k, axis=-1, keepdims=True) + EPS))
        vs.append(y[:, 2 * hw + lo:2 * hw + lo + dk])
        bbs.append(jnp.broadcast_to(beta_t[:, SM_BA + h:SM_BA + h + 1], (rb, dk)))
        gc = jnp.broadcast_to(gc_all[:, SM_AA + h:SM_AA + h + 1], (rb, dk))
        gc_t = jnp.concatenate([gc[r:r + dk].T for r in range(0, rb, dk)], axis=1)[:1]
        gc_i = jnp.concatenate([gc] * (rb // dk), axis=1)
        decays.append(jnp.where(incl, jnp.exp(jnp.where(incl, gc_i - gc_t, 0.0)), 0.0))
        gcs.append(gc)
    kbetas = [ks[h] * bbs[h] for h in heads]
    ypows = [jnp.where(strict, -(_bdot_nt(kbetas[h], ks[h]) * decays[h]), 0.0) for h in heads]
    tmats = [eye + ypows[h] for h in heads]
    for _ in range(int(np.log2(c)) - 1):
        ypows = [_bdot(ypows[h], ypows[h]) for h in heads]
        tmats = [tmats[h] + _bdot(tmats[h], ypows[h]) for h in heads]
    egcs = [jnp.exp(gcs[h]) for h in heads]
    wus = [_bdot(tmats[h], jnp.concatenate([kbetas[h] * egcs[h], vs[h] * bbs[h]], axis=1))
           for h in heads]
    aqks = [_bdot_nt(qs[h], ks[h]) * decays[h] for h in heads]
    qgs = [qs[h] * egcs[h] for h in heads]
    sts = [state_ref[h] for h in heads]
    v_news = [[] for _ in heads]
    o_inter = [[] for _ in heads]
    for ci in range(nch):
        r0 = ci * c
        for h in heads:
            gl = gcs[h][r0 + c - 1:r0 + c, :]
            kd = ks[h][r0:r0 + c] * jnp.exp(gl - gcs[h][r0:r0 + c])
            ws = _bdot(jnp.concatenate([wus[h][r0:r0 + c, :dk], qgs[h][r0:r0 + c]], axis=0), sts[h])
            v_new = wus[h][r0:r0 + c, dk:] - ws[:c]
            o_inter[h].append(ws[c:])
            sts[h] = sts[h] * jnp.exp(gl) + _bdot_tn(kd, v_new)
            v_news[h].append(v_new)
    for h in heads:
        lo = h * dk
        state_ref[h] = sts[h]
        o = jnp.concatenate(o_inter[h], axis=0) + _bdot(aqks[h], jnp.concatenate(v_news[h], axis=0))
        o = o * lax.rsqrt(jnp.mean(o * o, axis=-1, keepdims=True) + EPS) * on_ref[...]
        o_ref[:, lo:lo + dk] = o * _silu(za[:, lo:lo + dk])


def gdn_mixer(proj, conv_w, a_log, dt_bias, out_norm, batch, seq):
    t = proj.shape[0]
    rb = 256
    nsb = seq // rb
    hw = GDN_HEADS * GDN_DK
    nega =jnp.zeros((1, LANES), F32).at[0, SM_AA:SM_AA + GDN_HEADS].set(-jnp.exp(a_log))
    dtb = jnp.zeros((1, LANES), F32).at[0, SM_AA:SM_AA + GDN_HEADS].set(dt_bias)
    row = lambda b, s: b * nsb + s
    return pl.pallas_call(
        functools.partial(_gdn_kernel, rb=rb),
        grid=(batch, nsb),
        in_specs=[pl.BlockSpec((rb, 3 * hw), lambda b, s: (row(b, s), 0)),
                  pl.BlockSpec((rb, hw), lambda b, s: (row(b, s), EV_ZA // hw)),
                  pl.BlockSpec((rb, LANES), lambda b, s: (row(b, s), EV_SM // LANES)),
                  pl.BlockSpec((CONV_WIDTH, 3 * hw), lambda b, s: (0, 0)),
                  pl.BlockSpec((1, LANES), lambda b, s: (0, 0)),
                  pl.BlockSpec((1, LANES), lambda b, s: (0, 0)),
                  pl.BlockSpec((1, GDN_DK), lambda b, s: (0, 0))],
        out_specs=pl.BlockSpec((rb, hw), lambda b, s: (row(b, s), 0)),
        out_shape=jax.ShapeDtypeStruct((t, hw), F32),
        scratch_shapes=[pltpu.VMEM((SUBLANES, 3 * hw), F32),
                        pltpu.VMEM((GDN_HEADS, GDN_DK, GDN_DK), F32)],
        compiler_params=_cparams(("arbitrary", "arbitrary")),
        name="gdn_mixer",
    )(proj, proj, proj, conv_w, nega, dtb, out_norm.reshape(1, GDN_DK))


def _gla_kernel(qk_ref, v_ref, r_ref, gl_ref, wg_ref, bg_ref, on_ref, o_ref, state_ref, *, rb):
    c = GLA_CHUNK
    dk, dv = GLA_DK, GLA_DV
    sb = GLA_SUB

    @pl.when(pl.program_id(1) == 0)
    def _():
        state_ref[...] = jnp.zeros_like(state_ref)

    z = _dot3(gl_ref[...], wg_ref[...]) + bg_ref[...]
    log_a = -_softplus(-z) * (1.0 / GLA_TAU)
    qk = qk_ref[...]
    vv = v_ref[...]
    rr = r_ref[...]
    incl, _, _ = _tri_masks(c)
    incl_b = incl.astype(F32).astype(BF16)
    hw = GLA_HEADS * dk
    sub_row = lax.broadcasted_iota(I32, (sb, dk), 0)
    att_lane = lax.broadcasted_iota(I32, (sb, c), 1)

    for ci in range(rb // c):
        r0 = ci * c
        b_all = _dot_mask(incl_b, log_a[r0:r0 + c])
        for h in range(GLA_HEADS):
            q = qk[r0:r0 + c, h * dk:(h + 1) * dk] * dk ** -0.5
            k = qk[r0:r0 + c, hw + h * dk:hw + (h + 1) * dk]
            v = vv[r0:r0 + c, h * dv:(h + 1) * dv]
            b = b_all[:, h * dk:(h + 1) * dk]
            st = state_ref[h]
            o = _bdot_nt(q * jnp.exp(b), st)
            att_rows = []
            for bi in range(c // sb):
                s0 = bi * sb
                qi = q[s0:s0 + sb]
                bi_ = b[s0:s0 + sb]
                ref = b[s0:s0 + 1]
                if bi > 0:
                    qt = qi * jnp.exp(bi_ - ref)
                    kt = k * jnp.exp(jnp.minimum(ref - b, 0.0))
                    att = jnp.where(att_lane < s0, _bdot_nt(qt, kt), 0.0)
                else:
                    att = jnp.zeros((sb, c), F32)
                for j in range(sb):
                    kj = k[s0 + j:s0 + j + 1]
                    bj = b[s0 + j:s0 + j + 1]
                    e = jnp.exp(jnp.where(sub_row >= j, bi_ - bj, NEG_INF))
                    col = jnp.sum(qi * kj * e, axis=-1, keepdims=True)
                    att = jnp.where(att_lane == s0 + j, col, att)
                att_rows.append(att)
            o = o + _bdot(jnp.concatenate(att_rows, axis=0), v)
            bl = b[c - 1:c]
            state_ref[h] = st * jnp.exp(bl) + _bdot_tn(v, k * jnp.exp(bl - b))
            o = o * lax.rsqrt(jnp.mean(o * o, axis=-1, keepdims=True) + EPS) * on_ref[...]
            o_ref[r0:r0 + c, h * dv:(h + 1) * dv] = o * _silu(rr[r0:r0 + c, h * dv:(h + 1) * dv])


def gla_mixer(proj, w_gate2, b_gate2, out_norm, batch, seq):
    t = proj.shape[0]
    rb = 2 * GLA_CHUNK
    nsb = seq // rb
    hw = GLA_HEADS * GLA_DK
    hv = GLA_HEADS * GLA_DV
    wg = jnp.zeros((LANES, hw), F32).at[:GLA_RANK].set(w_gate2)
    row = lambda b, s: b * nsb + s
    return pl.pallas_call(
        functools.partial(_gla_kernel, rb=rb),
        grid=(batch, nsb),
        in_specs=[pl.BlockSpec((rb, 2 * hw), lambda b, s: (row(b, s), 0)),
                  pl.BlockSpec((rb, hv), lambda b, s: (row(b, s), OD_V // hv)),
                  pl.BlockSpec((rb, hv), lambda b, s: (row(b, s), OD_R // hv)),
                  pl.BlockSpec((rb, LANES), lambda b, s: (row(b, s), OD_G // LANES)),
                  pl.BlockSpec((LANES, hw), lambda b, s: (0, 0)),
                  pl.BlockSpec((1, hw), lambda b, s: (0, 0)),
                  pl.BlockSpec((1, GLA_DV), lambda b, s: (0, 0))],
        out_specs=pl.BlockSpec((rb, hv), lambda b, s: (row(b, s), 0)),
        out_shape=jax.ShapeDtypeStruct((t, hv), F32),
        scratch_shapes=[pltpu.VMEM((GLA_HEADS, GLA_DV, GLA_DK), F32)],
        compiler_params=_cparams(("arbitrary", "arbitrary")),
        name="gla_mixer",
    )(proj, proj, proj, proj, wg, b_gate2.reshape(1, hw), out_norm.reshape(1, GLA_DV))


def _rope_table_kernel(pos_ref, inva_ref, invi_ref, ca_ref, sa_ref, ci_ref, si_ref):
    pos = pos_ref[...].astype(F32)
    lane = lax.broadcasted_iota(I32, (pos.shape[0], LANES), 1)
    ang_a = pos * inva_ref[...]
    ca_ref[...] = jnp.cos(ang_a)
    sa_ref[...] = jnp.where(lane < ATT_DH // 2, -1.0, 1.0) * jnp.sin(ang_a)
    ang_i = pos * invi_ref[...]
    ci_ref[...] = jnp.cos(ang_i)
    si_ref[...] = jnp.where(lane % IDX_DH < IDX_DH // 2, -1.0, 1.0) * jnp.sin(ang_i)


def rope_tables(positions):
    t = positions.size
    inv_a = 1.0 / (ROPE_THETA ** (jnp.arange(0, ATT_DH, 2, dtype=F32) / ATT_DH))
    inv_i = 1.0 / (ROPE_THETA ** (jnp.arange(0, IDX_DH, 2, dtype=F32) / IDX_DH))
    inv_a = jnp.tile(inv_a, 2).reshape(1, LANES)
    inv_i = jnp.tile(inv_i, 4).reshape(1, LANES)
    tm = 512
    spec = pl.BlockSpec((tm, LANES), lambda i: (i, 0))
    vec = pl.BlockSpec((1, LANES), lambda i: (0, 0))
    return pl.pallas_call(
        _rope_table_kernel,
        grid=(t // tm,),
        in_specs=[pl.BlockSpec((tm, 1), lambda i: (i, 0)), vec, vec],
        out_specs=[spec] * 4,
        out_shape=[jax.ShapeDtypeStruct((t, LANES), F32)] * 4,
        compiler_params=_cparams(("arbitrary",)),
        name="rope_tables",
    )(positions.reshape(t, 1), inv_a, inv_i)


def _dsa_prep_kernel(qb_ref, kb_ref, vb_ref, qi_ref, sm_ref, ca_ref, sa_ref, ci_ref, si_ref,
                     qo_ref, ko_ref, vt_ref, qio_ref, kio_ref, kn_ref):
    ca, sa, ci, si = ca_ref[...], sa_ref[...], ci_ref[...], si_ref[...]
    lane = lax.broadcasted_iota(I32, ca.shape, 1)
    first_half = lane % IDX_DH < IDX_DH // 2
    knorm = jnp.zeros((SUBLANES, LANES), F32)
    lane8 = lax.broadcasted_iota(I32, (SUBLANES, LANES), 1)

    def rope_att(tile):
        return tile * ca + pltpu.roll(tile, ATT_DH // 2, axis=1) * sa

    def rope_idx(tile):
        partner = jnp.where(first_half, pltpu.roll(tile, LANES - IDX_DH // 2, axis=1),
                            pltpu.roll(tile, IDX_DH // 2, axis=1))
        return tile * ci + partner * si

    qb, kb = qb_ref[...], kb_ref[...]
    for h in range(ATT_HEADS):
        sl = slice(h * ATT_DH, (h + 1) * ATT_DH)
        qo_ref[:, sl] = (rope_att(qb[:, sl]) * ATT_DH ** -0.5).astype(BF16)
        kr = rope_att(kb[:, sl]).astype(BF16)
        ko_ref[:, sl] = kr
        krf = kr.astype(F32)
        knorm = jnp.where(lane8 == h, jnp.max(jnp.sum(krf * krf, axis=1, keepdims=True)), knorm)
    kn_ref[0] = knorm
    vt = vb_ref[...].T.astype(BF16)
    ones = jnp.ones((DSA_SUM_ROWS, vt.shape[1]), BF16)
    for h in range(ATT_HEADS):
        vt_ref[0, h * DSA_VT_ROWS:h * DSA_VT_ROWS + ATT_DH, :] = vt[h * ATT_DH:(h + 1) * ATT_DH]
        vt_ref[0, h * DSA_VT_ROWS + ATT_DH:(h + 1) * DSA_VT_ROWS, :] = ones
    qi = qi_ref[...]
    for p in range(IDX_HEADS * IDX_DH // LANES):
        sl = slice(p * LANES, (p + 1) * LANES)
        qio_ref[:, sl] = rope_idx(qi[:, sl]).astype(BF16)
    ki = rope_idx(sm_ref[...])
    kio_ref[...] = jnp.where(lane < IDX_DH, ki, pltpu.roll(ki, IDX_DH, axis=1)).astype(BF16)


def dsa_prep(proj, tables, seq):
    t = proj.shape[0]
    tm = min(DSA_KEY_CHUNK, seq)
    aw = ATT_HEADS * ATT_DH
    iw = IDX_HEADS * IDX_DH
    tab = pl.BlockSpec((tm, LANES), lambda i: (i, 0))
    return pl.pallas_call(
        _dsa_prep_kernel,
        grid=(t // tm,),
        in_specs=[pl.BlockSpec((tm, aw), lambda i: (i, EV_QB // aw)),
                  pl.BlockSpec((tm, aw), lambda i: (i, EV_KB // aw)),
                  pl.BlockSpec((tm, aw), lambda i: (i, EV_VB // aw)),
                  pl.BlockSpec((tm, iw), lambda i: (i, EV_QI // iw)),
                  pl.BlockSpec((tm, LANES), lambda i: (i, EV_SM // LANES)),
                  tab, tab, tab, tab],
        out_specs=[pl.BlockSpec((tm, aw), lambda i: (i, 0)),
                   pl.BlockSpec((tm, aw), lambda i: (i, 0)),
                   pl.BlockSpec((1, ATT_HEADS * DSA_VT_ROWS, tm), lambda i: (i, 0, 0)),
                   pl.BlockSpec((tm, iw), lambda i: (i, 0)),
                   pl.BlockSpec((tm, LANES), lambda i: (i, 0)),
                   pl.BlockSpec((1, SUBLANES, LANES), lambda i: (i, 0, 0))],
        out_shape=[jax.ShapeDtypeStruct((t, aw), BF16),
                   jax.ShapeDtypeStruct((t, aw), BF16),
                   jax.ShapeDtypeStruct((t // tm, ATT_HEADS * DSA_VT_ROWS, tm), BF16),
                   jax.ShapeDtypeStruct((t, iw), BF16),
                   jax.ShapeDtypeStruct((t, LANES), BF16),
                   jax.ShapeDtypeStruct((t // tm, SUBLANES, LANES), F32)],
        compiler_params=_cparams(("arbitrary",)),
        name="dsa_prep",
    )(proj, proj, proj, proj, proj, *tables)


def _fold_rows(x, op):
    parts = [x[r:r + SUBLANES] for r in range(0, x.shape[0], SUBLANES)]
    while len(parts) > 1:
        parts = [op(parts[j], parts[j + 1]) for j in range(0, len(parts), 2)]
    return parts[0]


def _dsa_kernel(qb_ref, qi_ref, sm_ref, kn_ref, k_ref, vt_ref, ki_ref, o_ref,
                key_ref, l_ref, acc_ref, *, kc, topk):
    nq = Q_BLOCK
    i = pl.program_id(1)
    nkc = (i * nq + nq + kc - 1) // kc
    q_pos = i * nq + lax.broadcasted_iota(I32, (1, nq), 1)
    key0 = lax.broadcasted_iota(I32, (kc, nq), 0)
    lane = lax.broadcasted_iota(I32, (nq, LANES), 1)
    wi_t = sm_ref[...].T * (IDX_HEADS * IDX_DH) ** -0.5
    qi = qi_ref[...]
    zero_b = jnp.zeros((nq, LANES), BF16)
    qi_heads = []
    for h in range(IDX_HEADS):
        pair = qi[:, (h // 2) * LANES:(h // 2 + 1) * LANES]
        own = (lane < IDX_DH) if h % 2 == 0 else (lane >= IDX_DH)
        qi_heads.append(jnp.where(own, pair, zero_b))

    def score_body(c, carry):
        start = pl.multiple_of(c * kc, kc)
        kic = ki_ref[0, pl.ds(start, kc), :]
        dots = [_dot_nt(kic, qi_heads[h]) for h in range(IDX_HEADS)]
        s = jnp.zeros((kc, nq), F32)
        for h in range(IDX_HEADS):
            s = s + wi_t[SM_WI + h:SM_WI + h + 1, :] * jnp.maximum(dots[h], 0.0)
        s = s + 0.0
        s = jnp.where(key0 <= q_pos - c * kc, s, NEG_INF)
        bits = pltpu.bitcast(s, I32)
        key_ref[c] = bits ^ ((bits >> 31) & 0x7FFFFFFF)
        return carry

    lax.fori_loop(0, nkc, score_body, 0)

    halves = [slice(j * LANES, (j + 1) * LANES) for j in range(nq // LANES)]
    key0h = key0[:, :LANES]
    zero8 = jnp.zeros((SUBLANES, LANES), I32)

    def count(fns, *rows):
        def body(c, accs):
            out = []
            for j, sl in enumerate(halves):
                key = key_ref[c, :, sl]
                args = [r[:, sl] for r in rows]
                for f, fn in enumerate(fns):
                    hit = jnp.where(fn(key, c, *args), 1, 0).astype(I32)
                    out.append(accs[j * len(fns) + f] + _fold_rows(hit, jnp.add))
            return tuple(out)

        accs = lax.fori_loop(0, nkc, body, (zero8,) * (len(halves) * len(fns)))
        return [jnp.concatenate([jnp.sum(accs[j * len(fns) + f], axis=0, keepdims=True)
                                 for j in range(len(halves))], axis=1) for f in range(len(fns))]

    thr = jnp.where(count([lambda key, c: key >= 0])[0] >= topk, 0, INT_MIN).astype(I32)

    def bit_body(it, thr):
        cand = thr + (jnp.int32(1) << (30 - it))
        return jnp.where(count([lambda key, c, cd: key >= cd], cand)[0] >= topk, cand, thr)

    thr = lax.fori_loop(0, 31, bit_body, thr)

    n_gt, n_eq = count([lambda key, c, t: key > t, lambda key, c, t: key == t], thr)
    need = topk - n_gt
    big = jnp.int32(2 ** 30)

    def tie_search():
        def idx_body(it, x):
            cand = x + (jnp.int32(1) << (14 - it))
            below = count([lambda key, c, t, cd: key0h < jnp.where(key == t, cd - c * kc, 0)],
                          thr, cand)[0]
            return jnp.where(below < need, cand, x)
        return lax.fori_loop(0, 15, idx_body, jnp.zeros((1, nq), I32))

    any_tie = jnp.max(jnp.where(n_eq > need, 1, 0)) > 0
    last = lax.cond(any_tie, tie_search, lambda: jnp.full((1, nq), big, I32))

    last = jnp.where(thr == KEY_NEG_INF, -1, last)
    zero_f = jnp.float32(0.0)
    ninf_f = jnp.float32(NEG_INF)
    qb = qb_ref[...]

    def select_bias(c):
        key = key_ref[c]
        tie_ok = jnp.where(key0 <= last - c * kc, zero_f, ninf_f)
        return jnp.where(key > thr, zero_f, jnp.where(key == thr, tie_ok, ninf_f))

    def chunk_logits(c, h, bias):
        start = pl.multiple_of(c * kc, kc)
        kch = k_ref[0, pl.ds(start, kc), h * ATT_DH:(h + 1) * ATT_DH]
        return _dot_nt(kch, qb[:, h * ATT_DH:(h + 1) * ATT_DH]) + bias

    def attend(shifts):
        acc_ref[...] = jnp.zeros(acc_ref.shape, F32)

        def body(c, carry):
            bias = select_bias(c)
            heads = range(ATT_HEADS)
            logits = [chunk_logits(c, h, bias) for h in heads]
            ps = [jnp.exp(logits[h] - shifts[h]) for h in heads]
            for h in heads:
                vth = vt_ref[0, c, h * DSA_VT_ROWS:(h + 1) * DSA_VT_ROWS, :]
                acc_ref[h] = acc_ref[h] + _dot(vth, ps[h].astype(BF16))
            return carry

        lax.fori_loop(0, nkc, body, 0)
        return [acc_ref[h, ATT_DH:ATT_DH + 1, :] for h in range(ATT_HEADS)]

    ones_b = jnp.ones((SUBLANES, ATT_DH), BF16)
    bounds = []
    for h in range(ATT_HEADS):
        qh = qb[:, h * ATT_DH:(h + 1) * ATT_DH].astype(F32)
        qsq = _dot_nt(ones_b, (qh * qh).astype(BF16))[0:1, :]
        bounds.append(jnp.sqrt(qsq * kn_ref[0][0:1, h:h + 1]) * 1.02)
    sums = attend(bounds)

    def write_out(sums):
        for h in range(ATT_HEADS):
            o_ref[:, h * ATT_DH:(h + 1) * ATT_DH] = (acc_ref[h, :ATT_DH, :] / sums[h]).T

    write_out(sums)
    lowest = jnp.minimum(jnp.minimum(sums[0], sums[1]), jnp.minimum(sums[2], sums[3]))

    @pl.when(jnp.min(lowest) <= 0.0)
    def _():
        l_ref[...] = jnp.full(l_ref.shape, NEG_INF, F32)

        def max_body(c, carry):
            bias = select_bias(c)
            for h in range(ATT_HEADS):
                l_ref[h] = jnp.maximum(l_ref[h], _fold_rows(chunk_logits(c, h, bias), jnp.maximum))
            return carry

        lax.fori_loop(0, nkc, max_body, 0)
        exact = [jnp.max(l_ref[h], axis=0, keepdims=True) for h in range(ATT_HEADS)]
        write_out(attend(exact))


def dsa_attention(proj, qb, kb, vt, qi, ki, knorm, batch, seq):
    t = proj.shape[0]
    aw = ATT_HEADS * ATT_DH
    iw = IDX_HEADS * IDX_DH
    kc = min(DSA_KEY_CHUNK, seq)
    nkc = seq // kc
    topk = min(TOPK_MAX, seq // 4)
    nq = seq // Q_BLOCK
    row = lambda b, i: b * nq + i
    kb3, ki3 = (a.reshape(batch, seq, a.shape[1]) for a in (kb, ki))
    vw = ATT_HEADS * DSA_VT_ROWS
    vt4 = vt.reshape(batch, nkc, vw, kc)
    kmax = jnp.max(knorm.reshape(batch, nkc, SUBLANES, LANES), axis=1)
    resident = lambda shape: pl.BlockSpec((1,) + shape, lambda b, i: (b,) + (0,) * len(shape),
                                          pipeline_mode=pl.Buffered(1))
    return pl.pallas_call(
        functools.partial(_dsa_kernel, kc=kc, topk=topk),
        grid=(batch, nq),
        in_specs=[pl.BlockSpec((Q_BLOCK, aw), lambda b, i: (row(b, i), 0)),
                  pl.BlockSpec((Q_BLOCK, iw), lambda b, i: (row(b, i), 0)),
                  pl.BlockSpec((Q_BLOCK, LANES), lambda b, i: (row(b, i), EV_SM // LANES)),
                  pl.BlockSpec((1, SUBLANES, LANES), lambda b, i: (b, 0, 0)),
                  resident((seq, aw)), resident((nkc, vw, kc)), resident((seq, LANES))],
        out_specs=pl.BlockSpec((Q_BLOCK, aw), lambda b, i: (row(b, i), 0)),
        out_shape=jax.ShapeDtypeStruct((t, aw), F32),
        scratch_shapes=[pltpu.VMEM((nkc, kc, Q_BLOCK), I32),
                        pltpu.VMEM((ATT_HEADS, SUBLANES, Q_BLOCK), F32),
                        pltpu.VMEM((ATT_HEADS, DSA_VT_ROWS, Q_BLOCK), F32)],
        compiler_params=_cparams(("arbitrary", "arbitrary")),
        name="dsa_attention",
    )(qb, qi, proj, kmax, kb3, vt4, ki3)


def _router_kernel(x_ref, g_ref, sh_ref, sc_ref, wr_ref, br_ref, h_ref, route_ref, cnt_ref,
                   carry_ref):
    @pl.when(pl.program_id(0) == 0)
    def _():
        carry_ref[...] = jnp.zeros_like(carry_ref)

    h = _norm_mod(x_ref[...], g_ref[...], sh_ref[0], sc_ref[0])
    h_ref[...] = h
    tm = h.shape[0]
    logits = _dot3(h, wr_ref[...]) + br_ref[...]
    lane = lax.broadcasted_iota(I32, (tm, LANES), 1)
    far = jnp.int32(LANES)

    def first_max(vals):
        m = jnp.max(vals, axis=1, keepdims=True)
        return m, jnp.min(jnp.where(vals == m, lane, far), axis=1, keepdims=True)

    is_group = lane < N_GROUPS
    gmax, gsel = first_max(jnp.where(is_group, logits, NEG_INF))
    gsum = jnp.sum(jnp.where(is_group, jnp.exp(logits - gmax), 0.0), axis=1, keepdims=True)
    lo = N_GROUPS + EXPERTS_PER_GROUP * gsel
    el = jnp.where((lane >= lo) & (lane < lo + EXPERTS_PER_GROUP), logits, NEG_INF)
    m1, i1 = first_max(el)
    m2, i2 = first_max(jnp.where(lane == i1, NEG_INF, el))
    e2 = jnp.exp(m2 - m1)
    gate1 = 1.0 / (gsum * (1.0 + e2))
    gate2 = e2 * gate1
    eid1 = i1 - N_GROUPS
    eid2 = i2 - N_GROUPS
    oh1 = lane == eid1
    oh2 = lane == eid2
    onehot = jnp.where(oh1 | oh2, 1.0, 0.0)
    r = lax.broadcasted_iota(I32, (tm, tm), 0)
    cidx = lax.broadcasted_iota(I32, (tm, tm), 1)
    below = jnp.where(r > cidx, 1.0, 0.0).astype(BF16)
    cum = _dot(below, onehot.astype(BF16)) + carry_ref[0:1, :]
    rank1 = jnp.sum(jnp.where(oh1, cum, 0.0), axis=1, keepdims=True)
    rank2 = jnp.sum(jnp.where(oh2, cum, 0.0), axis=1, keepdims=True)
    carry_ref[...] = carry_ref[...] + jnp.sum(onehot, axis=0, keepdims=True)
    cnt_ref[...] = carry_ref[...]
    out = jnp.zeros((tm, LANES), F32)
    for j, val in enumerate((eid1.astype(F32), eid2.astype(F32), gate1, gate2, rank1, rank2)):
        out = jnp.where(lane == j, val, out)
    route_ref[...] = out


def moe_router(x, g, shift, scale, wr, br, seq):
    t, d = x.shape
    tm = 512
    per_b = seq // tm
    return pl.pallas_call(
        _router_kernel,
        grid=(t // tm,),
        in_specs=[pl.BlockSpec((tm, d), lambda i: (i, 0)),
                  pl.BlockSpec((1, d), lambda i: (0, 0)),
                  pl.BlockSpec((1, 1, d), lambda i: (i // per_b, 0, 0)),
                  pl.BlockSpec((1, 1, d), lambda i: (i // per_b, 0, 0)),
                  pl.BlockSpec((d, LANES), lambda i: (0, 0)),
                  pl.BlockSpec((1, LANES), lambda i: (0, 0))],
        out_specs=[pl.BlockSpec((tm, d), lambda i: (i, 0)),
                   pl.BlockSpec((tm, LANES), lambda i: (i, 0)),
                   pl.BlockSpec((SUBLANES, LANES), lambda i: (0, 0))],
        out_shape=[jax.ShapeDtypeStruct((t, d), F32),
                   jax.ShapeDtypeStruct((t, LANES), F32),
                   jax.ShapeDtypeStruct((SUBLANES, LANES), F32)],
        scratch_shapes=[pltpu.VMEM((SUBLANES, LANES), F32)],
        compiler_params=_cparams(("arbitrary",)),
        name="moe_router",
    )(x, g, shift, scale, wr, br)


def _zero_kernel(o_ref):
    o_ref[...] = jnp.zeros_like(o_ref)


def zero_rows(rows, d):
    tm = 1024
    return pl.pallas_call(
        _zero_kernel,
        grid=(rows // tm,),
        out_specs=pl.BlockSpec((tm, d), lambda i: (i, 0)),
        out_shape=jax.ShapeDtypeStruct((rows, d), F32),
        compiler_params=_cparams(("arbitrary",)),
        name="zero_rows",
    )()


def _dispatch_kernel(dest_ref, seg_ref, h_ref, xs_in_ref, xs_ref, inv_ref, sem, *, tm):
    del xs_in_ref
    i = pl.program_id(0)
    base = i * tm * 2

    rows = MOE_ROWS
    n_assign = dest_ref.shape[0]

    @pl.when(i == 0)
    def _():
        def mark(p, carry):
            blk = p // rows
            spare = ((blk + MOE_OUT_BUFS - 1) % MOE_OUT_BUFS) * rows + (p - blk * rows)
            inv_ref[p] = n_assign + spare
            return carry
        lax.fori_loop(0, rows, mark, 0)
        for e in range(N_EXPERTS):
            lax.fori_loop(rows + seg_ref[e], rows + seg_ref[N_EXPERTS + e], mark, 0)
        lax.fori_loop(rows + seg_ref[2 * N_EXPERTS - 1], inv_ref.shape[0], mark, 0)

    def row_copy(r, d):
        return pltpu.make_async_copy(h_ref.at[pl.ds(r, 1)], xs_ref.at[pl.ds(d, 1)], sem)

    n_tok = n_assign // 2

    group = 4

    def start(g, carry):
        r0 = g * group
        ds = [[dest_ref[base + 2 * (r0 + k) + j] for j in range(2)] for k in range(group)]
        for k in range(group):
            for j in range(2):
                row_copy(r0 + k, ds[k][j]).start()
        for k in range(group):
            for j in range(2):
                inv_ref[rows + ds[k][j]] = j * n_tok + i * tm + r0 + k
        return carry

    def wait(r, carry):
        row_copy(0, 0).wait()
        row_copy(0, 0).wait()
        return carry

    lax.fori_loop(0, tm // group, start, 0)
    lax.fori_loop(0, tm, wait, 0)


def moe_dispatch(dest, seg, h, xs_zero):
    t, d = h.shape
    tm = 256
    return pl.pallas_call(
        functools.partial(_dispatch_kernel, tm=tm),
        grid_spec=pltpu.PrefetchScalarGridSpec(
            num_scalar_prefetch=2,
            grid=(t // tm,),
            in_specs=[pl.BlockSpec((tm, d), lambda i, dest, seg: (i, 0)),
                      pl.BlockSpec(memory_space=pl.ANY)],
            out_specs=[pl.BlockSpec(memory_space=pl.ANY),
                       pl.BlockSpec(memory_space=pltpu.SMEM)],
            scratch_shapes=[pltpu.SemaphoreType.DMA(())]),
        out_shape=[jax.ShapeDtypeStruct(xs_zero.shape, F32),
                   jax.ShapeDtypeStruct((xs_zero.shape[0] + MOE_ROWS,), I32)],
        input_output_aliases={3: 0},
        compiler_params=_cparams(("arbitrary",)),
        name="moe_dispatch",
    )(dest, seg, h, xs_zero)


MOE_OUT_BUFS = 3


def _expert_kernel(be_ref, inv_ref, xs_ref, w1_ref, w3_ref, w2_ref, y_ref,
                   w1b, w3b, w2b, ybuf, sems, *, n_assign):
    i = pl.program_id(0)
    last = pl.num_programs(0) - 1
    rows = ybuf.shape[1]
    slot = i % MOE_OUT_BUFS
    prev_slot = (i + MOE_OUT_BUFS - 1) % MOE_OUT_BUFS
    prev = be_ref[jnp.maximum(i - 1, 0)]

    def row_copy(s, r, dst):
        return pltpu.make_async_copy(ybuf.at[s, pl.ds(r, 1)], y_ref.at[pl.ds(dst, 1)], sems.at[s])

    def send(s, block):
        for r in range(rows):
            row_copy(s, r, inv_ref[(block + 1) * rows + r]).start()

    def drain(s):
        for r in range(rows):
            row_copy(s, 0, 0).wait()

    @pl.when(i == 0)
    def _():
        ybuf[prev_slot] = jnp.zeros((rows, ybuf.shape[2]), F32)
        for s in range(MOE_OUT_BUFS):
            if s != MOE_OUT_BUFS - 1:
                for r in range(rows):
                    row_copy(prev_slot, r, n_assign + s * rows + r).start()
                drain(prev_slot)

    @pl.when(i >= MOE_OUT_BUFS - 1)
    def _():
        drain(slot)

    @pl.when((i == 0) | (be_ref[i] != prev))
    def _():
        w1b[...] = w1_ref[0, 0].astype(BF16)
        w3b[...] = w3_ref[0, 0].astype(BF16)
        w2b[...] = w2_ref[0, 0].astype(BF16)

    send(prev_slot, i - 1)
    x = xs_ref[...].astype(BF16)
    a = _dot(x, w1b[...])
    hid = _silu(a) * _dot(x, w3b[...])
    ybuf[slot] = _dot(hid.astype(BF16), w2b[...])

    @pl.when(i == last)
    def _():
        send(slot, i)
        for s in range(MOE_OUT_BUFS):
            drain(s)


def moe_experts(blk_exp, inv, xs, w1, w3, w2, layer, t):
    p, d = xs.shape
    de = w1.shape[3]
    rows = MOE_ROWS
    wmap = lambda i, be, inv: (layer, be[i], 0, 0)
    return pl.pallas_call(
        functools.partial(_expert_kernel, n_assign=2 * t),
        grid_spec=pltpu.PrefetchScalarGridSpec(
            num_scalar_prefetch=2,
            grid=(p // rows,),
            in_specs=[pl.BlockSpec((rows, d), lambda i, be, inv: (i, 0)),
                      pl.BlockSpec((1, 1, d, de), wmap),
                      pl.BlockSpec((1, 1, d, de), wmap),
                      pl.BlockSpec((1, 1, de, d), wmap)],
            out_specs=pl.BlockSpec(memory_space=pl.ANY),
            scratch_shapes=[pltpu.VMEM((d, de), BF16), pltpu.VMEM((d, de), BF16),
                            pltpu.VMEM((de, d), BF16), pltpu.VMEM((MOE_OUT_BUFS, rows, d), F32),
                            pltpu.SemaphoreType.DMA((MOE_OUT_BUFS,))]),
        out_shape=jax.ShapeDtypeStruct((2 * t + MOE_OUT_BUFS * rows, d), F32),
        compiler_params=_cparams(("arbitrary",)),
        name="moe_experts",
    )(blk_exp, inv, xs, w1, w3, w2)


def _combine_kernel(y1_ref, y2_ref, route_ref, x_ref, gt_ref, *rest, final):
    route = route_ref[...]
    y = route[:, 2:3] * y1_ref[...] + route[:, 3:4] * y2_ref[...]
    out = x_ref[...] + gt_ref[0] * y
    if final:
        fn_ref, o_ref = rest
        out = out * lax.rsqrt(jnp.mean(out * out, axis=-1, keepdims=True) + EPS) * fn_ref[...]
    else:
        (o_ref,) = rest
    o_ref[...] = out


def moe_combine(y2, route, x, gate, seq, final_gain=None):
    t, d = x.shape
    tm = 512
    per_b = seq // tm
    final = final_gain is not None
    nt = t // tm
    in_specs = [pl.BlockSpec((tm, d), lambda i: (i, 0)),
                pl.BlockSpec((tm, d), lambda i: (nt + i, 0)),
                pl.BlockSpec((tm, LANES), lambda i: (i, 0)),
                pl.BlockSpec((tm, d), lambda i: (i, 0)),
                pl.BlockSpec((1, 1, d), lambda i: (i // per_b, 0, 0))]
    args = [y2, y2, route, x, gate]
    if final:
        in_specs.append(pl.BlockSpec((1, d), lambda i: (0, 0)))
        args.append(final_gain)
    return pl.pallas_call(
        functools.partial(_combine_kernel, final=final),
        grid=(t // tm,),
        in_specs=in_specs,
        out_specs=pl.BlockSpec((tm, d), lambda i: (i, 0)),
        out_shape=jax.ShapeDtypeStruct((t, d), F32),
        compiler_params=_cparams(("arbitrary",)),
        name="moe_combine",
    )(*args)


def hier_moe_layer(x, g, shift, scale, gate, wg, bg, we, be, w1, w3, w2, layer, seq,
                   final_gain=None):
    t, d = x.shape
    wr = jnp.zeros((d, LANES), F32)
    wr = wr.at[:, :N_GROUPS].set(wg)
    wr = wr.at[:, N_GROUPS:N_GROUPS + N_EXPERTS].set(we.transpose(1, 0, 2).reshape(d, N_EXPERTS))
    br = jnp.zeros((1, LANES), F32)
    br = br.at[0, :N_GROUPS].set(bg).at[0, N_GROUPS:N_GROUPS + N_EXPERTS].set(be.reshape(-1))
    h, route, cnt = moe_router(x, g, shift, scale, wr, br, seq)
    counts = cnt[0, :N_EXPERTS].astype(I32)
    padded = ((counts + MOE_ROWS - 1) // MOE_ROWS) * MOE_ROWS
    pad_end = jnp.cumsum(padded)
    pad_start = pad_end - padded
    eid = route[:, 0:2].astype(I32)
    rank = route[:, 4:6].astype(I32)
    experts = jnp.arange(N_EXPERTS, dtype=I32)
    start_of = jnp.sum(jnp.where(eid[:, :, None] == experts, pad_start, 0), axis=-1)
    dest = (start_of + rank).reshape(-1)
    p_rows = 2 * t + N_EXPERTS * MOE_ROWS
    nblk = p_rows // MOE_ROWS
    blk_row = jnp.arange(nblk, dtype=I32)[:, None] * MOE_ROWS
    blk_exp = jnp.minimum(jnp.sum((pad_end[None, :] <= blk_row).astype(I32), axis=1),
                          N_EXPERTS - 1)
    seg = jnp.concatenate([pad_start + counts, pad_end]).astype(I32)
    xs, inv = moe_dispatch(dest, seg, h, zero_rows(p_rows, d))
    y2 = moe_experts(blk_exp, inv, xs, w1, w3, w2, layer, t)
    return moe_combine(y2, route, x, gate, seq, final_gain)


def _even_weight(w_in):
    d = w_in.shape[0]
    sizes = [512, 512, 512, 512, 4, 4, 512, 512, 512, 256, 64, 4]
    offs = np.concatenate([[0], np.cumsum(sizes)])
    qa, ka, va, za, ba, aa, qb, kb, vb, qi, ki, wi = (w_in[:, offs[j]:offs[j + 1]]
                                                      for j in range(len(sizes)))
    pad = jnp.zeros((d, EV_N - EV_SM - 76), w_in.dtype)
    return jnp.concatenate([qa, ka, va, za, qb, kb, vb, qi, ki, ba, aa, wi, pad], axis=1)


def kernel(x, c, positions, ada_w, ada_b, norm_mix, norm_ffn, even_w_in, gdn_conv_w, gdn_a_log, gdn_dt_bias, gdn_out_norm, even_w_out, gla_w_in, gla_w_gate2, gla_b_gate2, gla_out_norm, gla_w_out, router_group_w, router_group_b, router_exp_w, router_exp_b, exp_w1, exp_w3, exp_w2, final_norm):
    batch, seq, d = x.shape
    depth = ada_w.shape[0]
    t = batch * seq
    xt = x.reshape(t, d)
    mod = ada_modulation(c, ada_w, ada_b)
    tables = rope_tables(positions)
    for layer in range(depth):
        sh_m, sc_m, gt_m, sh_f, sc_f, gt_f = (mod[layer, :, j * d:(j + 1) * d].reshape(batch, 1, d)
                                              for j in range(6))
        i = layer // 2
        g_mix = norm_mix[layer].reshape(1, d)
        if layer % 2 == 0:
            w_in = _even_weight(even_w_in[i]).astype(BF16)
            proj = in_projection(xt, g_mix, sh_m, sc_m, w_in, seq)
            oa = gdn_mixer(proj, gdn_conv_w[i], gdn_a_log[i], gdn_dt_bias[i], gdn_out_norm[i],
                           batch, seq)
            qb, kb, vt, qi, ki, knorm = dsa_prep(proj, tables, seq)
            ob = dsa_attention(proj, qb, kb, vt, qi, ki, knorm, batch, seq)
            w_out = even_w_out[i].astype(BF16)
            half = GDN_HEADS * GDN_DK
            xt = out_projection([oa, ob], [w_out[:half], w_out[half:]], xt, gt_m, seq)
        else:
            w_in = jnp.pad(gla_w_in[i], ((0, 0), (0, OD_N - gla_w_in.shape[2]))).astype(BF16)
            proj = in_projection(xt, g_mix, sh_m, sc_m, w_in, seq)
            o = gla_mixer(proj, gla_w_gate2[i], gla_b_gate2[i], gla_out_norm[i], batch, seq)
            xt = out_projection([o], [gla_w_out[i].astype(BF16)], xt, gt_m, seq)
        xt = hier_moe_layer(xt, norm_ffn[layer].reshape(1, d), sh_f, sc_f, gt_f,
                            router_group_w[layer], router_group_b[layer],
                            router_exp_w[layer], router_exp_b[layer],
                            exp_w1, exp_w3, exp_w2, layer, seq,
                            final_norm.reshape(1, d) if layer == depth - 1 else None)
    return xt.reshape(batch, seq, d)
```

```python
import functools

import jax
import jax.numpy as jnp
import numpy as np
from jax import lax
from jax.experimental import pallas as pl
from jax.experimental.pallas import tpu as pltpu

F32 = jnp.float32
BF16 = jnp.bfloat16
I32 = jnp.int32
HI = lax.Precision.HIGHEST

LANES = 128
SUBLANES = 8
VMEM_LIMIT = 56 * 1024 * 1024

EPS = 1e-6
ROPE_THETA = 10000.0
GDN_HEADS, GDN_DK = 4, 128
CONV_WIDTH = 4
GDN_CHUNK = 64
ATT_HEADS, ATT_DH = 4, 128
IDX_HEADS, IDX_DH = 4, 64
TOPK_MAX = 256
Q_BLOCK = 256
DSA_KEY_CHUNK = 512
DSA_SUM_ROWS = 16
DSA_VT_ROWS = ATT_DH + DSA_SUM_ROWS
GLA_HEADS, GLA_DK, GLA_DV = 4, 128, 256
GLA_RANK = 16
GLA_TAU = 16.0
GLA_CHUNK = 64
GLA_SUB = 16
N_GROUPS, EXPERTS_PER_GROUP = 4, 8
N_EXPERTS = N_GROUPS * EXPERTS_PER_GROUP
MOE_ROWS = 128

EV_QA, EV_KA, EV_VA, EV_ZA = 0, 512, 1024, 1536
EV_QB, EV_KB, EV_VB, EV_QI, EV_SM = 2048, 2560, 3072, 3584, 3840
EV_N = 3968
SM_KI, SM_BA, SM_AA, SM_WI = 0, 64, 68, 72
OD_Q, OD_K, OD_V, OD_R, OD_G = 0, 512, 1024, 2048, 3072
OD_N = 3200

NEG_INF = float("-inf")
INT_MIN = -(2 ** 31)
KEY_NEG_INF = INT_MIN + 0x7FFFFF


def _cparams(sem):
    return pltpu.CompilerParams(dimension_semantics=sem, vmem_limit_bytes=VMEM_LIMIT)


def _sigmoid(x):
    return 1.0 / (1.0 + jnp.exp(-x))


def _silu(x):
    return x * _sigmoid(x)


def _softplus(x):
    return jnp.maximum(x, 0.0) + jnp.log(1.0 + jnp.exp(-jnp.abs(x)))


def _dot(a, b, precision=None):
    return jnp.dot(a, b, precision=precision, preferred_element_type=F32)


def _dot_nt(a, b, precision=None):
    return lax.dot_general(a, b, (((1,), (1,)), ((), ())), precision=precision,
                           preferred_element_type=F32)


def _dot_tn(a, b, precision=None):
    return lax.dot_general(a, b, (((0,), (0,)), ((), ())), precision=precision,
                           preferred_element_type=F32)


def _bdot(a, b):
    return _dot(a.astype(BF16), b.astype(BF16))


def _bdot_nt(a, b):
    return _dot_nt(a.astype(BF16), b.astype(BF16))


def _bdot_tn(a, b):
    return _dot_tn(a.astype(BF16), b.astype(BF16))


def _split2(a):
    hi = a.astype(BF16)
    return hi, (a - hi.astype(F32)).astype(BF16)


def _dot3(a, b):
    ah, al = _split2(a)
    bh, bl = _split2(b)
    return _dot(ah, bh) + (_dot(ah, bl) + _dot(al, bh))


def _dot_mask(mask_b, x):
    x0 = x.astype(BF16)
    r1 = x - x0.astype(F32)
    x1 = r1.astype(BF16)
    x2 = (r1 - x1.astype(F32)).astype(BF16)
    return _dot(mask_b, x0) + (_dot(mask_b, x1) + _dot(mask_b, x2))


def _norm_mod(x, g, shift, scale):
    ms = jnp.mean(x * x, axis=-1, keepdims=True)
    y = x * lax.rsqrt(ms + EPS) * g
    return y * (1.0 + scale) + shift


def _ada_kernel(c_ref, w_ref, b_ref, o_ref):
    o_ref[0] = _dot(_silu(c_ref[...]), w_ref[0], HI) + b_ref[0]


def ada_modulation(c, ada_w, ada_b):
    depth, d, n = ada_w.shape
    b = c.shape[0]
    cp = jnp.zeros((SUBLANES, d), F32).at[:b].set(c)
    tn = 1536
    out = pl.pallas_call(
        _ada_kernel,
        grid=(depth, n // tn),
        in_specs=[pl.BlockSpec((SUBLANES, d), lambda l, j: (0, 0)),
                  pl.BlockSpec((1, d, tn), lambda l, j: (l, 0, j)),
                  pl.BlockSpec((1, 1, tn), lambda l, j: (l, 0, j))],
        out_specs=pl.BlockSpec((1, SUBLANES, tn), lambda l, j: (l, 0, j)),
        out_shape=jax.ShapeDtypeStruct((depth, SUBLANES, n), F32),
        compiler_params=_cparams(("arbitrary", "arbitrary")),
        name="ada_modulation",
    )(cp, ada_w, ada_b.reshape(depth, 1, n))
    return out[:, :b]


def _inproj_kernel(x_ref, g_ref, sh_ref, sc_ref, w_ref, o_ref):
    h = _norm_mod(x_ref[...], g_ref[...], sh_ref[0], sc_ref[0])
    o_ref[...] = _dot(h.astype(BF16), w_ref[...])


def in_projection(x, g, shift, scale, w, seq):
    t, d = x.shape
    n = w.shape[1]
    tm = 512
    per_b = seq // tm
    return pl.pallas_call(
        _inproj_kernel,
        grid=(t // tm,),
        in_specs=[pl.BlockSpec((tm, d), lambda i: (i, 0)),
                  pl.BlockSpec((1, d), lambda i: (0, 0)),
                  pl.BlockSpec((1, 1, d), lambda i: (i // per_b, 0, 0)),
                  pl.BlockSpec((1, 1, d), lambda i: (i // per_b, 0, 0)),
                  pl.BlockSpec((d, n), lambda i: (0, 0))],
        out_specs=pl.BlockSpec((tm, n), lambda i: (i, 0)),
        out_shape=jax.ShapeDtypeStruct((t, n), F32),
        compiler_params=_cparams(("arbitrary",)),
        name="in_projection",
    )(x, g, shift, scale, w)


def _outproj_kernel(*refs, n_in):
    o_refs = refs[:n_in]
    w_refs = refs[n_in:2 * n_in]
    x_ref, gt_ref, xo_ref = refs[2 * n_in:]
    acc = _dot(o_refs[0][...].astype(BF16), w_refs[0][...])
    for o_ref, w_ref in zip(o_refs[1:], w_refs[1:]):
        acc = acc + _dot(o_ref[...].astype(BF16), w_ref[...])
    xo_ref[...] = x_ref[...] + gt_ref[0] * acc


def out_projection(parts, weights, x, gate, seq):
    t, d = x.shape
    tm = 512
    per_b = seq // tm
    n_in = len(parts)
    in_specs = [pl.BlockSpec((tm, p.shape[1]), lambda i: (i, 0)) for p in parts]
    in_specs += [pl.BlockSpec(w.shape, lambda i: (0, 0)) for w in weights]
    in_specs += [pl.BlockSpec((tm, d), lambda i: (i, 0)),
                 pl.BlockSpec((1, 1, d), lambda i: (i // per_b, 0, 0))]
    return pl.pallas_call(
        functools.partial(_outproj_kernel, n_in=n_in),
        grid=(t // tm,),
        in_specs=in_specs,
        out_specs=pl.BlockSpec((tm, d), lambda i: (i, 0)),
        out_shape=jax.ShapeDtypeStruct((t, d), F32),
        compiler_params=_cparams(("arbitrary",)),
        name="out_projection",
    )(*parts, *weights, x, gate)


def _tri_masks(c):
    row = lax.broadcasted_iota(I32, (c, c), 0)
    col = lax.broadcasted_iota(I32, (c, c), 1)
    return row >= col, row > col, row == col


def _gdn_kernel(qkv_ref, za_ref, sm_ref, cw_ref, nega_ref, dtb_ref, on_ref, o_ref,
                tail_ref, state_ref, *, rb):
    c = GDN_CHUNK
    dk = GDN_DK

    @pl.when(pl.program_id(1) == 0)
    def _():
        tail_ref[...] = jnp.zeros_like(tail_ref)
        state_ref[...] = jnp.zeros_like(state_ref)

    x = qkv_ref[...]
    xx = jnp.concatenate([tail_ref[...], x], axis=0)
    tail_ref[...] = x[rb - SUBLANES:, :]
    cw = cw_ref[...]
    y = x * cw[CONV_WIDTH - 1:CONV_WIDTH, :]
    for i in range(1, CONV_WIDTH):
        y = y + pltpu.roll(xx, i, axis=0)[SUBLANES:, :] * cw[CONV_WIDTH - 1 - i:CONV_WIDTH - i, :]
    y = _silu(y)

    sm = sm_ref[...]
    beta_t = _sigmoid(sm)
    g_t = nega_ref[...] * _softplus(sm + dtb_ref[...])
    za = za_ref[...]
    hw = GDN_HEADS * dk
    nch = rb // c
    row = lax.broadcasted_iota(I32, (rb, rb), 0)
    col = lax.broadcasted_iota(I32, (rb, rb), 1)
    shift = int(np.log2(c))
    same = (row >> shift) == (col >> shift)
    incl = same & (row >= col)
    strict = same & (row > col)
    eye = (row == col).astype(F32)
    incl_b = jnp.where(incl, 1.0, 0.0).astype(BF16)
    gc_all = _dot_mask(incl_b, g_t)

    heads = range(GDN_HEADS)
    qs, ks, vs, bbs, gcs, decays = [], [], [], [], [], []
    for h in heads:
        lo = h * dk
        q = y[:, lo:lo + dk]
        k = y[:, hw + lo:hw + lo + dk]
        qs.append(q * lax.rsqrt(jnp.sum(q * q, axis=-1, keepdims=True) + EPS) * dk ** -0.5)
        ks.append(k * lax.rsqrt(jnp.sum(k * k, axis=-1, keepdims=True) + EPS))
        vs.append(y[:, 2 * hw + lo:2 * hw + lo + dk])
        bbs.append(jnp.broadcast_to(beta_t[:, SM_BA + h:SM_BA + h + 1], (rb, dk)))
        gc = jnp.broadcast_to(gc_all[:, SM_AA + h:SM_AA + h + 1], (rb, dk))
        gc_t = jnp.concatenate([gc[r:r + dk].T for r in range(0, rb, dk)], axis=1)[:1]
        gc_i = jnp.concatenate([gc] * (rb // dk), axis=1)
        decays.append(jnp.where(incl, jnp.exp(jnp.where(incl, gc_i - gc_t, 0.0)), 0.0))
        gcs.append(gc)
    kbetas = [ks[h] * bbs[h] for h in heads]
    ypows = [jnp.where(strict, -(_bdot_nt(kbetas[h], ks[h]) * decays[h]), 0.0) for h in heads]
    tmats = [eye + ypows[h] for h in heads]
    for _ in range(int(np.log2(c)) - 1):
        ypows = [_bdot(ypows[h], ypows[h]) for h in heads]
        tmats = [tmats[h] + _bdot(tmats[h], ypows[h]) for h in heads]
    egcs = [jnp.exp(gcs[h]) for h in heads]
    wus = [_bdot(tmats[h], jnp.concatenate([kbetas[h] * egcs[h], vs[h] * bbs[h]], axis=1))
           for h in heads]
    aqks = [_bdot_nt(qs[h], ks[h]) * decays[h] for h in heads]
    qgs = [qs[h] * egcs[h] for h in heads]
    sts = [state_ref[h] for h in heads]
    v_news = [[] for _ in heads]
    o_inter = [[] for _ in heads]
    for ci in range(nch):
        r0 = ci * c
        for h in heads:
            gl = gcs[h][r0 + c - 1:r0 + c, :]
            kd = ks[h][r0:r0 + c] * jnp.exp(gl - gcs[h][r0:r0 + c])
            ws = _bdot(jnp.concatenate([wus[h][r0:r0 + c, :dk], qgs[h][r0:r0 + c]], axis=0), sts[h])
            v_new = wus[h][r0:r0 + c, dk:] - ws[:c]
            o_inter[h].append(ws[c:])
            sts[h] = sts[h] * jnp.exp(gl) + _bdot_tn(kd, v_new)
            v_news[h].append(v_new)
    for h in heads:
        lo = h * dk
        state_ref[h] = sts[h]
        o = jnp.concatenate(o_inter[h], axis=0) + _bdot(aqks[h], jnp.concatenate(v_news[h], axis=0))
        o = o * lax.rsqrt(jnp.mean(o * o, axis=-1, keepdims=True) + EPS) * on_ref[...]
        o_ref[:, lo:lo + dk] = o * _silu(za[:, lo:lo + dk])


def gdn_mixer(proj, conv_w, a_log, dt_bias, out_norm, batch, seq):
    t = proj.shape[0]
    rb = 256
    nsb = seq // rb
    hw = GDN_HEADS * GDN_DK
    nega =jnp.zeros((1, LANES), F32).at[0, SM_AA:SM_AA + GDN_HEADS].set(-jnp.exp(a_log))
    dtb = jnp.zeros((1, LANES), F32).at[0, SM_AA:SM_AA + GDN_HEADS].set(dt_bias)
    row = lambda b, s: b * nsb + s
    return pl.pallas_call(
        functools.partial(_gdn_kernel, rb=rb),
        grid=(batch, nsb),
        in_specs=[pl.BlockSpec((rb, 3 * hw), lambda b, s: (row(b, s), 0)),
                  pl.BlockSpec((rb, hw), lambda b, s: (row(b, s), EV_ZA // hw)),
                  pl.BlockSpec((rb, LANES), lambda b, s: (row(b, s), EV_SM // LANES)),
                  pl.BlockSpec((CONV_WIDTH, 3 * hw), lambda b, s: (0, 0)),
                  pl.BlockSpec((1, LANES), lambda b, s: (0, 0)),
                  pl.BlockSpec((1, LANES), lambda b, s: (0, 0)),
                  pl.BlockSpec((1, GDN_DK), lambda b, s: (0, 0))],
        out_specs=pl.BlockSpec((rb, hw), lambda b, s: (row(b, s), 0)),
        out_shape=jax.ShapeDtypeStruct((t, hw), F32),
        scratch_shapes=[pltpu.VMEM((SUBLANES, 3 * hw), F32),
                        pltpu.VMEM((GDN_HEADS, GDN_DK, GDN_DK), F32)],
        compiler_params=_cparams(("arbitrary", "arbitrary")),
        name="gdn_mixer",
    )(proj, proj, proj, conv_w, nega, dtb, out_norm.reshape(1, GDN_DK))


def _gla_kernel(qk_ref, v_ref, r_ref, gl_ref, wg_ref, bg_ref, on_ref, o_ref, state_ref, *, rb):
    c = GLA_CHUNK
    dk, dv = GLA_DK, GLA_DV
    sb = GLA_SUB

    @pl.when(pl.program_id(1) == 0)
    def _():
        state_ref[...] = jnp.zeros_like(state_ref)

    z = _dot3(gl_ref[...], wg_ref[...]) + bg_ref[...]
    log_a = -_softplus(-z) * (1.0 / GLA_TAU)
    qk = qk_ref[...]
    vv = v_ref[...]
    rr = r_ref[...]
    incl, _, _ = _tri_masks(c)
    incl_b = incl.astype(F32).astype(BF16)
    hw = GLA_HEADS * dk
    sub_row = lax.broadcasted_iota(I32, (sb, dk), 0)
    att_lane = lax.broadcasted_iota(I32, (sb, c), 1)

    for ci in range(rb // c):
        r0 = ci * c
        b_all = _dot_mask(incl_b, log_a[r0:r0 + c])
        for h in range(GLA_HEADS):
            q = qk[r0:r0 + c, h * dk:(h + 1) * dk] * dk ** -0.5
            k = qk[r0:r0 + c, hw + h * dk:hw + (h + 1) * dk]
            v = vv[r0:r0 + c, h * dv:(h + 1) * dv]
            b = b_all[:, h * dk:(h + 1) * dk]
            st = state_ref[h]
            o = _bdot_nt(q * jnp.exp(b), st)
            att_rows = []
            for bi in range(c // sb):
                s0 = bi * sb
                qi = q[s0:s0 + sb]
                bi_ = b[s0:s0 + sb]
                ref = b[s0:s0 + 1]
                if bi > 0:
                    qt = qi * jnp.exp(bi_ - ref)
                    kt = k * jnp.exp(jnp.minimum(ref - b, 0.0))
                    att = jnp.where(att_lane < s0, _bdot_nt(qt, kt), 0.0)
                else:
                    att = jnp.zeros((sb, c), F32)
                for j in range(sb):
                    kj = k[s0 + j:s0 + j + 1]
                    bj = b[s0 + j:s0 + j + 1]
                    e = jnp.exp(jnp.where(sub_row >= j, bi_ - bj, NEG_INF))
                    col = jnp.sum(qi * kj * e, axis=-1, keepdims=True)
                    att = jnp.where(att_lane == s0 + j, col, att)
                att_rows.append(att)
            o = o + _bdot(jnp.concatenate(att_rows, axis=0), v)
            bl = b[c - 1:c]
            state_ref[h] = st * jnp.exp(bl) + _bdot_tn(v, k * jnp.exp(bl - b))
            o = o * lax.rsqrt(jnp.mean(o * o, axis=-1, keepdims=True) + EPS) * on_ref[...]
            o_ref[r0:r0 + c, h * dv:(h + 1) * dv] = o * _silu(rr[r0:r0 + c, h * dv:(h + 1) * dv])


def gla_mixer(proj, w_gate2, b_gate2, out_norm, batch, seq):
    t = proj.shape[0]
    rb = 4 * GLA_CHUNK
    nsb = seq // rb
    hw = GLA_HEADS * GLA_DK
    hv = GLA_HEADS * GLA_DV
    wg = jnp.zeros((LANES, hw), F32).at[:GLA_RANK].set(w_gate2)
    row = lambda b, s: b * nsb + s
    return pl.pallas_call(
        functools.partial(_gla_kernel, rb=rb),
        grid=(batch, nsb),
        in_specs=[pl.BlockSpec((rb, 2 * hw), lambda b, s: (row(b, s), 0)),
                  pl.BlockSpec((rb, hv), lambda b, s: (row(b, s), OD_V // hv)),
                  pl.BlockSpec((rb, hv), lambda b, s: (row(b, s), OD_R // hv)),
                  pl.BlockSpec((rb, LANES), lambda b, s: (row(b, s), OD_G // LANES)),
                  pl.BlockSpec((LANES, hw), lambda b, s: (0, 0)),
                  pl.BlockSpec((1, hw), lambda b, s: (0, 0)),
                  pl.BlockSpec((1, GLA_DV), lambda b, s: (0, 0))],
        out_specs=pl.BlockSpec((rb, hv), lambda b, s: (row(b, s), 0)),
        out_shape=jax.ShapeDtypeStruct((t, hv), F32),
        scratch_shapes=[pltpu.VMEM((GLA_HEADS, GLA_DV, GLA_DK), F32)],
        compiler_params=_cparams(("arbitrary", "arbitrary")),
        name="gla_mixer",
    )(proj, proj, proj, proj, wg, b_gate2.reshape(1, hw), out_norm.reshape(1, GLA_DV))


def _rope_table_kernel(pos_ref, inva_ref, invi_ref, ca_ref, sa_ref, ci_ref, si_ref):
    pos = pos_ref[...].astype(F32)
    lane = lax.broadcasted_iota(I32, (pos.shape[0], LANES), 1)
    ang_a = pos * inva_ref[...]
    ca_ref[...] = jnp.cos(ang_a)
    sa_ref[...] = jnp.where(lane < ATT_DH // 2, -1.0, 1.0) * jnp.sin(ang_a)
    ang_i = pos * invi_ref[...]
    ci_ref[...] = jnp.cos(ang_i)
    si_ref[...] = jnp.where(lane % IDX_DH < IDX_DH // 2, -1.0, 1.0) * jnp.sin(ang_i)


def rope_tables(positions):
    t = positions.size
    inv_a = 1.0 / (ROPE_THETA ** (jnp.arange(0, ATT_DH, 2, dtype=F32) / ATT_DH))
    inv_i = 1.0 / (ROPE_THETA ** (jnp.arange(0, IDX_DH, 2, dtype=F32) / IDX_DH))
    inv_a = jnp.tile(inv_a, 2).reshape(1, LANES)
    inv_i = jnp.tile(inv_i, 4).reshape(1, LANES)
    tm = 512
    spec = pl.BlockSpec((tm, LANES), lambda i: (i, 0))
    vec = pl.BlockSpec((1, LANES), lambda i: (0, 0))
    return pl.pallas_call(
        _rope_table_kernel,
        grid=(t // tm,),
        in_specs=[pl.BlockSpec((tm, 1), lambda i: (i, 0)), vec, vec],
        out_specs=[spec] * 4,
        out_shape=[jax.ShapeDtypeStruct((t, LANES), F32)] * 4,
        compiler_params=_cparams(("arbitrary",)),
        name="rope_tables",
    )(positions.reshape(t, 1), inv_a, inv_i)


def _dsa_prep_kernel(qb_ref, kb_ref, vb_ref, qi_ref, sm_ref, ca_ref, sa_ref, ci_ref, si_ref,
                     qo_ref, ko_ref, vt_ref, qio_ref, kio_ref, kn_ref):
    ca, sa, ci, si = ca_ref[...], sa_ref[...], ci_ref[...], si_ref[...]
    lane = lax.broadcasted_iota(I32, ca.shape, 1)
    first_half = lane % IDX_DH < IDX_DH // 2
    knorm = jnp.zeros((SUBLANES, LANES), F32)
    lane8 = lax.broadcasted_iota(I32, (SUBLANES, LANES), 1)

    def rope_att(tile):
        return tile * ca + pltpu.roll(tile, ATT_DH // 2, axis=1) * sa

    def rope_idx(tile):
        partner = jnp.where(first_half, pltpu.roll(tile, LANES - IDX_DH // 2, axis=1),
                            pltpu.roll(tile, IDX_DH // 2, axis=1))
        return tile * ci + partner * si

    qb, kb = qb_ref[...], kb_ref[...]
    for h in range(ATT_HEADS):
        sl = slice(h * ATT_DH, (h + 1) * ATT_DH)
        qo_ref[:, sl] = (rope_att(qb[:, sl]) * ATT_DH ** -0.5).astype(BF16)
        kr = rope_att(kb[:, sl]).astype(BF16)
        ko_ref[:, sl] = kr
        krf = kr.astype(F32)
        knorm = jnp.where(lane8 == h, jnp.max(jnp.sum(krf * krf, axis=1, keepdims=True)), knorm)
    kn_ref[0] = knorm
    vt = vb_ref[...].T.astype(BF16)
    ones = jnp.ones((DSA_SUM_ROWS, vt.shape[1]), BF16)
    for h in range(ATT_HEADS):
        vt_ref[0, h * DSA_VT_ROWS:h * DSA_VT_ROWS + ATT_DH, :] = vt[h * ATT_DH:(h + 1) * ATT_DH]
        vt_ref[0, h * DSA_VT_ROWS + ATT_DH:(h + 1) * DSA_VT_ROWS, :] = ones
    qi = qi_ref[...]
    for p in range(IDX_HEADS * IDX_DH // LANES):
        sl = slice(p * LANES, (p + 1) * LANES)
        qio_ref[:, sl] = rope_idx(qi[:, sl]).astype(BF16)
    ki = rope_idx(sm_ref[...])
    kio_ref[...] = jnp.where(lane < IDX_DH, ki, pltpu.roll(ki, IDX_DH, axis=1)).astype(BF16)


def dsa_prep(proj, tables, seq):
    t = proj.shape[0]
    tm = min(DSA_KEY_CHUNK, seq)
    aw = ATT_HEADS * ATT_DH
    iw = IDX_HEADS * IDX_DH
    tab = pl.BlockSpec((tm, LANES), lambda i: (i, 0))
    return pl.pallas_call(
        _dsa_prep_kernel,
        grid=(t // tm,),
        in_specs=[pl.BlockSpec((tm, aw), lambda i: (i, EV_QB // aw)),
                  pl.BlockSpec((tm, aw), lambda i: (i, EV_KB // aw)),
                  pl.BlockSpec((tm, aw), lambda i: (i, EV_VB // aw)),
                  pl.BlockSpec((tm, iw), lambda i: (i, EV_QI // iw)),
                  pl.BlockSpec((tm, LANES), lambda i: (i, EV_SM // LANES)),
                  tab, tab, tab, tab],
        out_specs=[pl.BlockSpec((tm, aw), lambda i: (i, 0)),
                   pl.BlockSpec((tm, aw), lambda i: (i, 0)),
                   pl.BlockSpec((1, ATT_HEADS * DSA_VT_ROWS, tm), lambda i: (i, 0, 0)),
                   pl.BlockSpec((tm, iw), lambda i: (i, 0)),
                   pl.BlockSpec((tm, LANES), lambda i: (i, 0)),
                   pl.BlockSpec((1, SUBLANES, LANES), lambda i: (i, 0, 0))],
        out_shape=[jax.ShapeDtypeStruct((t, aw), BF16),
                   jax.ShapeDtypeStruct((t, aw), BF16),
                   jax.ShapeDtypeStruct((t // tm, ATT_HEADS * DSA_VT_ROWS, tm), BF16),
                   jax.ShapeDtypeStruct((t, iw), BF16),
                   jax.ShapeDtypeStruct((t, LANES), BF16),
                   jax.ShapeDtypeStruct((t // tm, SUBLANES, LANES), F32)],
        compiler_params=_cparams(("arbitrary",)),
        name="dsa_prep",
    )(proj, proj, proj, proj, proj, *tables)


def _fold_rows(x, op):
    parts = [x[r:r + SUBLANES] for r in range(0, x.shape[0], SUBLANES)]
    while len(parts) > 1:
        parts = [op(parts[j], parts[j + 1]) for j in range(0, len(parts), 2)]
    return parts[0]


def _dsa_kernel(qb_ref, qi_ref, sm_ref, kn_ref, k_ref, vt_ref, ki_ref, o_ref,
                key_ref, l_ref, acc_ref, *, kc, topk):
    nq = Q_BLOCK
    i = pl.program_id(1)
    nkc = (i * nq + nq + kc - 1) // kc
    q_pos = i * nq + lax.broadcasted_iota(I32, (1, nq), 1)
    key0 = lax.broadcasted_iota(I32, (kc, nq), 0)
    lane = lax.broadcasted_iota(I32, (nq, LANES), 1)
    wi_t = sm_ref[...].T * (IDX_HEADS * IDX_DH) ** -0.5
    qi = qi_ref[...]
    zero_b = jnp.zeros((nq, LANES), BF16)
    qi_heads = []
    for h in range(IDX_HEADS):
        pair = qi[:, (h // 2) * LANES:(h // 2 + 1) * LANES]
        own = (lane < IDX_DH) if h % 2 == 0 else (lane >= IDX_DH)
        qi_heads.append(jnp.where(own, pair, zero_b))

    def score_body(c, carry):
        start = pl.multiple_of(c * kc, kc)
        kic = ki_ref[0, pl.ds(start, kc), :]
        dots = [_dot_nt(kic, qi_heads[h]) for h in range(IDX_HEADS)]
        s = jnp.zeros((kc, nq), F32)
        for h in range(IDX_HEADS):
            s = s + wi_t[SM_WI + h:SM_WI + h + 1, :] * jnp.maximum(dots[h], 0.0)
        s = s + 0.0
        s = jnp.where(key0 <= q_pos - c * kc, s, NEG_INF)
        bits = pltpu.bitcast(s, I32)
        key_ref[c] = bits ^ ((bits >> 31) & 0x7FFFFFFF)
        return carry

    lax.fori_loop(0, nkc, score_body, 0)

    halves = [slice(j * LANES, (j + 1) * LANES) for j in range(nq // LANES)]
    key0h = key0[:, :LANES]
    zero8 = jnp.zeros((SUBLANES, LANES), I32)

    def count(fns, *rows):
        def body(c, accs):
            out = []
            for j, sl in enumerate(halves):
                key = key_ref[c, :, sl]
                args = [r[:, sl] for r in rows]
                for f, fn in enumerate(fns):
                    hit = jnp.where(fn(key, c, *args), 1, 0).astype(I32)
                    out.append(accs[j * len(fns) + f] + _fold_rows(hit, jnp.add))
            return tuple(out)

        accs = lax.fori_loop(0, nkc, body, (zero8,) * (len(halves) * len(fns)))
        return [jnp.concatenate([jnp.sum(accs[j * len(fns) + f], axis=0, keepdims=True)
                                 for j in range(len(halves))], axis=1) for f in range(len(fns))]

    thr = jnp.where(count([lambda key, c: key >= 0])[0] >= topk, 0, INT_MIN).astype(I32)

    def bit_body(it, thr):
        cand = thr + (jnp.int32(1) << (30 - it))
        return jnp.where(count([lambda key, c, cd: key >= cd], cand)[0] >= topk, cand, thr)

    thr = lax.fori_loop(0, 31, bit_body, thr)

    n_gt, n_eq = count([lambda key, c, t: key > t, lambda key, c, t: key == t], thr)
    need = topk - n_gt
    big = jnp.int32(2 ** 30)

    def tie_search():
        def idx_body(it, x):
            cand = x + (jnp.int32(1) << (14 - it))
            below = count([lambda key, c, t, cd: key0h < jnp.where(key == t, cd - c * kc, 0)],
                          thr, cand)[0]
            return jnp.where(below < need, cand, x)
        return lax.fori_loop(0, 15, idx_body, jnp.zeros((1, nq), I32))

    any_tie = jnp.max(jnp.where(n_eq > need, 1, 0)) > 0
    last = lax.cond(any_tie, tie_search, lambda: jnp.full((1, nq), big, I32))

    last = jnp.where(thr == KEY_NEG_INF, -1, last)
    zero_f = jnp.float32(0.0)
    ninf_f = jnp.float32(NEG_INF)
    qb = qb_ref[...]

    def select_bias(c):
        key = key_ref[c]
        tie_ok = jnp.where(key0 <= last - c * kc, zero_f, ninf_f)
        return jnp.where(key > thr, zero_f, jnp.where(key == thr, tie_ok, ninf_f))

    def chunk_logits(c, h, bias):
        start = pl.multiple_of(c * kc, kc)
        kch = k_ref[0, pl.ds(start, kc), h * ATT_DH:(h + 1) * ATT_DH]
        return _dot_nt(kch, qb[:, h * ATT_DH:(h + 1) * ATT_DH]) + bias

    def attend(shifts):
        acc_ref[...] = jnp.zeros(acc_ref.shape, F32)

        def body(c, carry):
            bias = select_bias(c)
            heads = range(ATT_HEADS)
            logits = [chunk_logits(c, h, bias) for h in heads]
            ps = [jnp.exp(logits[h] - shifts[h]) for h in heads]
            for h in heads:
                vth = vt_ref[0, c, h * DSA_VT_ROWS:(h + 1) * DSA_VT_ROWS, :]
                acc_ref[h] = acc_ref[h] + _dot(vth, ps[h].astype(BF16))
            return carry

        lax.fori_loop(0, nkc, body, 0)
        return [acc_ref[h, ATT_DH:ATT_DH + 1, :] for h in range(ATT_HEADS)]

    ones_b = jnp.ones((SUBLANES, ATT_DH), BF16)
    bounds = []
    for h in range(ATT_HEADS):
        qh = qb[:, h * ATT_DH:(h + 1) * ATT_DH].astype(F32)
        qsq = _dot_nt(ones_b, (qh * qh).astype(BF16))[0:1, :]
        bounds.append(jnp.sqrt(qsq * kn_ref[0][0:1, h:h + 1]) * 1.02)
    sums = attend(bounds)

    def write_out(sums):
        for h in range(ATT_HEADS):
            o_ref[:, h * ATT_DH:(h + 1) * ATT_DH] = (acc_ref[h, :ATT_DH, :] / sums[h]).T

    write_out(sums)
    lowest = jnp.minimum(jnp.minimum(sums[0], sums[1]), jnp.minimum(sums[2], sums[3]))

    @pl.when(jnp.min(lowest) <= 0.0)
    def _():
        l_ref[...] = jnp.full(l_ref.shape, NEG_INF, F32)

        def max_body(c, carry):
            bias = select_bias(c)
            for h in range(ATT_HEADS):
                l_ref[h] = jnp.maximum(l_ref[h], _fold_rows(chunk_logits(c, h, bias), jnp.maximum))
            return carry

        lax.fori_loop(0, nkc, max_body, 0)
        exact = [jnp.max(l_ref[h], axis=0, keepdims=True) for h in range(ATT_HEADS)]
        write_out(attend(exact))


def dsa_attention(proj, qb, kb, vt, qi, ki, knorm, batch, seq):
    t = proj.shape[0]
    aw = ATT_HEADS * ATT_DH
    iw = IDX_HEADS * IDX_DH
    kc = min(DSA_KEY_CHUNK, seq)
    nkc = seq // kc
    topk = min(TOPK_MAX, seq // 4)
    nq = seq // Q_BLOCK
    row = lambda b, i: b * nq + i
    kb3, ki3 = (a.reshape(batch, seq, a.shape[1]) for a in (kb, ki))
    vw = ATT_HEADS * DSA_VT_ROWS
    vt4 = vt.reshape(batch, nkc, vw, kc)
    kmax = jnp.max(knorm.reshape(batch, nkc, SUBLANES, LANES), axis=1)
    resident = lambda shape: pl.BlockSpec((1,) + shape, lambda b, i: (b,) + (0,) * len(shape),
                                          pipeline_mode=pl.Buffered(1))
    return pl.pallas_call(
        functools.partial(_dsa_kernel, kc=kc, topk=topk),
        grid=(batch, nq),
        in_specs=[pl.BlockSpec((Q_BLOCK, aw), lambda b, i: (row(b, i), 0)),
                  pl.BlockSpec((Q_BLOCK, iw), lambda b, i: (row(b, i), 0)),
                  pl.BlockSpec((Q_BLOCK, LANES), lambda b, i: (row(b, i), EV_SM // LANES)),
                  pl.BlockSpec((1, SUBLANES, LANES), lambda b, i: (b, 0, 0)),
                  resident((seq, aw)), resident((nkc, vw, kc)), resident((seq, LANES))],
        out_specs=pl.BlockSpec((Q_BLOCK, aw), lambda b, i: (row(b, i), 0)),
        out_shape=jax.ShapeDtypeStruct((t, aw), F32),
        scratch_shapes=[pltpu.VMEM((nkc, kc, Q_BLOCK), I32),
                        pltpu.VMEM((ATT_HEADS, SUBLANES, Q_BLOCK), F32),
                        pltpu.VMEM((ATT_HEADS, DSA_VT_ROWS, Q_BLOCK), F32)],
        compiler_params=_cparams(("arbitrary", "arbitrary")),
        name="dsa_attention",
    )(qb, qi, proj, kmax, kb3, vt4, ki3)


def _router_kernel(x_ref, g_ref, sh_ref, sc_ref, wr_ref, br_ref, h_ref, route_ref, cnt_ref,
                   carry_ref):
    @pl.when(pl.program_id(0) == 0)
    def _():
        carry_ref[...] = jnp.zeros_like(carry_ref)

    h = _norm_mod(x_ref[...], g_ref[...], sh_ref[0], sc_ref[0])
    h_ref[...] = h
    tm = h.shape[0]
    logits = _dot3(h, wr_ref[...]) + br_ref[...]
    lane = lax.broadcasted_iota(I32, (tm, LANES), 1)
    far = jnp.int32(LANES)

    def first_max(vals):
        m = jnp.max(vals, axis=1, keepdims=True)
        return m, jnp.min(jnp.where(vals == m, lane, far), axis=1, keepdims=True)

    is_group = lane < N_GROUPS
    gmax, gsel = first_max(jnp.where(is_group, logits, NEG_INF))
    gsum = jnp.sum(jnp.where(is_group, jnp.exp(logits - gmax), 0.0), axis=1, keepdims=True)
    lo = N_GROUPS + EXPERTS_PER_GROUP * gsel
    el = jnp.where((lane >= lo) & (lane < lo + EXPERTS_PER_GROUP), logits, NEG_INF)
    m1, i1 = first_max(el)
    m2, i2 = first_max(jnp.where(lane == i1, NEG_INF, el))
    e2 = jnp.exp(m2 - m1)
    gate1 = 1.0 / (gsum * (1.0 + e2))
    gate2 = e2 * gate1
    eid1 = i1 - N_GROUPS
    eid2 = i2 - N_GROUPS
    oh1 = lane == eid1
    oh2 = lane == eid2
    onehot = jnp.where(oh1 | oh2, 1.0, 0.0)
    r = lax.broadcasted_iota(I32, (tm, tm), 0)
    cidx = lax.broadcasted_iota(I32, (tm, tm), 1)
    below = jnp.where(r > cidx, 1.0, 0.0).astype(BF16)
    cum = _dot(below, onehot.astype(BF16)) + carry_ref[0:1, :]
    rank1 = jnp.sum(jnp.where(oh1, cum, 0.0), axis=1, keepdims=True)
    rank2 = jnp.sum(jnp.where(oh2, cum, 0.0), axis=1, keepdims=True)
    carry_ref[...] = carry_ref[...] + jnp.sum(onehot, axis=0, keepdims=True)
    cnt_ref[...] = carry_ref[...]
    out = jnp.zeros((tm, LANES), F32)
    for j, val in enumerate((eid1.astype(F32), eid2.astype(F32), gate1, gate2, rank1, rank2)):
        out = jnp.where(lane == j, val, out)
    route_ref[...] = out


def moe_router(x, g, shift, scale, wr, br, seq):
    t, d = x.shape
    tm = 512
    per_b = seq // tm
    return pl.pallas_call(
        _router_kernel,
        grid=(t // tm,),
        in_specs=[pl.BlockSpec((tm, d), lambda i: (i, 0)),
                  pl.BlockSpec((1, d), lambda i: (0, 0)),
                  pl.BlockSpec((1, 1, d), lambda i: (i // per_b, 0, 0)),
                  pl.BlockSpec((1, 1, d), lambda i: (i // per_b, 0, 0)),
                  pl.BlockSpec((d, LANES), lambda i: (0, 0)),
                  pl.BlockSpec((1, LANES), lambda i: (0, 0))],
        out_specs=[pl.BlockSpec((tm, d), lambda i: (i, 0)),
                   pl.BlockSpec((tm, LANES), lambda i: (i, 0)),
                   pl.BlockSpec((SUBLANES, LANES), lambda i: (0, 0))],
        out_shape=[jax.ShapeDtypeStruct((t, d), F32),
                   jax.ShapeDtypeStruct((t, LANES), F32),
                   jax.ShapeDtypeStruct((SUBLANES, LANES), F32)],
        scratch_shapes=[pltpu.VMEM((SUBLANES, LANES), F32)],
        compiler_params=_cparams(("arbitrary",)),
        name="moe_router",
    )(x, g, shift, scale, wr, br)


def _zero_kernel(o_ref):
    o_ref[...] = jnp.zeros_like(o_ref)


def zero_rows(rows, d):
    tm = 1024
    return pl.pallas_call(
        _zero_kernel,
        grid=(rows // tm,),
        out_specs=pl.BlockSpec((tm, d), lambda i: (i, 0)),
        out_shape=jax.ShapeDtypeStruct((rows, d), F32),
        compiler_params=_cparams(("arbitrary",)),
        name="zero_rows",
    )()


def _dispatch_kernel(dest_ref, seg_ref, h_ref, xs_in_ref, xs_ref, inv_ref, sem, *, tm):
    del xs_in_ref
    i = pl.program_id(0)
    base = i * tm * 2

    rows = MOE_ROWS
    n_assign = dest_ref.shape[0]

    @pl.when(i == 0)
    def _():
        def mark(p, carry):
            blk = p // rows
            spare = ((blk + MOE_OUT_BUFS - 1) % MOE_OUT_BUFS) * rows + (p - blk * rows)
            inv_ref[p] = n_assign + spare
            return carry
        lax.fori_loop(0, rows, mark, 0)
        for e in range(N_EXPERTS):
            lax.fori_loop(rows + seg_ref[e], rows + seg_ref[N_EXPERTS + e], mark, 0)
        lax.fori_loop(rows + seg_ref[2 * N_EXPERTS - 1], inv_ref.shape[0], mark, 0)

    def row_copy(r, d):
        return pltpu.make_async_copy(h_ref.at[pl.ds(r, 1)], xs_ref.at[pl.ds(d, 1)], sem)

    n_tok = n_assign // 2

    group = 4

    def start(g, carry):
        r0 = g * group
        ds = [[dest_ref[base + 2 * (r0 + k) + j] for j in range(2)] for k in range(group)]
        for k in range(group):
            for j in range(2):
                row_copy(r0 + k, ds[k][j]).start()
        for k in range(group):
            for j in range(2):
                inv_ref[rows + ds[k][j]] = j * n_tok + i * tm + r0 + k
        return carry

    def wait(r, carry):
        row_copy(0, 0).wait()
        row_copy(0, 0).wait()
        return carry

    lax.fori_loop(0, tm // group, start, 0)
    lax.fori_loop(0, tm, wait, 0)


def moe_dispatch(dest, seg, h, xs_zero):
    t, d = h.shape
    tm = 256
    return pl.pallas_call(
        functools.partial(_dispatch_kernel, tm=tm),
        grid_spec=pltpu.PrefetchScalarGridSpec(
            num_scalar_prefetch=2,
            grid=(t // tm,),
            in_specs=[pl.BlockSpec((tm, d), lambda i, dest, seg: (i, 0)),
                      pl.BlockSpec(memory_space=pl.ANY)],
            out_specs=[pl.BlockSpec(memory_space=pl.ANY),
                       pl.BlockSpec(memory_space=pltpu.SMEM)],
            scratch_shapes=[pltpu.SemaphoreType.DMA(())]),
        out_shape=[jax.ShapeDtypeStruct(xs_zero.shape, F32),
                   jax.ShapeDtypeStruct((xs_zero.shape[0] + MOE_ROWS,), I32)],
        input_output_aliases={3: 0},
        compiler_params=_cparams(("arbitrary",)),
        name="moe_dispatch",
    )(dest, seg, h, xs_zero)


MOE_OUT_BUFS = 3


def _expert_kernel(be_ref, inv_ref, xs_ref, w1_ref, w3_ref, w2_ref, y_ref,
                   w1b, w3b, w2b, ybuf, sems, *, n_assign):
    i = pl.program_id(0)
    last = pl.num_programs(0) - 1
    rows = ybuf.shape[1]
    slot = i % MOE_OUT_BUFS
    prev_slot = (i + MOE_OUT_BUFS - 1) % MOE_OUT_BUFS
    prev = be_ref[jnp.maximum(i - 1, 0)]

    def row_copy(s, r, dst):
        return pltpu.make_async_copy(ybuf.at[s, pl.ds(r, 1)], y_ref.at[pl.ds(dst, 1)], sems.at[s])

    def send(s, block):
        for r in range(rows):
            row_copy(s, r, inv_ref[(block + 1) * rows + r]).start()

    def drain(s):
        for r in range(rows):
            row_copy(s, 0, 0).wait()

    @pl.when(i == 0)
    def _():
        ybuf[prev_slot] = jnp.zeros((rows, ybuf.shape[2]), F32)
        for s in range(MOE_OUT_BUFS):
            if s != MOE_OUT_BUFS - 1:
                for r in range(rows):
                    row_copy(prev_slot, r, n_assign + s * rows + r).start()
                drain(prev_slot)

    @pl.when(i >= MOE_OUT_BUFS - 1)
    def _():
        drain(slot)

    @pl.when((i == 0) | (be_ref[i] != prev))
    def _():
        w1b[...] = w1_ref[0, 0].astype(BF16)
        w3b[...] = w3_ref[0, 0].astype(BF16)
        w2b[...] = w2_ref[0, 0].astype(BF16)

    send(prev_slot, i - 1)
    x = xs_ref[...].astype(BF16)
    a = _dot(x, w1b[...])
    hid = _silu(a) * _dot(x, w3b[...])
    ybuf[slot] = _dot(hid.astype(BF16), w2b[...])

    @pl.when(i == last)
    def _():
        send(slot, i)
        for s in range(MOE_OUT_BUFS):
            drain(s)


def moe_experts(blk_exp, inv, xs, w1, w3, w2, layer, t):
    p, d = xs.shape
    de = w1.shape[3]
    rows = MOE_ROWS
    wmap = lambda i, be, inv: (layer, be[i], 0, 0)
    return pl.pallas_call(
        functools.partial(_expert_kernel, n_assign=2 * t),
        grid_spec=pltpu.PrefetchScalarGridSpec(
            num_scalar_prefetch=2,
            grid=(p // rows,),
            in_specs=[pl.BlockSpec((rows, d), lambda i, be, inv: (i, 0)),
                      pl.BlockSpec((1, 1, d, de), wmap),
                      pl.BlockSpec((1, 1, d, de), wmap),
                      pl.BlockSpec((1, 1, de, d), wmap)],
            out_specs=pl.BlockSpec(memory_space=pl.ANY),
            scratch_shapes=[pltpu.VMEM((d, de), BF16), pltpu.VMEM((d, de), BF16),
                            pltpu.VMEM((de, d), BF16), pltpu.VMEM((MOE_OUT_BUFS, rows, d), F32),
                            pltpu.SemaphoreType.DMA((MOE_OUT_BUFS,))]),
        out_shape=jax.ShapeDtypeStruct((2 * t + MOE_OUT_BUFS * rows, d), F32),
        compiler_params=_cparams(("arbitrary",)),
        name="moe_experts",
    )(blk_exp, inv, xs, w1, w3, w2)


def _combine_kernel(y1_ref, y2_ref, route_ref, x_ref, gt_ref, *rest, final):
    route = route_ref[...]
    y = route[:, 2:3] * y1_ref[...] + route[:, 3:4] * y2_ref[...]
    out = x_ref[...] + gt_ref[0] * y
    if final:
        fn_ref, o_ref = rest
        out = out * lax.rsqrt(jnp.mean(out * out, axis=-1, keepdims=True) + EPS) * fn_ref[...]
    else:
        (o_ref,) = rest
    o_ref[...] = out


def moe_combine(y2, route, x, gate, seq, final_gain=None):
    t, d = x.shape
    tm = 512
    per_b = seq // tm
    final = final_gain is not None
    nt = t // tm
    in_specs = [pl.BlockSpec((tm, d), lambda i: (i, 0)),
                pl.BlockSpec((tm, d), lambda i: (nt + i, 0)),
                pl.BlockSpec((tm, LANES), lambda i: (i, 0)),
                pl.BlockSpec((tm, d), lambda i: (i, 0)),
                pl.BlockSpec((1, 1, d), lambda i: (i // per_b, 0, 0))]
    args = [y2, y2, route, x, gate]
    if final:
        in_specs.append(pl.BlockSpec((1, d), lambda i: (0, 0)))
        args.append(final_gain)
    return pl.pallas_call(
        functools.partial(_combine_kernel, final=final),
        grid=(t // tm,),
        in_specs=in_specs,
        out_specs=pl.BlockSpec((tm, d), lambda i: (i, 0)),
        out_shape=jax.ShapeDtypeStruct((t, d), F32),
        compiler_params=_cparams(("arbitrary",)),
        name="moe_combine",
    )(*args)


def hier_moe_layer(x, g, shift, scale, gate, wg, bg, we, be, w1, w3, w2, layer, seq,
                   final_gain=None):
    t, d = x.shape
    wr = jnp.zeros((d, LANES), F32)
    wr = wr.at[:, :N_GROUPS].set(wg)
    wr = wr.at[:, N_GROUPS:N_GROUPS + N_EXPERTS].set(we.transpose(1, 0, 2).reshape(d, N_EXPERTS))
    br = jnp.zeros((1, LANES), F32)
    br = br.at[0, :N_GROUPS].set(bg).at[0, N_GROUPS:N_GROUPS + N_EXPERTS].set(be.reshape(-1))
    h, route, cnt = moe_router(x, g, shift, scale, wr, br, seq)
    counts = cnt[0, :N_EXPERTS].astype(I32)
    padded = ((counts + MOE_ROWS - 1) // MOE_ROWS) * MOE_ROWS
    pad_end = jnp.cumsum(padded)
    pad_start = pad_end - padded
    eid = route[:, 0:2].astype(I32)
    rank = route[:, 4:6].astype(I32)
    experts = jnp.arange(N_EXPERTS, dtype=I32)
    start_of = jnp.sum(jnp.where(eid[:, :, None] == experts, pad_start, 0), axis=-1)
    dest = (start_of + rank).reshape(-1)
    p_rows = 2 * t + N_EXPERTS * MOE_ROWS
    nblk = p_rows // MOE_ROWS
    blk_row = jnp.arange(nblk, dtype=I32)[:, None] * MOE_ROWS
    blk_exp = jnp.minimum(jnp.sum((pad_end[None, :] <= blk_row).astype(I32), axis=1),
                          N_EXPERTS - 1)
    seg = jnp.concatenate([pad_start + counts, pad_end]).astype(I32)
    xs, inv = moe_dispatch(dest, seg, h, zero_rows(p_rows, d))
    y2 = moe_experts(blk_exp, inv, xs, w1, w3, w2, layer, t)
    return moe_combine(y2, route, x, gate, seq, final_gain)


def _even_weight(w_in):
    d = w_in.shape[0]
    sizes = [512, 512, 512, 512, 4, 4, 512, 512, 512, 256, 64, 4]
    offs = np.concatenate([[0], np.cumsum(sizes)])
    qa, ka, va, za, ba, aa, qb, kb, vb, qi, ki, wi = (w_in[:, offs[j]:offs[j + 1]]
                                                      for j in range(len(sizes)))
    pad = jnp.zeros((d, EV_N - EV_SM - 76), w_in.dtype)
    return jnp.concatenate([qa, ka, va, za, qb, kb, vb, qi, ki, ba, aa, wi, pad], axis=1)


def kernel(x, c, positions, ada_w, ada_b, norm_mix, norm_ffn, even_w_in, gdn_conv_w, gdn_a_log, gdn_dt_bias, gdn_out_norm, even_w_out, gla_w_in, gla_w_gate2, gla_b_gate2, gla_out_norm, gla_w_out, router_group_w, router_group_b, router_exp_w, router_exp_b, exp_w1, exp_w3, exp_w2, final_norm):
    batch, seq, d = x.shape
    depth = ada_w.shape[0]
    t = batch * seq
    xt = x.reshape(t, d)
    mod = ada_modulation(c, ada_w, ada_b)
    tables = rope_tables(positions)
    for layer in range(depth):
        sh_m, sc_m, gt_m, sh_f, sc_f, gt_f = (mod[layer, :, j * d:(j + 1) * d].reshape(batch, 1, d)
                                              for j in range(6))
        i = layer // 2
        g_mix = norm_mix[layer].reshape(1, d)
        if layer % 2 == 0:
            w_in = _even_weight(even_w_in[i]).astype(BF16)
            proj = in_projection(xt, g_mix, sh_m, sc_m, w_in, seq)
            oa = gdn_mixer(proj, gdn_conv_w[i], gdn_a_log[i], gdn_dt_bias[i], gdn_out_norm[i],
                           batch, seq)
            qb, kb, vt, qi, ki, knorm = dsa_prep(proj, tables, seq)
            ob = dsa_attention(proj, qb, kb, vt, qi, ki, knorm, batch, seq)
            w_out = even_w_out[i].astype(BF16)
            half = GDN_HEADS * GDN_DK
            xt = out_projection([oa, ob], [w_out[:half], w_out[half:]], xt, gt_m, seq)
        else:
            w_in = jnp.pad(gla_w_in[i], ((0, 0), (0, OD_N - gla_w_in.shape[2]))).astype(BF16)
            proj = in_projection(xt, g_mix, sh_m, sc_m, w_in, seq)
            o = gla_mixer(proj, gla_w_gate2[i], gla_b_gate2[i], gla_out_norm[i], batch, seq)
            xt = out_projection([o], [gla_w_out[i].astype(BF16)], xt, gt_m, seq)
        xt = hier_moe_layer(xt, norm_ffn[layer].reshape(1, d), sh_f, sc_f, gt_f,
                            router_group_w[layer], router_group_b[layer],
                            router_exp_w[layer], router_exp_b[layer],
                            exp_w1, exp_w3, exp_w2, layer, seq,
                            final_norm.reshape(1, d) if layer == depth - 1 else None)
    return xt.reshape(batch, seq, d)
```
